```python
import jax, jax.numpy as jnp
from jax import lax
import numpy as np

D_MODEL = 1024
BATCH = 32
SEQ = 2048
DEPTH = 4

MEM_LEN = 256
HEAD_DIM = D_MODEL // 16
A_HEADS = 6
B_HEADS = 4
C_HEADS = 6
D_A = A_HEADS * HEAD_DIM
D_B = B_HEADS * HEAD_DIM
D_C = C_HEADS * HEAD_DIM
CONV_A_WIDTH = 31
CONV_C_WIDTH = 3
CHUNK = 128
IN_WIDTH = 2 * D_A + 2 * D_B + 3 * D_C
SPLITS = (D_A, 2 * D_A, 2 * D_A + D_B, 2 * D_A + 2 * D_B,
          2 * D_A + 2 * D_B + D_C, 2 * D_A + 2 * D_B + 2 * D_C)
X_HEADS = 4
X_HEAD_DIM = D_MODEL // X_HEADS
D_FF = 4 * D_MODEL
LN_EPS = 1e-5
DEEPNORM_ALPHA = (2.0 * DEPTH) ** 0.25
DEEPNORM_BETA = (8.0 * DEPTH) ** -0.25

kernel_name = 'hybrid_conv_gmlp_shortconv_deepnorm_trunk'


def _layer_norm(x, g, b):
    xf = x.astype(jnp.float32)
    mu = jnp.mean(xf, axis=-1, keepdims=True)
    var = jnp.mean(jnp.square(xf - mu), axis=-1, keepdims=True)
    y = (xf - mu) * lax.rsqrt(var + LN_EPS)
    return (y * g.astype(jnp.float32) + b.astype(jnp.float32)).astype(x.dtype)


def _causal_depthwise_conv(x, w):
    k, c = w.shape
    return lax.conv_general_dilated(
        x, w[:, None, :].astype(x.dtype), window_strides=(1,), padding=[(k - 1, 0)],
        dimension_numbers=('NWC', 'WIO', 'NWC'), feature_group_count=c)


def _chunked_spatial_gate(u, v, w_s, b_s):
    bsz, s, _ = v.shape
    vh = v.reshape(bsz, s // CHUNK, CHUNK, B_HEADS, HEAD_DIM)
    causal = jnp.tril(jnp.ones((CHUNK, CHUNK), dtype=bool))
    w = jnp.where(causal[None], w_s, jnp.zeros_like(w_s)).astype(v.dtype)
    mixed = jnp.einsum('hts,bcshd->bcthd', w, vh) + b_s.T[None, None, :, :, None].astype(v.dtype)
    return u * mixed.reshape(bsz, s, D_B)


def _hybrid_mixer(h, w_in, conv_a_w, conv_a_b, ln_a_g, ln_a_b, ln_v_g, ln_v_b, w_s, b_s, conv_c_w, w_out):
    proj = h @ w_in
    a_val, a_gate, b_u, b_v, c_b, c_c, c_x = jnp.split(proj, SPLITS, axis=-1)
    a = _causal_depthwise_conv(a_val * jax.nn.sigmoid(a_gate), conv_a_w) + conv_a_b
    a = jax.nn.swish(_layer_norm(a, ln_a_g, ln_a_b))
    u = jax.nn.gelu(b_u, approximate=False)
    v = _layer_norm(jax.nn.gelu(b_v, approximate=False), ln_v_g, ln_v_b)
    bo = _chunked_spatial_gate(u, v, w_s, b_s)
    co = c_b * _causal_depthwise_conv(c_c * c_x, conv_c_w)
    return jnp.concatenate([a, bo, co], axis=-1) @ w_out


def _memory_cross_attention(h, mem, w_q, w_kv, w_o):
    bsz, s, _ = h.shape
    m = mem.shape[1]
    q = (h @ w_q).reshape(bsz, s, X_HEADS, X_HEAD_DIM)
    k, v = jnp.split(mem @ w_kv, 2, axis=-1)
    k = k.reshape(bsz, m, X_HEADS, X_HEAD_DIM)
    v = v.reshape(bsz, m, X_HEADS, X_HEAD_DIM)
    scores = jnp.einsum('bshd,bmhd->bhsm', q.astype(jnp.float32), k.astype(jnp.float32)) * (X_HEAD_DIM ** -0.5)
    p = jax.nn.softmax(scores, axis=-1).astype(h.dtype)
    o = jnp.einsum('bhsm,bmhd->bshd', p, v).reshape(bsz, s, D_MODEL)
    return o @ w_o


def _sq_relu_mlp(h, w_ff1, w_ff2):
    return jnp.square(jax.nn.relu(h @ w_ff1)) @ w_ff2


def _fwd_setup_inputs(seed: int = 0) -> dict:
    key = jax.random.key(seed)
    ks = jax.random.split(key, 26)

    def nrm(k, shape, scale):
        return jax.random.normal(k, shape, dtype=jnp.float32) * scale

    def gain(k, n):
        return 1.0 + nrm(k, (DEPTH, n), 0.02)

    w_kv = jnp.concatenate([nrm(ks[16], (DEPTH, D_MODEL, D_MODEL), D_MODEL ** -0.5),
                            nrm(ks[17], (DEPTH, D_MODEL, D_MODEL), D_MODEL ** -0.5 * DEEPNORM_BETA)], axis=-1)
    return {
        'x': nrm(ks[0], (BATCH, SEQ, D_MODEL), 1.0),
        'mem': nrm(ks[1], (BATCH, MEM_LEN, D_MODEL), 1.0),
        'w_in': nrm(ks[2], (DEPTH, D_MODEL, IN_WIDTH), D_MODEL ** -0.5),
        'conv_a_w': nrm(ks[3], (DEPTH, CONV_A_WIDTH, D_A), CONV_A_WIDTH ** -0.5),
        'conv_a_b': nrm(ks[4], (DEPTH, D_A), 0.02),
        'ln_a_g': gain(ks[5], D_A),
        'ln_a_b': nrm(ks[6], (DEPTH, D_A), 0.02),
        'ln_v_g': gain(ks[7], D_B),
        'ln_v_b': nrm(ks[8], (DEPTH, D_B), 0.02),
        'w_s': nrm(ks[9], (DEPTH, B_HEADS, CHUNK, CHUNK), CHUNK ** -0.5),
        'b_s': 1.0 + nrm(ks[10], (DEPTH, B_HEADS, CHUNK), 0.02),
        'conv_c_w': nrm(ks[11], (DEPTH, CONV_C_WIDTH, D_C), CONV_C_WIDTH ** -0.5),
        'w_out': nrm(ks[12], (DEPTH, D_MODEL, D_MODEL), D_MODEL ** -0.5 * DEEPNORM_BETA),
        'ln1_g': gain(ks[13], D_MODEL),
        'ln1_b': nrm(ks[14], (DEPTH, D_MODEL), 0.02),
        'w_q': nrm(ks[15], (DEPTH, D_MODEL, D_MODEL), D_MODEL ** -0.5),
        'w_kv': w_kv,
        'w_o': nrm(ks[18], (DEPTH, D_MODEL, D_MODEL), D_MODEL ** -0.5 * DEEPNORM_BETA),
        'ln2_g': gain(ks[19], D_MODEL),
        'ln2_b': nrm(ks[20], (DEPTH, D_MODEL), 0.02),
        'w_ff1': nrm(ks[21], (DEPTH, D_MODEL, D_FF), D_MODEL ** -0.5),
        'w_ff2': nrm(ks[22], (DEPTH, D_FF, D_MODEL), D_FF ** -0.5 * DEEPNORM_BETA),
        'ln3_g': gain(ks[23], D_MODEL),
        'ln3_b': nrm(ks[24], (DEPTH, D_MODEL), 0.02),
    }


def _fwd_reference(x, mem, w_in, conv_a_w, conv_a_b, ln_a_g, ln_a_b, ln_v_g, ln_v_b, w_s, b_s, conv_c_w, w_out,
              ln1_g, ln1_b, w_q, w_kv, w_o, ln2_g, ln2_b, w_ff1, w_ff2, ln3_g, ln3_b):
    for l in range(DEPTH):
        mix = _hybrid_mixer(x, w_in[l], conv_a_w[l], conv_a_b[l], ln_a_g[l], ln_a_b[l],
                            ln_v_g[l], ln_v_b[l], w_s[l], b_s[l], conv_c_w[l], w_out[l])
        x = _layer_norm(DEEPNORM_ALPHA * x + mix, ln1_g[l], ln1_b[l])
        att = _memory_cross_attention(x, mem, w_q[l], w_kv[l], w_o[l])
        x = _layer_norm(DEEPNORM_ALPHA * x + att, ln2_g[l], ln2_b[l])
        ff = _sq_relu_mlp(x, w_ff1[l], w_ff2[l])
        x = _layer_norm(DEEPNORM_ALPHA * x + ff, ln3_g[l], ln3_b[l])
    return x


import jax as _jax
import jax.numpy as _jnp

TWIN_FORMAT = 'train_step'
FWD_PARAMS = ['x', 'mem', 'w_in', 'conv_a_w', 'conv_a_b', 'ln_a_g', 'ln_a_b', 'ln_v_g', 'ln_v_b', 'w_s', 'b_s', 'conv_c_w', 'w_out', 'ln1_g', 'ln1_b', 'w_q', 'w_kv', 'w_o', 'ln2_g', 'ln2_b', 'w_ff1', 'w_ff2', 'ln3_g', 'ln3_b']
TWIN_WEIGHTS = ['w_in', 'conv_a_w', 'conv_a_b', 'ln_a_g', 'ln_a_b', 'ln_v_g', 'ln_v_b', 'w_s', 'b_s', 'conv_c_w', 'w_out', 'ln1_g', 'ln1_b', 'w_q', 'w_kv', 'w_o', 'ln2_g', 'ln2_b', 'w_ff1', 'w_ff2', 'ln3_g', 'ln3_b']
TWIN_DIFF_INPUT = 'x'
TWIN_INPUTS = ['x', 'mem', 'w_in', 'conv_a_w', 'conv_a_b', 'ln_a_g', 'ln_a_b', 'ln_v_g', 'ln_v_b', 'w_s', 'b_s', 'conv_c_w', 'w_out', 'ln1_g', 'ln1_b', 'w_q', 'w_kv', 'w_o', 'ln2_g', 'ln2_b', 'w_ff1', 'w_ff2', 'ln3_g', 'ln3_b', 'loss_target', 'm_w_in', 'm_conv_a_w', 'm_conv_a_b', 'm_ln_a_g', 'm_ln_a_b', 'm_ln_v_g', 'm_ln_v_b', 'm_w_s', 'm_b_s', 'm_conv_c_w', 'm_w_out', 'm_ln1_g', 'm_ln1_b', 'm_w_q', 'm_w_kv', 'm_w_o', 'm_ln2_g', 'm_ln2_b', 'm_w_ff1', 'm_w_ff2', 'm_ln3_g', 'm_ln3_b', 'v_w_in', 'v_conv_a_w', 'v_conv_a_b', 'v_ln_a_g', 'v_ln_a_b', 'v_ln_v_g', 'v_ln_v_b', 'v_w_s', 'v_b_s', 'v_conv_c_w', 'v_w_out', 'v_ln1_g', 'v_ln1_b', 'v_w_q', 'v_w_kv', 'v_w_o', 'v_ln2_g', 'v_ln2_b', 'v_w_ff1', 'v_w_ff2', 'v_ln3_g', 'v_ln3_b']
TWIN_OUTPUTS = ['loss', 'grad_x', 'grad_w_in', 'grad_conv_a_w', 'grad_conv_a_b', 'grad_ln_a_g', 'grad_ln_a_b', 'grad_ln_v_g', 'grad_ln_v_b', 'grad_w_s', 'grad_b_s', 'grad_conv_c_w', 'grad_w_out', 'grad_ln1_g', 'grad_ln1_b', 'grad_w_q', 'grad_w_kv', 'grad_w_o', 'grad_ln2_g', 'grad_ln2_b', 'grad_w_ff1', 'grad_w_ff2', 'grad_ln3_g', 'grad_ln3_b', 'delta_w_in', 'delta_conv_a_w', 'delta_conv_a_b', 'delta_ln_a_g', 'delta_ln_a_b', 'delta_ln_v_g', 'delta_ln_v_b', 'delta_w_s', 'delta_b_s', 'delta_conv_c_w', 'delta_w_out', 'delta_ln1_g', 'delta_ln1_b', 'delta_w_q', 'delta_w_kv', 'delta_w_o', 'delta_ln2_g', 'delta_ln2_b', 'delta_w_ff1', 'delta_w_ff2', 'delta_ln3_g', 'delta_ln3_b', 'new_m_w_in', 'new_m_conv_a_w', 'new_m_conv_a_b', 'new_m_ln_a_g', 'new_m_ln_a_b', 'new_m_ln_v_g', 'new_m_ln_v_b', 'new_m_w_s', 'new_m_b_s', 'new_m_conv_c_w', 'new_m_w_out', 'new_m_ln1_g', 'new_m_ln1_b', 'new_m_w_q', 'new_m_w_kv', 'new_m_w_o', 'new_m_ln2_g', 'new_m_ln2_b', 'new_m_w_ff1', 'new_m_w_ff2', 'new_m_ln3_g', 'new_m_ln3_b', 'new_v_w_in', 'new_v_conv_a_w', 'new_v_conv_a_b', 'new_v_ln_a_g', 'new_v_ln_a_b', 'new_v_ln_v_g', 'new_v_ln_v_b', 'new_v_w_s', 'new_v_b_s', 'new_v_conv_c_w', 'new_v_w_out', 'new_v_ln1_g', 'new_v_ln1_b', 'new_v_w_q', 'new_v_w_kv', 'new_v_w_o', 'new_v_ln2_g', 'new_v_ln2_b', 'new_v_w_ff1', 'new_v_w_ff2', 'new_v_ln3_g', 'new_v_ln3_b']
TWIN_LEAF_KINDS = {'loss': 'loss', 'grad_x': 'grad_x', 'grad_w_in': 'grad_w', 'grad_conv_a_w': 'grad_w', 'grad_conv_a_b': 'grad_w', 'grad_ln_a_g': 'grad_w', 'grad_ln_a_b': 'grad_w', 'grad_ln_v_g': 'grad_w', 'grad_ln_v_b': 'grad_w', 'grad_w_s': 'grad_w', 'grad_b_s': 'grad_w', 'grad_conv_c_w': 'grad_w', 'grad_w_out': 'grad_w', 'grad_ln1_g': 'grad_w', 'grad_ln1_b': 'grad_w', 'grad_w_q': 'grad_w', 'grad_w_kv': 'grad_w', 'grad_w_o': 'grad_w', 'grad_ln2_g': 'grad_w', 'grad_ln2_b': 'grad_w', 'grad_w_ff1': 'grad_w', 'grad_w_ff2': 'grad_w', 'grad_ln3_g': 'grad_w', 'grad_ln3_b': 'grad_w', 'delta_w_in': 'delta_w', 'delta_conv_a_w': 'delta_w', 'delta_conv_a_b': 'delta_w', 'delta_ln_a_g': 'delta_w', 'delta_ln_a_b': 'delta_w', 'delta_ln_v_g': 'delta_w', 'delta_ln_v_b': 'delta_w', 'delta_w_s': 'delta_w', 'delta_b_s': 'delta_w', 'delta_conv_c_w': 'delta_w', 'delta_w_out': 'delta_w', 'delta_ln1_g': 'delta_w', 'delta_ln1_b': 'delta_w', 'delta_w_q': 'delta_w', 'delta_w_kv': 'delta_w', 'delta_w_o': 'delta_w', 'delta_ln2_g': 'delta_w', 'delta_ln2_b': 'delta_w', 'delta_w_ff1': 'delta_w', 'delta_w_ff2': 'delta_w', 'delta_ln3_g': 'delta_w', 'delta_ln3_b': 'delta_w', 'new_m_w_in': 'new_m', 'new_m_conv_a_w': 'new_m', 'new_m_conv_a_b': 'new_m', 'new_m_ln_a_g': 'new_m', 'new_m_ln_a_b': 'new_m', 'new_m_ln_v_g': 'new_m', 'new_m_ln_v_b': 'new_m', 'new_m_w_s': 'new_m', 'new_m_b_s': 'new_m', 'new_m_conv_c_w': 'new_m', 'new_m_w_out': 'new_m', 'new_m_ln1_g': 'new_m', 'new_m_ln1_b': 'new_m', 'new_m_w_q': 'new_m', 'new_m_w_kv': 'new_m', 'new_m_w_o': 'new_m', 'new_m_ln2_g': 'new_m', 'new_m_ln2_b': 'new_m', 'new_m_w_ff1': 'new_m', 'new_m_w_ff2': 'new_m', 'new_m_ln3_g': 'new_m', 'new_m_ln3_b': 'new_m', 'new_v_w_in': 'new_v', 'new_v_conv_a_w': 'new_v', 'new_v_conv_a_b': 'new_v', 'new_v_ln_a_g': 'new_v', 'new_v_ln_a_b': 'new_v', 'new_v_ln_v_g': 'new_v', 'new_v_ln_v_b': 'new_v', 'new_v_w_s': 'new_v', 'new_v_b_s': 'new_v', 'new_v_conv_c_w': 'new_v', 'new_v_w_out': 'new_v', 'new_v_ln1_g': 'new_v', 'new_v_ln1_b': 'new_v', 'new_v_w_q': 'new_v', 'new_v_w_kv': 'new_v', 'new_v_w_o': 'new_v', 'new_v_ln2_g': 'new_v', 'new_v_ln2_b': 'new_v', 'new_v_w_ff1': 'new_v', 'new_v_w_ff2': 'new_v', 'new_v_ln3_g': 'new_v', 'new_v_ln3_b': 'new_v'}


def _forward(args):
    return _fwd_reference(*[args[k] for k in FWD_PARAMS])


def _output_shape():
    out = _jax.eval_shape(lambda: _forward(_fwd_setup_inputs(0)))
    return out.shape, out.dtype

N_MICROBATCH = 1
ADAM_LR = 0.001
ADAM_B1 = 0.9
ADAM_B2 = 0.999
ADAM_EPS = 1e-08
ADAM_WD = 0.01
ADAM_STEP = 10
PER_EXAMPLE_BATCH_AXIS = {'x': 0, 'mem': 0, 'loss_target': 0}
SHARED_INPUTS = []
_WEIGHT_DTYPES = {'w_in': _jnp.float32, 'conv_a_w': _jnp.float32, 'conv_a_b': _jnp.float32, 'ln_a_g': _jnp.float32, 'ln_a_b': _jnp.float32, 'ln_v_g': _jnp.float32, 'ln_v_b': _jnp.float32, 'w_s': _jnp.float32, 'b_s': _jnp.float32, 'conv_c_w': _jnp.float32, 'w_out': _jnp.float32, 'ln1_g': _jnp.float32, 'ln1_b': _jnp.float32, 'w_q': _jnp.float32, 'w_kv': _jnp.float32, 'w_o': _jnp.float32, 'ln2_g': _jnp.float32, 'ln2_b': _jnp.float32, 'w_ff1': _jnp.float32, 'w_ff2': _jnp.float32, 'ln3_g': _jnp.float32, 'ln3_b': _jnp.float32}
MOMENT_SCALE = {'w_in': 5.299011e-02, 'conv_a_w': 4.329773e-02, 'conv_a_b': 1.515124e-01, 'ln_a_g': 6.824961e-02, 'ln_a_b': 9.752437e-02, 'ln_v_g': 3.275619e-02, 'ln_v_b': 3.242291e-02, 'w_s': 2.152847e-02, 'b_s': 3.082140e-02, 'conv_c_w': 6.926378e-02, 'w_out': 1.444554e-01, 'ln1_g': 1.419316e+00, 'ln1_b': 8.265096e-01, 'w_q': 2.993452e-03, 'w_kv': 6.316443e-03, 'w_o': 8.356786e-03, 'ln2_g': 1.422305e+00, 'ln2_b': 8.281100e-01, 'w_ff1': 4.605027e-02, 'w_ff2': 2.602705e-01, 'ln3_g': 3.224529e+01, 'ln3_b': 7.825602e+00}


def _to_microbatches(a, axis):
    t = _jnp.moveaxis(a, axis, 0)
    t = t.reshape((N_MICROBATCH, t.shape[0] // N_MICROBATCH) + t.shape[1:])
    return _jnp.moveaxis(t, 1, axis + 1)


def setup_inputs(seed: int = 0) -> dict:
    inp = _fwd_setup_inputs(seed)
    key = _jax.random.fold_in(_jax.random.key(seed), 7919)
    shape, _ = _output_shape()
    out = dict(inp)
    out["loss_target"] = _jax.random.normal(_jax.random.fold_in(key, 0), shape, _jnp.float32)
    for i, name in enumerate(TWIN_WEIGHTS):
        w = inp[name].astype(_jnp.float32)
        if MOMENT_SCALE is None:
            s = _jnp.sqrt(_jnp.mean(_jnp.square(w)) + 1e-30)
        else:
            s = MOMENT_SCALE[name]
        km, kv = _jax.random.split(_jax.random.fold_in(key, i + 1))
        out[name] = w
        out["m_" + name] = s * _jax.random.normal(km, w.shape, _jnp.float32)
        out["v_" + name] = (s * s) * _jax.random.uniform(kv, w.shape, _jnp.float32, 0.5, 1.5)
    if N_MICROBATCH > 1:
        for name, axis in PER_EXAMPLE_BATCH_AXIS.items():
            out[name] = _to_microbatches(out[name], axis)
    return {'x': out['x'], 'mem': out['mem'], 'w_in': out['w_in'], 'conv_a_w': out['conv_a_w'], 'conv_a_b': out['conv_a_b'], 'ln_a_g': out['ln_a_g'], 'ln_a_b': out['ln_a_b'], 'ln_v_g': out['ln_v_g'], 'ln_v_b': out['ln_v_b'], 'w_s': out['w_s'], 'b_s': out['b_s'], 'conv_c_w': out['conv_c_w'], 'w_out': out['w_out'], 'ln1_g': out['ln1_g'], 'ln1_b': out['ln1_b'], 'w_q': out['w_q'], 'w_kv': out['w_kv'], 'w_o': out['w_o'], 'ln2_g': out['ln2_g'], 'ln2_b': out['ln2_b'], 'w_ff1': out['w_ff1'], 'w_ff2': out['w_ff2'], 'ln3_g': out['ln3_g'], 'ln3_b': out['ln3_b'], 'loss_target': out['loss_target'], 'm_w_in': out['m_w_in'], 'm_conv_a_w': out['m_conv_a_w'], 'm_conv_a_b': out['m_conv_a_b'], 'm_ln_a_g': out['m_ln_a_g'], 'm_ln_a_b': out['m_ln_a_b'], 'm_ln_v_g': out['m_ln_v_g'], 'm_ln_v_b': out['m_ln_v_b'], 'm_w_s': out['m_w_s'], 'm_b_s': out['m_b_s'], 'm_conv_c_w': out['m_conv_c_w'], 'm_w_out': out['m_w_out'], 'm_ln1_g': out['m_ln1_g'], 'm_ln1_b': out['m_ln1_b'], 'm_w_q': out['m_w_q'], 'm_w_kv': out['m_w_kv'], 'm_w_o': out['m_w_o'], 'm_ln2_g': out['m_ln2_g'], 'm_ln2_b': out['m_ln2_b'], 'm_w_ff1': out['m_w_ff1'], 'm_w_ff2': out['m_w_ff2'], 'm_ln3_g': out['m_ln3_g'], 'm_ln3_b': out['m_ln3_b'], 'v_w_in': out['v_w_in'], 'v_conv_a_w': out['v_conv_a_w'], 'v_conv_a_b': out['v_conv_a_b'], 'v_ln_a_g': out['v_ln_a_g'], 'v_ln_a_b': out['v_ln_a_b'], 'v_ln_v_g': out['v_ln_v_g'], 'v_ln_v_b': out['v_ln_v_b'], 'v_w_s': out['v_w_s'], 'v_b_s': out['v_b_s'], 'v_conv_c_w': out['v_conv_c_w'], 'v_w_out': out['v_w_out'], 'v_ln1_g': out['v_ln1_g'], 'v_ln1_b': out['v_ln1_b'], 'v_w_q': out['v_w_q'], 'v_w_kv': out['v_w_kv'], 'v_w_o': out['v_w_o'], 'v_ln2_g': out['v_ln2_g'], 'v_ln2_b': out['v_ln2_b'], 'v_w_ff1': out['v_w_ff1'], 'v_w_ff2': out['v_w_ff2'], 'v_ln3_g': out['v_ln3_g'], 'v_ln3_b': out['v_ln3_b']}


def _loss(weights, diff, rest, loss_target):
    with _jax.named_scope("forward"):
        args = {**rest, TWIN_DIFF_INPUT: diff, **{k: w.astype(_WEIGHT_DTYPES[k]) for k, w in weights.items()}}
        y = _forward(args)
    with _jax.named_scope("loss_head"):
        err = _jnp.square(y.astype(_jnp.float32) - loss_target)
        return 0.5 * _jnp.sum(_jnp.mean(err, axis=-1)) if err.ndim else 0.5 * err


def _adamw(w, g, m, v):
    m = ADAM_B1 * m + (1.0 - ADAM_B1) * g
    v = ADAM_B2 * v + (1.0 - ADAM_B2) * _jnp.square(g)
    m_hat = m / (1.0 - ADAM_B1 ** ADAM_STEP)
    v_hat = v / (1.0 - ADAM_B2 ** ADAM_STEP)
    delta = -ADAM_LR * (m_hat / (_jnp.sqrt(v_hat) + ADAM_EPS) + ADAM_WD * w)
    return delta, m, v


def reference(x, mem, w_in, conv_a_w, conv_a_b, ln_a_g, ln_a_b, ln_v_g, ln_v_b, w_s, b_s, conv_c_w, w_out, ln1_g, ln1_b, w_q, w_kv, w_o, ln2_g, ln2_b, w_ff1, w_ff2, ln3_g, ln3_b, loss_target, m_w_in, m_conv_a_w, m_conv_a_b, m_ln_a_g, m_ln_a_b, m_ln_v_g, m_ln_v_b, m_w_s, m_b_s, m_conv_c_w, m_w_out, m_ln1_g, m_ln1_b, m_w_q, m_w_kv, m_w_o, m_ln2_g, m_ln2_b, m_w_ff1, m_w_ff2, m_ln3_g, m_ln3_b, v_w_in, v_conv_a_w, v_conv_a_b, v_ln_a_g, v_ln_a_b, v_ln_v_g, v_ln_v_b, v_w_s, v_b_s, v_conv_c_w, v_w_out, v_ln1_g, v_ln1_b, v_w_q, v_w_kv, v_w_o, v_ln2_g, v_ln2_b, v_w_ff1, v_w_ff2, v_ln3_g, v_ln3_b):
    given = dict(x=x, mem=mem, w_in=w_in, conv_a_w=conv_a_w, conv_a_b=conv_a_b, ln_a_g=ln_a_g, ln_a_b=ln_a_b, ln_v_g=ln_v_g, ln_v_b=ln_v_b, w_s=w_s, b_s=b_s, conv_c_w=conv_c_w, w_out=w_out, ln1_g=ln1_g, ln1_b=ln1_b, w_q=w_q, w_kv=w_kv, w_o=w_o, ln2_g=ln2_g, ln2_b=ln2_b, w_ff1=w_ff1, w_ff2=w_ff2, ln3_g=ln3_g, ln3_b=ln3_b, loss_target=loss_target, m_w_in=m_w_in, m_conv_a_w=m_conv_a_w, m_conv_a_b=m_conv_a_b, m_ln_a_g=m_ln_a_g, m_ln_a_b=m_ln_a_b, m_ln_v_g=m_ln_v_g, m_ln_v_b=m_ln_v_b, m_w_s=m_w_s, m_b_s=m_b_s, m_conv_c_w=m_conv_c_w, m_w_out=m_w_out, m_ln1_g=m_ln1_g, m_ln1_b=m_ln1_b, m_w_q=m_w_q, m_w_kv=m_w_kv, m_w_o=m_w_o, m_ln2_g=m_ln2_g, m_ln2_b=m_ln2_b, m_w_ff1=m_w_ff1, m_w_ff2=m_w_ff2, m_ln3_g=m_ln3_g, m_ln3_b=m_ln3_b, v_w_in=v_w_in, v_conv_a_w=v_conv_a_w, v_conv_a_b=v_conv_a_b, v_ln_a_g=v_ln_a_g, v_ln_a_b=v_ln_a_b, v_ln_v_g=v_ln_v_g, v_ln_v_b=v_ln_v_b, v_w_s=v_w_s, v_b_s=v_b_s, v_conv_c_w=v_conv_c_w, v_w_out=v_w_out, v_ln1_g=v_ln1_g, v_ln1_b=v_ln1_b, v_w_q=v_w_q, v_w_kv=v_w_kv, v_w_o=v_w_o, v_ln2_g=v_ln2_g, v_ln2_b=v_ln2_b, v_w_ff1=v_w_ff1, v_w_ff2=v_w_ff2, v_ln3_g=v_ln3_g, v_ln3_b=v_ln3_b)
    weights = {n: given[n] for n in TWIN_WEIGHTS}
    shared = {n: given[n] for n in SHARED_INPUTS}
    per_example = {n: given[n] for n in ['x', 'mem']}
    grad_fn = _jax.value_and_grad(_loss, argnums=(0, 1))

    def one_microbatch(ex, loss_target):
        ex = dict(ex)
        diff = ex.pop(TWIN_DIFF_INPUT)
        return grad_fn(weights, diff, {**shared, **ex}, loss_target)

    if N_MICROBATCH == 1:
        loss, (grad_w, grad_x) = one_microbatch(per_example, given["loss_target"])
    else:
        def body(carry, xs):
            loss_sum, grad_sum = carry
            l_k, (gw_k, gx_k) = one_microbatch(xs[0], xs[1])
            with _jax.named_scope("update"):
                return (loss_sum + l_k, _jax.tree.map(_jnp.add, grad_sum, gw_k)), gx_k

        init = (_jnp.zeros((), _jnp.float32), _jax.tree.map(_jnp.zeros_like, weights))
        (loss, grad_w), grad_x = _jax.lax.scan(body, init, (per_example, given["loss_target"]))
    with _jax.named_scope("update"):
        delta_w, new_m, new_v = {}, {}, {}
        for n in TWIN_WEIGHTS:
            delta_w[n], new_m[n], new_v[n] = _adamw(weights[n], grad_w[n], given["m_" + n], given["v_" + n])
    return (loss, grad_x, *[grad_w[n] for n in TWIN_WEIGHTS], *[delta_w[n] for n in TWIN_WEIGHTS],
            *[new_m[n] for n in TWIN_WEIGHTS], *[new_v[n] for n in TWIN_WEIGHTS])
```

```python
import functools
import math

import jax
import jax.numpy as jnp
from jax import lax
from jax.experimental import pallas as pl
from jax.experimental.pallas import tpu as pltpu

F32 = jnp.float32
BF16 = jnp.bfloat16

DEPTH = 4
D = 1024
D_A, D_B, D_C = 384, 256, 384
HEAD = 64
B_HEADS = 4
CHUNK = 128
KA, KC = 31, 3
HALO_A, HALO_C = 32, 8
IN_W = 2 * D_A + 2 * D_B + 3 * D_C
X_HEADS = 4
X_HD = D // X_HEADS
D_FF = 4 * D
EPS = 1e-5
ALPHA = (2.0 * DEPTH) ** 0.25
LR, B1, B2, ADAM_EPS, WD, STEP = 0.001, 0.9, 0.999, 1e-08, 0.01, 10
INV_SQRT2 = 0.7071067811865476
INV_SQRT_2PI = 0.3989422804014327
N_DEV = 8
VMEM_LIMIT = 56 * 1024 * 1024
MESH = pl.DeviceIdType.MESH
ANY = pl.BlockSpec(memory_space=pl.ANY)


def _cp(sem=None):
    return pltpu.CompilerParams(dimension_semantics=sem, vmem_limit_bytes=VMEM_LIMIT)


def _sds(shape, dtype):
    return jax.ShapeDtypeStruct(tuple(shape), dtype)


def _ln(z):
    mu = jnp.mean(z, axis=-1, keepdims=True)
    zc = z - mu
    var = jnp.mean(zc * zc, axis=-1, keepdims=True)
    rstd = lax.rsqrt(var + EPS)
    return zc * rstd, rstd


def _ln_bwd(dxhat, xhat, rstd):
    m1 = jnp.mean(dxhat, axis=-1, keepdims=True)
    m2 = jnp.mean(dxhat * xhat, axis=-1, keepdims=True)
    return rstd * (dxhat - m1 - xhat * m2)


def _gelu(x):
    return 0.5 * x * (1.0 + lax.erf(x * INV_SQRT2))


def _gelu_grad(x):
    return 0.5 * (1.0 + lax.erf(x * INV_SQRT2)) + x * jnp.exp(-0.5 * x * x) * INV_SQRT_2PI


def _fold8(x):
    r, c = x.shape
    return jnp.sum(x.reshape(r // 8, 8, c), axis=0)


def _relu2(h):
    return jnp.square(jnp.maximum(h, 0.0))


def _mm_nn(a, w, out_dtype, name, tm=512, tn=None):
    n, k = a.shape
    m = w.shape[1]
    tm = min(tm, n)
    tn = m if tn is None else min(tn, m)

    def kern(a_ref, w_ref, o_ref):
        o_ref[...] = jnp.dot(a_ref[...].astype(BF16), w_ref[...], preferred_element_type=F32).astype(out_dtype)

    return pl.pallas_call(
        kern, name=name, grid=(n // tm, m // tn),
        in_specs=[pl.BlockSpec((tm, k), lambda i, j: (i, 0)), pl.BlockSpec((k, tn), lambda i, j: (0, j))],
        out_specs=pl.BlockSpec((tm, tn), lambda i, j: (i, j)),
        out_shape=_sds((n, m), out_dtype),
        compiler_params=_cp(("parallel", "parallel")))(a, w)


def _mm_res_ln(a, w, res, g, b, name, relu2=False, tm=512):
    n, k = a.shape
    tm = min(tm, n)

    def kern(a_ref, w_ref, res_ref, g_ref, b_ref, x_ref, xhat_ref, rstd_ref):
        av = a_ref[...]
        if relu2:
            av = _relu2(av.astype(F32))
        z = ALPHA * res_ref[...] + jnp.dot(av.astype(BF16), w_ref[...], preferred_element_type=F32)
        xhat, rstd = _ln(z)
        xhat_ref[...] = xhat
        rstd_ref[...] = rstd
        x_ref[...] = xhat * g_ref[...] + b_ref[...]

    row = lambda i: (i, 0)
    fix = lambda i: (0, 0)
    return pl.pallas_call(
        kern, name=name, grid=(n // tm,),
        in_specs=[pl.BlockSpec((tm, k), row), pl.BlockSpec((k, D), fix), pl.BlockSpec((tm, D), row),
                  pl.BlockSpec((1, D), fix), pl.BlockSpec((1, D), fix)],
        out_specs=[pl.BlockSpec((tm, D), row), pl.BlockSpec((tm, D), row), pl.BlockSpec((tm, 1), row)],
        out_shape=[_sds((n, D), F32), _sds((n, D), F32), _sds((n, 1), F32)],
        compiler_params=_cp(("parallel",)))(a, w, res, g, b)


def _mm_nt(a, w, out_dtype, name, relu2_grad_of=None, tm=512, tn=1024):
    n, k = a.shape
    m = w.shape[0]
    tm = min(tm, n)
    tn = min(tn, m)
    with_h = relu2_grad_of is not None

    def kern(*refs):
        a_ref, w_ref = refs[0], refs[1]
        o_ref = refs[-1]
        r = lax.dot_general(a_ref[...].astype(BF16), w_ref[...], (((1,), (1,)), ((), ())), preferred_element_type=F32)
        if with_h:
            r = r * (2.0 * jnp.maximum(refs[2][...].astype(F32), 0.0))
        o_ref[...] = r.astype(out_dtype)

    in_specs = [pl.BlockSpec((tm, k), lambda i, j: (i, 0)), pl.BlockSpec((tn, k), lambda i, j: (j, 0))]
    args = [a, w]
    if with_h:
        in_specs.append(pl.BlockSpec((tm, tn), lambda i, j: (i, j)))
        args.append(relu2_grad_of)
    return pl.pallas_call(
        kern, name=name, grid=(n // tm, m // tn), in_specs=in_specs,
        out_specs=pl.BlockSpec((tm, tn), lambda i, j: (i, j)),
        out_shape=_sds((n, m), out_dtype),
        compiler_params=_cp(("parallel", "parallel")))(*args)


def _mm_tn(a, b, name, relu2=False, t1=1024, tn=2048, tk=512):
    n, k1 = a.shape
    m = b.shape[1]
    t1 = min(t1, k1)
    tn = m if m <= 2432 and m % tn else min(tn, m)
    tk = min(tk, n)

    def kern(a_ref, b_ref, o_ref):
        @pl.when(pl.program_id(2) == 0)
        def _():
            o_ref[...] = jnp.zeros_like(o_ref)

        av = a_ref[...]
        if relu2:
            av = _relu2(av.astype(F32))
        o_ref[...] += lax.dot_general(av.astype(BF16), b_ref[...].astype(BF16), (((0,), (0,)), ((), ())),
                                      preferred_element_type=F32)

    return pl.pallas_call(
        kern, name=name, grid=(k1 // t1, m // tn, n // tk),
        in_specs=[pl.BlockSpec((tk, t1), lambda i, j, k: (k, i)), pl.BlockSpec((tk, tn), lambda i, j, k: (k, j))],
        out_specs=pl.BlockSpec((t1, tn), lambda i, j, k: (i, j)),
        out_shape=_sds((k1, m), F32),
        compiler_params=_cp(("parallel", "parallel", "arbitrary")))(a, b)


def _bwd_in(dz_next, da, w, name, ln=None, tm=512):
    n, k2 = da.shape
    tm = min(tm, n)
    row = lambda i: (i, 0)
    fix = lambda i: (0, 0)

    def dx_of(dzn_ref, da_ref, w_ref):
        return ALPHA * dzn_ref[...] + lax.dot_general(da_ref[...], w_ref[...], (((1,), (1,)), ((), ())),
                                                      preferred_element_type=F32)

    base_specs = [pl.BlockSpec((tm, D), row), pl.BlockSpec((tm, k2), row), pl.BlockSpec((D, k2), fix)]
    if ln is None:
        def kern(dzn_ref, da_ref, w_ref, dx_ref):
            dx_ref[...] = dx_of(dzn_ref, da_ref, w_ref)

        return pl.pallas_call(
            kern, name=name, grid=(n // tm,), in_specs=base_specs, out_specs=pl.BlockSpec((tm, D), row),
            out_shape=_sds((n, D), F32), compiler_params=_cp(("parallel",)))(dz_next, da, w)

    xhat, rstd, g = ln

    def kern(dzn_ref, da_ref, w_ref, xhat_ref, rstd_ref, g_ref, dz_ref, dg_ref, db_ref):
        @pl.when(pl.program_id(0) == 0)
        def _():
            dg_ref[...] = jnp.zeros_like(dg_ref)
            db_ref[...] = jnp.zeros_like(db_ref)

        dx = dx_of(dzn_ref, da_ref, w_ref)
        xh = xhat_ref[...]
        dg_ref[...] += _fold8(dx * xh)
        db_ref[...] += _fold8(dx)
        dz_ref[...] = _ln_bwd(dx * g_ref[...], xh, rstd_ref[...])

    return pl.pallas_call(
        kern, name=name, grid=(n // tm,),
        in_specs=base_specs + [pl.BlockSpec((tm, D), row), pl.BlockSpec((tm, 1), row), pl.BlockSpec((1, D), fix)],
        out_specs=[pl.BlockSpec((tm, D), row), pl.BlockSpec((8, D), fix), pl.BlockSpec((8, D), fix)],
        out_shape=[_sds((n, D), F32), _sds((8, D), F32), _sds((8, D), F32)],
        compiler_params=_cp(("arbitrary",)))(dz_next, da, w, xhat, rstd, g)


def _loss_lnbwd(x, target, xhat, rstd, g, name, tm=512):
    n = x.shape[0]
    tm = min(tm, n)
    row = lambda i: (i, 0)
    fix = lambda i: (0, 0)

    def kern(x_ref, t_ref, xhat_ref, rstd_ref, g_ref, sq_ref, dz_ref, dg_ref, db_ref):
        @pl.when(pl.program_id(0) == 0)
        def _():
            sq_ref[...] = jnp.zeros_like(sq_ref)
            dg_ref[...] = jnp.zeros_like(dg_ref)
            db_ref[...] = jnp.zeros_like(db_ref)

        err = x_ref[...] - t_ref[...]
        sq_ref[...] += _fold8(err * err)
        dx = err * (1.0 / D)
        xh = xhat_ref[...]
        dg_ref[...] += _fold8(dx * xh)
        db_ref[...] += _fold8(dx)
        dz_ref[...] = _ln_bwd(dx * g_ref[...], xh, rstd_ref[...])

    return pl.pallas_call(
        kern, name=name, grid=(n // tm,),
        in_specs=[pl.BlockSpec((tm, D), row), pl.BlockSpec((tm, D), row), pl.BlockSpec((tm, D), row),
                  pl.BlockSpec((tm, 1), row), pl.BlockSpec((1, D), fix)],
        out_specs=[pl.BlockSpec((8, D), fix), pl.BlockSpec((tm, D), row), pl.BlockSpec((8, D), fix),
                   pl.BlockSpec((8, D), fix)],
        out_shape=[_sds((8, D), F32), _sds((n, D), F32), _sds((8, D), F32), _sds((8, D), F32)],
        compiler_params=_cp(("arbitrary",)))(x, target, xhat, rstd, g)


def _softmax_rows(s):
    s = s - jnp.max(s, axis=-1, keepdims=True)
    e = jnp.exp(s)
    return e / jnp.sum(e, axis=-1, keepdims=True)


def _attn_fwd(q, kv, bd, name, tm=512):
    n = q.shape[0]
    s_len = n // bd
    m_len = kv.shape[0] // bd
    tm = min(tm, s_len)
    nt = s_len // tm
    scale = X_HD ** -0.5

    def kern(q_ref, k_ref, v_ref, o_ref):
        for h in range(X_HEADS):
            cs = slice(h * X_HD, (h + 1) * X_HD)
            s = lax.dot_general(q_ref[:, cs], k_ref[:, cs], (((1,), (1,)), ((), ())), preferred_element_type=F32)
            p = _softmax_rows(s * scale)
            o_ref[:, cs] = jnp.dot(p.astype(BF16), v_ref[:, cs], preferred_element_type=F32).astype(BF16)

    return pl.pallas_call(
        kern, name=name, grid=(bd, nt),
        in_specs=[pl.BlockSpec((tm, D), lambda b, i: (b * nt + i, 0)),
                  pl.BlockSpec((m_len, D), lambda b, i: (b, 0)), pl.BlockSpec((m_len, D), lambda b, i: (b, 1))],
        out_specs=pl.BlockSpec((tm, D), lambda b, i: (b * nt + i, 0)),
        out_shape=_sds((n, D), BF16),
        compiler_params=_cp(("parallel", "parallel")))(q, kv, kv)


def _attn_bwd(q, kv, do, bd, name, tm=512):
    n = q.shape[0]
    s_len = n // bd
    m_len = kv.shape[0] // bd
    tm = min(tm, s_len)
    nt = s_len // tm
    scale = X_HD ** -0.5

    def kern(q_ref, k_ref, v_ref, do_ref, dq_ref, dkv_ref):
        @pl.when(pl.program_id(1) == 0)
        def _():
            dkv_ref[...] = jnp.zeros_like(dkv_ref)

        for h in range(X_HEADS):
            cs = slice(h * X_HD, (h + 1) * X_HD)
            vs = slice(D + h * X_HD, D + (h + 1) * X_HD)
            qh, kh, vh, doh = q_ref[:, cs], k_ref[:, cs], v_ref[:, cs], do_ref[:, cs]
            s = lax.dot_general(qh, kh, (((1,), (1,)), ((), ())), preferred_element_type=F32)
            p = _softmax_rows(s * scale)
            pb = p.astype(BF16)
            dp = lax.dot_general(doh, vh, (((1,), (1,)), ((), ())), preferred_element_type=F32)
            dkv_ref[:, vs] += lax.dot_general(pb, doh, (((0,), (0,)), ((), ())), preferred_element_type=F32)
            ds = (p * (dp - jnp.sum(dp * p, axis=-1, keepdims=True)) * scale).astype(BF16)
            dq_ref[:, cs] = jnp.dot(ds, kh, preferred_element_type=F32).astype(BF16)
            dkv_ref[:, cs] += lax.dot_general(ds, qh, (((0,), (0,)), ((), ())), preferred_element_type=F32)

    return pl.pallas_call(
        kern, name=name, grid=(bd, nt),
        in_specs=[pl.BlockSpec((tm, D), lambda b, i: (b * nt + i, 0)),
                  pl.BlockSpec((m_len, D), lambda b, i: (b, 0)), pl.BlockSpec((m_len, D), lambda b, i: (b, 1)),
                  pl.BlockSpec((tm, D), lambda b, i: (b * nt + i, 0))],
        out_specs=[pl.BlockSpec((tm, D), lambda b, i: (b * nt + i, 0)),
                   pl.BlockSpec((m_len, 2 * D), lambda b, i: (b, 0))],
        out_shape=[_sds((n, D), BF16), _sds((bd * m_len, 2 * D), F32)],
        compiler_params=_cp(("parallel", "arbitrary")))(q, kv, kv, do)


C_AV, C_AG, C_BU, C_BV, C_CB, C_CC, C_CX = 0, 384, 768, 1024, 1280, 1664, 2048


def _causal_conv(win_ref, r0, halo, taps, w_ref, n_taps):
    parts = []
    off = halo - (n_taps - 1)
    for cb in range(3):
        cs = slice(cb * 128, (cb + 1) * 128)
        win = win_ref[pl.ds(r0, CHUNK + halo), cs]
        acc = jnp.zeros((CHUNK, 128), F32)
        for k in range(n_taps):
            acc = acc + win[off + k:off + k + CHUNK, :] * w_ref[k:k + 1, cs]
        parts.append(acc)
    return jnp.concatenate(parts, axis=1)


def _anticausal_conv(win_ref, r0, halo, w_ref, n_taps):
    parts = []
    for cb in range(3):
        cs = slice(cb * 128, (cb + 1) * 128)
        win = win_ref[pl.ds(r0, CHUNK + halo), cs]
        acc = jnp.zeros((CHUNK, 128), F32)
        for k in range(n_taps):
            o = n_taps - 1 - k
            acc = acc + win[o:o + CHUNK, :] * w_ref[k:k + 1, cs]
        parts.append(acc)
    return jnp.concatenate(parts, axis=1)


def _head_of_lane():
    return lax.broadcasted_iota(jnp.int32, (1, D_B), 1) // HEAD


def _spatial_mix(wm_ref, vb, head):
    mixed = jnp.zeros((CHUNK, D_B), F32)
    for h in range(B_HEADS):
        mh = jnp.dot(wm_ref[h], vb, preferred_element_type=F32)
        mixed = jnp.where(head == h, mh, mixed)
    return mixed


def _mixer_fwd(proj, p, bd, name):
    n = proj.shape[0]
    s_len = n // bd
    n_chunks = s_len // CHUNK

    def kern(proj_ref, caw_ref, cab_ref, lag_ref, lab_ref, lvg_ref, lvb_ref, wm_ref, bsx_ref, ccw_ref, cat_ref,
             gs_ref, ccs_ref):
        gs_ref[0:HALO_A, :] = jnp.zeros((HALO_A, D_A), F32)
        ccs_ref[0:HALO_C, :] = jnp.zeros((HALO_C, D_C), F32)
        head = _head_of_lane()

        def chunk(i, carry):
            r0 = pl.multiple_of(i * CHUNK, CHUNK)
            rows = pl.ds(r0, CHUNK)
            ld = lambda c0, w: proj_ref[rows, c0:c0 + w].astype(F32)
            gs_ref[pl.ds(r0 + HALO_A, CHUNK), :] = ld(C_AV, D_A) * jax.nn.sigmoid(ld(C_AG, D_A))
            ca = _causal_conv(gs_ref, r0, HALO_A, None, caw_ref, KA) + cab_ref[...]
            lna = _ln(ca)[0] * lag_ref[...] + lab_ref[...]
            cat_ref[rows, 0:D_A] = (lna * jax.nn.sigmoid(lna)).astype(BF16)
            u = _gelu(ld(C_BU, D_B))
            v = _ln(_gelu(ld(C_BV, D_B)))[0] * lvg_ref[...] + lvb_ref[...]
            mixed = _spatial_mix(wm_ref, v.astype(BF16), head) + bsx_ref[...]
            cat_ref[rows, D_A:D_A + D_B] = (u * mixed).astype(BF16)
            ccs_ref[pl.ds(r0 + HALO_C, CHUNK), :] = ld(C_CC, D_C) * ld(C_CX, D_C)
            conv = _causal_conv(ccs_ref, r0, HALO_C, None, ccw_ref, KC)
            cat_ref[rows, D_A + D_B:D] = (ld(C_CB, D_C) * conv).astype(BF16)
            return carry

        lax.fori_loop(0, n_chunks, chunk, 0)

    fix2 = lambda b: (0, 0)
    args = [proj, p["caw"], p["cab"], p["lag"], p["lab"], p["lvg"], p["lvb"], p["wm"], p["bsx"], p["ccw"]]
    in_specs = [pl.BlockSpec((s_len, IN_W), lambda b: (b, 0))]
    for a in args[1:]:
        in_specs.append(pl.BlockSpec(a.shape, (lambda b: (0, 0, 0)) if a.ndim == 3 else fix2))
    return pl.pallas_call(
        kern, name=name, grid=(bd,), in_specs=in_specs,
        out_specs=pl.BlockSpec((s_len, D), lambda b: (b, 0)),
        out_shape=_sds((n, D), BF16),
        scratch_shapes=[pltpu.VMEM((s_len + HALO_A, D_A), F32), pltpu.VMEM((s_len + HALO_C, D_C), F32)],
        compiler_params=_cp(("parallel",)))(*args)


def _mixer_bwd(proj, dcat, p, bd, name):
    n = proj.shape[0]
    s_len = n // bd
    n_chunks = s_len // CHUNK

    def kern(proj_ref, dcat_ref, caw_ref, cab_ref, lag_ref, lab_ref, lvg_ref, lvb_ref, wm_ref, bsx_ref, ccw_ref,
             dproj_ref, dcaw_ref, dcab_ref, dlag_ref, dlab_ref, dlvg_ref, dlvb_ref, dws_ref, dbs_ref, dccw_ref,
             gs_ref, dcas_ref, ccs_ref, dcs_ref, a_caw, a_cab, a_lag, a_lab, a_lvg, a_lvb, a_ccw):
        gs_ref[0:HALO_A, :] = jnp.zeros((HALO_A, D_A), F32)
        ccs_ref[0:HALO_C, :] = jnp.zeros((HALO_C, D_C), F32)
        dcas_ref[s_len:s_len + HALO_A, :] = jnp.zeros((HALO_A, D_A), F32)
        dcs_ref[s_len:s_len + HALO_C, :] = jnp.zeros((HALO_C, D_C), F32)
        for acc in (a_caw, a_cab, a_lag, a_lab, a_lvg, a_lvb, a_ccw, dws_ref, dbs_ref):
            acc[...] = jnp.zeros_like(acc)
        head = _head_of_lane()
        lane128 = lax.broadcasted_iota(jnp.int32, (1, CHUNK), 1)

        def pass1(i, carry):
            r0 = pl.multiple_of(i * CHUNK, CHUNK)
            rows = pl.ds(r0, CHUNK)
            ld = lambda c0, w: proj_ref[rows, c0:c0 + w].astype(F32)
            dld = lambda c0, w: dcat_ref[rows, c0:c0 + w].astype(F32)
            gs_ref[pl.ds(r0 + HALO_A, CHUNK), :] = ld(C_AV, D_A) * jax.nn.sigmoid(ld(C_AG, D_A))
            ca = _causal_conv(gs_ref, r0, HALO_A, None, caw_ref, KA) + cab_ref[...]
            xh, rstd = _ln(ca)
            lna = xh * lag_ref[...] + lab_ref[...]
            sg = jax.nn.sigmoid(lna)
            dlna = dld(0, D_A) * (sg * (1.0 + lna * (1.0 - sg)))
            a_lag[...] += _fold8(dlna * xh)
            a_lab[...] += _fold8(dlna)
            dca = _ln_bwd(dlna * lag_ref[...], xh, rstd)
            dcas_ref[rows, :] = dca
            a_cab[...] += _fold8(dca)
            for cb in range(3):
                cs = slice(cb * 128, (cb + 1) * 128)
                win = gs_ref[pl.ds(r0, CHUNK + HALO_A), cs]
                dcab = dca[:, cs]
                for k in range(KA):
                    o = HALO_A - (KA - 1) + k
                    a_caw[k * 8:(k + 1) * 8, cs] += _fold8(dcab * win[o:o + CHUNK, :])
            pu, pv = ld(C_BU, D_B), ld(C_BV, D_B)
            u = _gelu(pu)
            vxh, vrstd = _ln(_gelu(pv))
            v = vxh * lvg_ref[...] + lvb_ref[...]
            vb = v.astype(BF16)
            mixed = _spatial_mix(wm_ref, vb, head) + bsx_ref[...]
            dbo = dld(D_A, D_B)
            dproj_ref[rows, C_BU:C_BU + D_B] = (dbo * mixed * _gelu_grad(pu)).astype(BF16)
            dmixed = dbo * u
            dv = jnp.zeros((CHUNK, D_B), F32)
            bsum = jnp.zeros((CHUNK, CHUNK), F32)
            for h in range(B_HEADS):
                dmh = jnp.where(head == h, dmixed, 0.0)
                dmb = dmh.astype(BF16)
                dvh = lax.dot_general(wm_ref[h], dmb, (((0,), (0,)), ((), ())), preferred_element_type=F32)
                dv = jnp.where(head == h, dvh, dv)
                dws_ref[h] += lax.dot_general(dmb, vb, (((1,), (1,)), ((), ())), preferred_element_type=F32)
                bsum = bsum + jnp.where(lane128 == h, jnp.sum(dmh, axis=-1, keepdims=True), 0.0)
            dbs_ref[...] += bsum
            a_lvg[...] += _fold8(dv * vxh)
            a_lvb[...] += _fold8(dv)
            dgv = _ln_bwd(dv * lvg_ref[...], vxh, vrstd)
            dproj_ref[rows, C_BV:C_BV + D_B] = (dgv * _gelu_grad(pv)).astype(BF16)
            ccs_ref[pl.ds(r0 + HALO_C, CHUNK), :] = ld(C_CC, D_C) * ld(C_CX, D_C)
            conv = _causal_conv(ccs_ref, r0, HALO_C, None, ccw_ref, KC)
            dco = dld(D_A + D_B, D_C)
            dproj_ref[rows, C_CB:C_CB + D_C] = (dco * conv).astype(BF16)
            dconv = dco * ld(C_CB, D_C)
            dcs_ref[rows, :] = dconv
            for cb in range(3):
                cs = slice(cb * 128, (cb + 1) * 128)
                win = ccs_ref[pl.ds(r0, CHUNK + HALO_C), cs]
                for k in range(KC):
                    o = HALO_C - (KC - 1) + k
                    a_ccw[k * 8:(k + 1) * 8, cs] += _fold8(dconv[:, cs] * win[o:o + CHUNK, :])
            return carry

        lax.fori_loop(0, n_chunks, pass1, 0)

        def pass2(i, carry):
            r0 = pl.multiple_of(i * CHUNK, CHUNK)
            rows = pl.ds(r0, CHUNK)
            ld = lambda c0, w: proj_ref[rows, c0:c0 + w].astype(F32)
            dg = _anticausal_conv(dcas_ref, r0, HALO_A, caw_ref, KA)
            pa = ld(C_AV, D_A)
            sg = jax.nn.sigmoid(ld(C_AG, D_A))
            dproj_ref[rows, C_AV:C_AV + D_A] = (dg * sg).astype(BF16)
            dproj_ref[rows, C_AG:C_AG + D_A] = (dg * pa * sg * (1.0 - sg)).astype(BF16)
            dcc = _anticausal_conv(dcs_ref, r0, HALO_C, ccw_ref, KC)
            dproj_ref[rows, C_CC:C_CC + D_C] = (dcc * ld(C_CX, D_C)).astype(BF16)
            dproj_ref[rows, C_CX:C_CX + D_C] = (dcc * ld(C_CC, D_C)).astype(BF16)
            return carry

        lax.fori_loop(0, n_chunks, pass2, 0)

        for k in range(KA):
            dcaw_ref[k:k + 1, :] = jnp.sum(a_caw[k * 8:(k + 1) * 8, :], axis=0, keepdims=True)
        dcaw_ref[KA:KA + 1, :] = jnp.zeros((1, D_A), F32)
        for k in range(8):
            if k < KC:
                dccw_ref[k:k + 1, :] = jnp.sum(a_ccw[k * 8:(k + 1) * 8, :], axis=0, keepdims=True)
            else:
                dccw_ref[k:k + 1, :] = jnp.zeros((1, D_C), F32)
        dcab_ref[...] = a_cab[...]
        dlag_ref[...] = a_lag[...]
        dlab_ref[...] = a_lab[...]
        dlvg_ref[...] = a_lvg[...]
        dlvb_ref[...] = a_lvb[...]

    fix2 = lambda b: (0, 0)
    args = [proj, dcat, p["caw"], p["cab"], p["lag"], p["lab"], p["lvg"], p["lvb"], p["wm"], p["bsx"], p["ccw"]]
    once = pl.Buffered(1)
    in_specs = [pl.BlockSpec((s_len, IN_W), lambda b: (b, 0), pipeline_mode=once),
                pl.BlockSpec((s_len, D), lambda b: (b, 0), pipeline_mode=once)]
    for a in args[2:]:
        in_specs.append(pl.BlockSpec(a.shape, (lambda b: (0, 0, 0)) if a.ndim == 3 else fix2))

    def per_seq(*shape):
        nd = len(shape)
        return (pl.BlockSpec((None,) + shape, lambda b: (b,) + (0,) * nd), _sds((bd,) + shape, F32))

    outs = [(pl.BlockSpec((s_len, IN_W), lambda b: (b, 0), pipeline_mode=once), _sds((n, IN_W), BF16)),
            per_seq(32, D_A), per_seq(8, D_A), per_seq(8, D_A), per_seq(8, D_A), per_seq(8, D_B), per_seq(8, D_B),
            per_seq(B_HEADS, CHUNK, CHUNK), per_seq(CHUNK, CHUNK), per_seq(8, D_C)]
    return pl.pallas_call(
        kern, name=name, grid=(bd,), in_specs=in_specs,
        out_specs=[o[0] for o in outs], out_shape=[o[1] for o in outs],
        scratch_shapes=[pltpu.VMEM((s_len + HALO_A, D_A), F32), pltpu.VMEM((s_len + HALO_A, D_A), F32),
                        pltpu.VMEM((s_len + HALO_C, D_C), F32), pltpu.VMEM((s_len + HALO_C, D_C), F32),
                        pltpu.VMEM((KA * 8, D_A), F32), pltpu.VMEM((8, D_A), F32), pltpu.VMEM((8, D_A), F32),
                        pltpu.VMEM((8, D_A), F32), pltpu.VMEM((8, D_B), F32), pltpu.VMEM((8, D_B), F32),
                        pltpu.VMEM((KC * 8, D_C), F32)],
        compiler_params=_cp(("parallel",)))(*args)


def _adamw(w, m, v, g_parts, name):
    shape = w.shape
    c = shape[-1]
    r = math.prod(shape[:-1])
    as2d = lambda a: a.reshape(r, c)
    tr = 512 if r % 512 == 0 else r
    n_g = g_parts.shape[0]

    def kern(*refs):
        w_ref, m_ref, v_ref = refs[:3]
        g_refs = refs[3:3 + n_g]
        go_ref, d_ref, mo_ref, vo_ref = refs[3 + n_g:]
        g = g_refs[0][...]
        for gr in g_refs[1:]:
            g = g + gr[...]
        mn = B1 * m_ref[...] + (1.0 - B1) * g
        vn = B2 * v_ref[...] + (1.0 - B2) * jnp.square(g)
        m_hat = mn / (1.0 - B1 ** STEP)
        v_hat = vn / (1.0 - B2 ** STEP)
        go_ref[...] = g
        mo_ref[...] = mn
        vo_ref[...] = vn
        d_ref[...] = -LR * (m_hat / (jnp.sqrt(v_hat) + ADAM_EPS) + WD * w_ref[...])

    spec = pl.BlockSpec((tr, c), lambda i: (i, 0))
    g_specs = [pl.BlockSpec((None, tr, c), functools.partial(lambda p, i: (p, i, 0), p)) for p in range(n_g)]
    g3d = g_parts.reshape(n_g, r, c)
    outs = pl.pallas_call(
        kern, name=name, grid=(r // tr,), in_specs=[spec] * 3 + g_specs, out_specs=[spec] * 4,
        out_shape=[_sds((r, c), F32)] * 4, compiler_params=_cp(("parallel",)))(
            as2d(w), as2d(m), as2d(v), *([g3d] * n_g))
    return [o.reshape(shape) for o in outs]


def _place():
    x, y, c = lax.axis_index("x"), lax.axis_index("y"), lax.axis_index("c")
    other_chips = [(1 - x, y), (x, 1 - y), (1 - x, 1 - y)]
    return x, y, c, other_chips


def _all_gather(arrays, name):
    n_arr = len(arrays)

    def body(*refs):
        ins, outs = refs[:n_arr], refs[n_arr:2 * n_arr]
        send_sems, recv_sems, local_sems = refs[2 * n_arr:]
        x, y, c, chips = _place()
        me, sibling = (x, y, c), (x, y, 1 - c)
        slot = lambda px, py, pc: 4 * px + 2 * py + pc

        def copy(a, k, block, to, from_input=False):
            dst = outs[a].at[slot(*block)]
            return pltpu.make_async_remote_copy(
                src_ref=ins[a] if from_input else dst, dst_ref=dst, send_sem=send_sems.at[k, a],
                recv_sem=recv_sems.at[k, a], device_id=to, device_id_type=MESH)

        mine = [pltpu.make_async_copy(ins[a], outs[a].at[slot(*me)], local_sems.at[a]) for a in range(n_arr)]
        for cp in mine:
            cp.start()
        first = []
        for a in range(n_arr):
            first.append(copy(a, 0, me, sibling, True))
            first += [copy(a, 1 + j, me, (*chip, c), True) for j, chip in enumerate(chips)]
        for cp in first:
            cp.start()
        passed = []
        for j, chip in enumerate(chips):
            for a in range(n_arr):
                copy(a, 1 + j, (*chip, c), me).wait_recv()
                passed.append(copy(a, 4 + j, (*chip, c), sibling))
                passed[-1].start()
        for a in range(n_arr):
            copy(a, 0, sibling, me).wait_recv()
            for j, chip in enumerate(chips):
                copy(a, 4 + j, (*chip, 1 - c), me).wait_recv()
        for cp in first + passed:
            cp.wait_send()
        for cp in mine:
            cp.wait()

    return pl.pallas_call(
        body, name=name, in_specs=[ANY] * n_arr, out_specs=[ANY] * n_arr,
        out_shape=[_sds((N_DEV,) + a.shape, a.dtype) for a in arrays],
        scratch_shapes=[pltpu.SemaphoreType.DMA((7, n_arr)), pltpu.SemaphoreType.DMA((7, n_arr)),
                        pltpu.SemaphoreType.DMA((n_arr,))])(*arrays)


def _pair_exchange(arrays, name):
    n_arr = len(arrays)

    def body(*refs):
        ins, outs = refs[:n_arr], refs[n_arr:2 * n_arr]
        send_sems, recv_sems = refs[2 * n_arr:]
        x, y, c, _ = _place()
        copies = [pltpu.make_async_remote_copy(
            src_ref=ins[a].at[1 - c], dst_ref=outs[a], send_sem=send_sems.at[a], recv_sem=recv_sems.at[a],
            device_id=(x, y, 1 - c), device_id_type=MESH) for a in range(n_arr)]
        for cp in copies:
            cp.start()
        for cp in copies:
            cp.wait()

    return pl.pallas_call(
        body, name=name, in_specs=[ANY] * n_arr, out_specs=[ANY] * n_arr,
        out_shape=[_sds(a.shape[1:], a.dtype) for a in arrays],
        scratch_shapes=[pltpu.SemaphoreType.DMA((n_arr,)), pltpu.SemaphoreType.DMA((n_arr,))])(*arrays)


def _pair_add(mine, theirs, core, name):
    _, r, c = mine.shape
    tr = 512 if r % 512 == 0 else r

    def kern(core_ref, a_ref, b_ref, o_ref):
        o_ref[...] = a_ref[...] + b_ref[...]

    return pl.pallas_call(
        kern, name=name,
        grid_spec=pltpu.PrefetchScalarGridSpec(
            num_scalar_prefetch=1, grid=(r // tr,),
            in_specs=[pl.BlockSpec((None, tr, c), lambda i, core_ref: (core_ref[0], i, 0)),
                      pl.BlockSpec((tr, c), lambda i, core_ref: (i, 0))],
            out_specs=pl.BlockSpec((tr, c), lambda i, core_ref: (i, 0))),
        out_shape=_sds((r, c), F32), compiler_params=_cp(("parallel",)))(core, mine, theirs)


def _chip_scatter(arrays, name):
    n_arr = len(arrays)

    def body(*refs):
        ins, outs = refs[:n_arr], refs[n_arr:2 * n_arr]
        send_sems, recv_sems, local_sems = refs[2 * n_arr:]
        x, y, c, chips = _place()
        own = [pltpu.make_async_copy(ins[a].at[2 * x + y], outs[a].at[0], local_sems.at[a]) for a in range(n_arr)]
        for cp in own:
            cp.start()
        copies = []
        for a in range(n_arr):
            for j, (px, py) in enumerate(chips):
                copies.append(pltpu.make_async_remote_copy(
                    src_ref=ins[a].at[2 * px + py], dst_ref=outs[a].at[1 + j], send_sem=send_sems.at[j, a],
                    recv_sem=recv_sems.at[j, a], device_id=(px, py, c), device_id_type=MESH))
        for cp in copies:
            cp.start()
        for cp in copies:
            cp.wait()
        for cp in own:
            cp.wait()

    return pl.pallas_call(
        body, name=name, in_specs=[ANY] * n_arr, out_specs=[ANY] * n_arr,
        out_shape=[_sds(a.shape, a.dtype) for a in arrays],
        scratch_shapes=[pltpu.SemaphoreType.DMA((3, n_arr)), pltpu.SemaphoreType.DMA((3, n_arr)),
                        pltpu.SemaphoreType.DMA((n_arr,))])(*arrays)


def _mixer_params(conv_a_w, conv_a_b, ln_a_g, ln_a_b, ln_v_g, ln_v_b, w_s, b_s, conv_c_w):
    causal = jnp.tril(jnp.ones((CHUNK, CHUNK), dtype=bool))
    row = lambda a: a.reshape(1, -1)
    return dict(
        caw=jnp.pad(conv_a_w, ((0, 32 - KA), (0, 0))), cab=row(conv_a_b), lag=row(ln_a_g), lab=row(ln_a_b),
        lvg=row(ln_v_g), lvb=row(ln_v_b), wm=jnp.where(causal[None], w_s, 0.0).astype(BF16),
        bsx=jnp.repeat(b_s.T, HEAD, axis=1), ccw=jnp.pad(conv_c_w, ((0, 8 - KC), (0, 0))))


def _local_step(x, mem, target, big, small, bd):
    row = lambda a: a.reshape(1, -1)
    saved = []
    for l in range(DEPTH):
        mp = _mixer_params(*[small[k][l] for k in ("conv_a_w", "conv_a_b", "ln_a_g", "ln_a_b", "ln_v_g", "ln_v_b",
                                                    "w_s", "b_s", "conv_c_w")])
        proj = _mm_nn(x, big["w_in"][l], BF16, f"in_proj_{l}")
        cat = _mixer_fwd(proj, mp, bd, f"mixer_fwd_{l}")
        x1, xh1, rs1 = _mm_res_ln(cat, big["w_out"][l], x, row(small["ln1_g"][l]), row(small["ln1_b"][l]),
                                  f"out_proj_ln1_{l}")
        q = _mm_nn(x1, big["w_q"][l], BF16, f"q_proj_{l}")
        kv = _mm_nn(mem, big["w_kv"][l], BF16, f"kv_proj_{l}")
        o = _attn_fwd(q, kv, bd, f"attn_fwd_{l}")
        x2, xh2, rs2 = _mm_res_ln(o, big["w_o"][l], x1, row(small["ln2_g"][l]), row(small["ln2_b"][l]),
                                  f"o_proj_ln2_{l}")
        h = _mm_nn(x2, big["w_ff1"][l], BF16, f"ff1_{l}", tn=1024)
        x3, xh3, rs3 = _mm_res_ln(h, big["w_ff2"][l], x2, row(small["ln3_g"][l]), row(small["ln3_b"][l]),
                                  f"ff2_ln3_{l}", relu2=True)
        saved.append(dict(mp=mp, x=x, proj=proj, cat=cat, x1=x1, xh1=xh1, rs1=rs1, q=q, kv=kv, o=o, x2=x2, xh2=xh2,
                          rs2=rs2, h=h, xh3=xh3, rs3=rs3))
        x = x3

    names = ("w_in", "conv_a_w", "conv_a_b", "ln_a_g", "ln_a_b", "ln_v_g", "ln_v_b", "w_s", "b_s", "conv_c_w", "w_out",
             "ln1_g", "ln1_b", "w_q", "w_kv", "w_o", "ln2_g", "ln2_b", "w_ff1", "w_ff2", "ln3_g", "ln3_b")
    grads = {k: [None] * DEPTH for k in names}
    s = saved[-1]
    sq, dz3, dg, db = _loss_lnbwd(x, target, s["xh3"], s["rs3"], row(small["ln3_g"][DEPTH - 1]), "loss_ln3_bwd")
    grad_x = None
    causal = jnp.tril(jnp.ones((CHUNK, CHUNK), dtype=bool))
    for l in reversed(range(DEPTH)):
        s = saved[l]
        grads["ln3_g"][l], grads["ln3_b"][l] = jnp.sum(dg, axis=0), jnp.sum(db, axis=0)
        dh = _mm_nt(dz3, big["w_ff2"][l], BF16, f"ff2_bwd_{l}", relu2_grad_of=s["h"])
        grads["w_ff2"][l] = _mm_tn(s["h"], dz3, f"ff2_wgrad_{l}", relu2=True, t1=2048)
        grads["w_ff1"][l] = _mm_tn(s["x2"], dh, f"ff1_wgrad_{l}")
        dz2, dg, db = _bwd_in(dz3, dh, big["w_ff1"][l], f"ff1_bwd_ln2_{l}",
                              ln=(s["xh2"], s["rs2"], row(small["ln2_g"][l])))
        grads["ln2_g"][l], grads["ln2_b"][l] = jnp.sum(dg, axis=0), jnp.sum(db, axis=0)
        do = _mm_nt(dz2, big["w_o"][l], BF16, f"o_proj_bwd_{l}")
        grads["w_o"][l] = _mm_tn(s["o"], dz2, f"o_proj_wgrad_{l}")
        dq, dkv = _attn_bwd(s["q"], s["kv"], do, bd, f"attn_bwd_{l}")
        grads["w_q"][l] = _mm_tn(s["x1"], dq, f"q_wgrad_{l}")
        grads["w_kv"][l] = _mm_tn(mem, dkv, f"kv_wgrad_{l}")
        dz1, dg, db = _bwd_in(dz2, dq, big["w_q"][l], f"q_bwd_ln1_{l}", ln=(s["xh1"], s["rs1"], row(small["ln1_g"][l])))
        grads["ln1_g"][l], grads["ln1_b"][l] = jnp.sum(dg, axis=0), jnp.sum(db, axis=0)
        dcat = _mm_nt(dz1, big["w_out"][l], BF16, f"out_proj_bwd_{l}")
        grads["w_out"][l] = _mm_tn(s["cat"], dz1, f"out_proj_wgrad_{l}")
        (dproj, dcaw, dcab, dlag, dlab, dlvg, dlvb, dws, dbs, dccw) = _mixer_bwd(s["proj"], dcat, s["mp"], bd,
                                                                                 f"mixer_bwd_{l}")
        grads["conv_a_w"][l] = jnp.sum(dcaw, axis=0)[:KA]
        grads["conv_a_b"][l] = jnp.sum(dcab, axis=(0, 1))
        grads["ln_a_g"][l] = jnp.sum(dlag, axis=(0, 1))
        grads["ln_a_b"][l] = jnp.sum(dlab, axis=(0, 1))
        grads["ln_v_g"][l] = jnp.sum(dlvg, axis=(0, 1))
        grads["ln_v_b"][l] = jnp.sum(dlvb, axis=(0, 1))
        grads["w_s"][l] = jnp.where(causal[None], jnp.sum(dws, axis=0), 0.0)
        grads["b_s"][l] = jnp.sum(dbs, axis=0)[:, :B_HEADS].T
        grads["conv_c_w"][l] = jnp.sum(dccw, axis=0)[:KC]
        grads["w_in"][l] = _mm_tn(s["x"], dproj, f"in_proj_wgrad_{l}")
        if l > 0:
            p = saved[l - 1]
            dz3, dg, db = _bwd_in(dz1, dproj, big["w_in"][l], f"in_proj_bwd_ln3_{l}",
                                  ln=(p["xh3"], p["rs3"], row(small["ln3_g"][l - 1])))
        else:
            grad_x = _bwd_in(dz1, dproj, big["w_in"][l], "in_proj_bwd_0")
    return sq, grad_x, grads


WEIGHTS = ("w_in", "conv_a_w", "conv_a_b", "ln_a_g", "ln_a_b", "ln_v_g", "ln_v_b", "w_s", "b_s", "conv_c_w", "w_out",
           "ln1_g", "ln1_b", "w_q", "w_kv", "w_o", "ln2_g", "ln2_b", "w_ff1", "w_ff2", "ln3_g", "ln3_b")
COL_SHARDED = ("w_in", "w_kv", "w_ff1")
ROW_SHARDED = ("w_out", "w_q", "w_o", "w_ff2")
BIG = COL_SHARDED + ROW_SHARDED
REPLICATED = tuple(k for k in WEIGHTS if k not in BIG and k not in ("conv_a_w", "conv_c_w"))
PACK_LANES = 128


def _gathered_to_full(g, col_sharded):
    _, depth, a, b = g.shape
    if col_sharded:
        return g.transpose(1, 2, 0, 3).reshape(depth, a, N_DEV * b)
    return g.transpose(1, 0, 2, 3).reshape(depth, N_DEV * a, b)


def _full_to_owner_major(g, col_sharded):
    depth, a, b = g.shape
    if col_sharded:
        return g.reshape(depth, a, 4, 2, b // N_DEV).transpose(3, 2, 0, 1, 4)
    return g.reshape(depth, 4, 2, a // N_DEV, b).transpose(2, 1, 0, 3, 4)


def _pack_rows(parts):
    flat = jnp.concatenate([p.reshape(-1, PACK_LANES) for p in parts], axis=0)
    return jnp.pad(flat, ((0, -flat.shape[0] % 8), (0, 0)))


def _unpack_rows(packed, like):
    out, r = [], 0
    for p in like:
        n = p.size // PACK_LANES
        out.append(packed[r:r + n].reshape(p.shape))
        r += n
    return out


def kernel(x, mem, w_in, conv_a_w, conv_a_b, ln_a_g, ln_a_b, ln_v_g, ln_v_b, w_s, b_s, conv_c_w, w_out, ln1_g, ln1_b, w_q, w_kv, w_o, ln2_g, ln2_b, w_ff1, w_ff2, ln3_g, ln3_b, loss_target, m_w_in, m_conv_a_w, m_conv_a_b, m_ln_a_g, m_ln_a_b, m_ln_v_g, m_ln_v_b, m_w_s, m_b_s, m_conv_c_w, m_w_out, m_ln1_g, m_ln1_b, m_w_q, m_w_kv, m_w_o, m_ln2_g, m_ln2_b, m_w_ff1, m_w_ff2, m_ln3_g, m_ln3_b, v_w_in, v_conv_a_w, v_conv_a_b, v_ln_a_g, v_ln_a_b, v_ln_v_g, v_ln_v_b, v_w_s, v_b_s, v_conv_c_w, v_w_out, v_ln1_g, v_ln1_b, v_w_q, v_w_kv, v_w_o, v_ln2_g, v_ln2_b, v_w_ff1, v_w_ff2, v_ln3_g, v_ln3_b):
    given = dict(locals())
    w = {k: given[k] for k in WEIGHTS}
    mom = {k: given["m_" + k] for k in WEIGHTS}
    var = {k: given["v_" + k] for k in WEIGHTS}
    bd, s_len, _ = x.shape
    core = lax.axis_index("c").astype(jnp.int32).reshape(1)

    conv_pack = lambda d: jnp.concatenate([d["conv_a_w"], d["conv_c_w"]], axis=1)
    gathered = _all_gather([w[k].astype(BF16) for k in BIG] + [conv_pack(w)], "weights_all_gather")
    big = {}
    for k, g in zip(BIG, gathered):
        full = _gathered_to_full(g, k in COL_SHARDED)
        big[k] = [full[l] for l in range(DEPTH)]
    convs = _gathered_to_full(gathered[-1], True)
    small = {k: w[k] for k in REPLICATED}
    small["conv_a_w"], small["conv_c_w"] = convs[:, :KA], convs[:, KA:]

    sq, grad_x, grads = _local_step(x.reshape(bd * s_len, D), mem.reshape(-1, D), loss_target.reshape(bd * s_len, D),
                                    big, small, bd)
    loss = lax.psum(0.5 * jnp.sum(sq) / D, ("x", "y", "c"))

    full_grads = [_full_to_owner_major(jnp.stack(grads[k]), k in COL_SHARDED) for k in BIG]
    full_grads.append(_full_to_owner_major(
        jnp.concatenate([jnp.stack(grads["conv_a_w"]), jnp.stack(grads["conv_c_w"])], axis=1), True))
    from_sibling = _pair_exchange(full_grads, "grad_pair_exchange")
    chip_sums = []
    for i, (g, r) in enumerate(zip(full_grads, from_sibling)):
        rows, cols = math.prod(r.shape[:-1]), r.shape[-1]
        chip_sums.append(_pair_add(g.reshape(2, rows, cols), r.reshape(rows, cols), core,
                                   f"grad_pair_add_{i}").reshape(r.shape))
    parts = _chip_scatter(chip_sums, "grad_chip_scatter")
    out = {}
    for k, p in zip(BIG, parts):
        out[k] = _adamw(w[k], mom[k], var[k], p, f"adamw_{k}")
    conv_out = _adamw(conv_pack(w), conv_pack(mom), conv_pack(var), parts[-1], "adamw_conv")
    out["conv_a_w"] = [o[:, :KA] for o in conv_out]
    out["conv_c_w"] = [o[:, KA:] for o in conv_out]

    rep_grads = _all_gather([_pack_rows([jnp.stack(grads[k]) for k in REPLICATED])], "replicated_grads_all_gather")[0]
    rep_out = _adamw(_pack_rows([w[k] for k in REPLICATED]), _pack_rows([mom[k] for k in REPLICATED]),
                     _pack_rows([var[k] for k in REPLICATED]), rep_grads, "adamw_replicated")
    for i, o in enumerate(rep_out):
        for k, piece in zip(REPLICATED, _unpack_rows(o, [w[k] for k in REPLICATED])):
            out.setdefault(k, [None] * 4)[i] = piece

    res = [loss, grad_x.reshape(bd, s_len, D)]
    for i in range(4):
        res += [out[k][i] for k in WEIGHTS]
    return tuple(res)
```

```python
import functools
import math

import jax
import jax.numpy as jnp
from jax import lax
from jax.experimental import pallas as pl
from jax.experimental.pallas import tpu as pltpu

F32 = jnp.float32
BF16 = jnp.bfloat16

DEPTH = 4
D = 1024
D_A, D_B, D_C = 384, 256, 384
HEAD = 64
B_HEADS = 4
CHUNK = 128
KA, KC = 31, 3
HALO_A, HALO_C = 32, 8
IN_W = 2 * D_A + 2 * D_B + 3 * D_C
X_HEADS = 4
X_HD = D // X_HEADS
D_FF = 4 * D
EPS = 1e-5
ALPHA = (2.0 * DEPTH) ** 0.25
LR, B1, B2, ADAM_EPS, WD, STEP = 0.001, 0.9, 0.999, 1e-08, 0.01, 10
INV_SQRT2 = 0.7071067811865476
INV_SQRT_2PI = 0.3989422804014327
N_DEV = 8
VMEM_LIMIT = 56 * 1024 * 1024
MESH = pl.DeviceIdType.MESH
ANY = pl.BlockSpec(memory_space=pl.ANY)


def _cp(sem=None):
    return pltpu.CompilerParams(dimension_semantics=sem, vmem_limit_bytes=VMEM_LIMIT)


def _sds(shape, dtype):
    return jax.ShapeDtypeStruct(tuple(shape), dtype)


class _Rider:
    def __init__(self, arrays, out_shape, sems, start, finish):
        self.arrays, self.out_shape, self.sems, self.start, self.finish = arrays, out_shape, sems, start, finish


def _call(kern, name, grid, in_specs, out_specs, out_shape, args, sem, scratch=(), rider=None):
    single = not isinstance(out_shape, (list, tuple))
    out_specs_l = [out_specs] if single else list(out_specs)
    out_shape_l = [out_shape] if single else list(out_shape)
    if rider is None:
        res = pl.pallas_call(kern, name=name, grid=grid, in_specs=in_specs, out_specs=out_specs_l,
                             out_shape=out_shape_l, scratch_shapes=list(scratch), compiler_params=_cp(sem))(*args)
        return (res[0] if single else list(res)), None
    n_in, n_out, n_scr = len(args), len(out_shape_l), len(scratch)
    n_rin, n_rout = len(rider.arrays), len(rider.out_shape)

    def body(*refs):
        ins, refs = refs[:n_in], refs[n_in:]
        r_ins, refs = refs[:n_rin], refs[n_rin:]
        outs, refs = refs[:n_out], refs[n_out:]
        r_outs, refs = refs[:n_rout], refs[n_rout:]
        scr, r_sems = refs[:n_scr], refs[n_scr:]
        ids = [pl.program_id(d) for d in range(len(grid))]
        first = functools.reduce(jnp.logical_and, [i == 0 for i in ids])
        last = functools.reduce(jnp.logical_and, [i == g - 1 for i, g in zip(ids, grid)])

        @pl.when(first)
        def _():
            rider.start(r_ins, r_outs, r_sems)

        kern(*ins, *outs, *scr)

        @pl.when(last)
        def _():
            rider.finish(r_ins, r_outs, r_sems)

    res = pl.pallas_call(
        body, name=name, grid=grid, in_specs=list(in_specs) + [ANY] * n_rin,
        out_specs=out_specs_l + [ANY] * n_rout, out_shape=out_shape_l + list(rider.out_shape),
        scratch_shapes=list(scratch) + list(rider.sems),
        compiler_params=_cp(("arbitrary",) * len(grid)))(*args, *rider.arrays)
    mine, theirs = list(res[:n_out]), list(res[n_out:])
    return (mine[0] if single else mine), theirs


def _ride_alone(rider, name):
    def body(*refs):
        n_rin, n_rout = len(rider.arrays), len(rider.out_shape)
        r_ins, r_outs, r_sems = refs[:n_rin], refs[n_rin:n_rin + n_rout], refs[n_rin + n_rout:]
        rider.start(r_ins, r_outs, r_sems)
        rider.finish(r_ins, r_outs, r_sems)

    return list(pl.pallas_call(
        body, name=name, in_specs=[ANY] * len(rider.arrays), out_specs=[ANY] * len(rider.out_shape),
        out_shape=list(rider.out_shape), scratch_shapes=list(rider.sems))(*rider.arrays))


def _ln(z):
    mu = jnp.mean(z, axis=-1, keepdims=True)
    zc = z - mu
    var = jnp.mean(zc * zc, axis=-1, keepdims=True)
    rstd = lax.rsqrt(var + EPS)
    return zc * rstd, rstd


def _ln_bwd(dxhat, xhat, rstd):
    m1 = jnp.mean(dxhat, axis=-1, keepdims=True)
    m2 = jnp.mean(dxhat * xhat, axis=-1, keepdims=True)
    return rstd * (dxhat - m1 - xhat * m2)


def _gelu(x):
    return 0.5 * x * (1.0 + lax.erf(x * INV_SQRT2))


def _gelu_grad(x):
    return 0.5 * (1.0 + lax.erf(x * INV_SQRT2)) + x * jnp.exp(-0.5 * x * x) * INV_SQRT_2PI


def _fold8(x):
    r, c = x.shape
    return jnp.sum(x.reshape(r // 8, 8, c), axis=0)


def _relu2(h):
    return jnp.square(jnp.maximum(h, 0.0))


def _mm_nn(a, w, out_dtype, name, tm=512, tn=None, rider=None):
    n, k = a.shape
    m = w.shape[1]
    tm = min(tm, n)
    tn = m if tn is None else min(tn, m)

    def kern(a_ref, w_ref, o_ref):
        o_ref[...] = jnp.dot(a_ref[...].astype(BF16), w_ref[...], preferred_element_type=F32).astype(out_dtype)

    res, rode = _call(
        kern, name, (n // tm, m // tn),
        [pl.BlockSpec((tm, k), lambda i, j: (i, 0)), pl.BlockSpec((k, tn), lambda i, j: (0, j))],
        pl.BlockSpec((tm, tn), lambda i, j: (i, j)), _sds((n, m), out_dtype), (a, w), ("parallel", "parallel"),
        rider=rider)
    return res if rider is None else (res, rode)


def _mm_res_ln(a, w, res, g, b, name, relu2=False, tm=512):
    n, k = a.shape
    tm = min(tm, n)

    def kern(a_ref, w_ref, res_ref, g_ref, b_ref, x_ref, xhat_ref, rstd_ref):
        av = a_ref[...]
        if relu2:
            av = _relu2(av.astype(F32))
        z = ALPHA * res_ref[...] + jnp.dot(av.astype(BF16), w_ref[...], preferred_element_type=F32)
        xhat, rstd = _ln(z)
        xhat_ref[...] = xhat
        rstd_ref[...] = rstd
        x_ref[...] = xhat * g_ref[...] + b_ref[...]

    row = lambda i: (i, 0)
    fix = lambda i: (0, 0)
    return pl.pallas_call(
        kern, name=name, grid=(n // tm,),
        in_specs=[pl.BlockSpec((tm, k), row), pl.BlockSpec((k, D), fix), pl.BlockSpec((tm, D), row),
                  pl.BlockSpec((1, D), fix), pl.BlockSpec((1, D), fix)],
        out_specs=[pl.BlockSpec((tm, D), row), pl.BlockSpec((tm, D), row), pl.BlockSpec((tm, 1), row)],
        out_shape=[_sds((n, D), F32), _sds((n, D), F32), _sds((n, 1), F32)],
        compiler_params=_cp(("parallel",)))(a, w, res, g, b)


def _mm_nt(a, w, out_dtype, name, relu2_grad_of=None, tm=512, tn=1024, rider=None):
    n, k = a.shape
    m = w.shape[0]
    tm = min(tm, n)
    tn = min(tn, m)
    with_h = relu2_grad_of is not None

    def kern(*refs):
        a_ref, w_ref = refs[0], refs[1]
        o_ref = refs[-1]
        r = lax.dot_general(a_ref[...].astype(BF16), w_ref[...], (((1,), (1,)), ((), ())), preferred_element_type=F32)
        if with_h:
            r = r * (2.0 * jnp.maximum(refs[2][...].astype(F32), 0.0))
        o_ref[...] = r.astype(out_dtype)

    in_specs = [pl.BlockSpec((tm, k), lambda i, j: (i, 0)), pl.BlockSpec((tn, k), lambda i, j: (j, 0))]
    args = [a, w]
    if with_h:
        in_specs.append(pl.BlockSpec((tm, tn), lambda i, j: (i, j)))
        args.append(relu2_grad_of)
    res, rode = _call(kern, name, (n // tm, m // tn), in_specs, pl.BlockSpec((tm, tn), lambda i, j: (i, j)),
                      _sds((n, m), out_dtype), args, ("parallel", "parallel"), rider=rider)
    return res if rider is None else (res, rode)


def _mm_tn(a, b, name, relu2=False, t1=1024, tn=2048, tk=512):
    n, k1 = a.shape
    m = b.shape[1]
    t1 = min(t1, k1)
    tn = m if m <= 2432 and m % tn else min(tn, m)
    tk = min(tk, n)

    def kern(a_ref, b_ref, o_ref):
        @pl.when(pl.program_id(2) == 0)
        def _():
            o_ref[...] = jnp.zeros_like(o_ref)

        av = a_ref[...]
        if relu2:
            av = _relu2(av.astype(F32))
        o_ref[...] += lax.dot_general(av.astype(BF16), b_ref[...].astype(BF16), (((0,), (0,)), ((), ())),
                                      preferred_element_type=F32)

    return pl.pallas_call(
        kern, name=name, grid=(k1 // t1, m // tn, n // tk),
        in_specs=[pl.BlockSpec((tk, t1), lambda i, j, k: (k, i)), pl.BlockSpec((tk, tn), lambda i, j, k: (k, j))],
        out_specs=pl.BlockSpec((t1, tn), lambda i, j, k: (i, j)),
        out_shape=_sds((k1, m), F32),
        compiler_params=_cp(("parallel", "parallel", "arbitrary")))(a, b)


def _bwd_in(dz_next, da, w, name, ln=None, tm=512):
    n, k2 = da.shape
    tm = min(tm, n)
    row = lambda i: (i, 0)
    fix = lambda i: (0, 0)

    def dx_of(dzn_ref, da_ref, w_ref):
        return ALPHA * dzn_ref[...] + lax.dot_general(da_ref[...], w_ref[...], (((1,), (1,)), ((), ())),
                                                      preferred_element_type=F32)

    base_specs = [pl.BlockSpec((tm, D), row), pl.BlockSpec((tm, k2), row), pl.BlockSpec((D, k2), fix)]
    if ln is None:
        def kern(dzn_ref, da_ref, w_ref, dx_ref):
            dx_ref[...] = dx_of(dzn_ref, da_ref, w_ref)

        return pl.pallas_call(
            kern, name=name, grid=(n // tm,), in_specs=base_specs, out_specs=pl.BlockSpec((tm, D), row),
            out_shape=_sds((n, D), F32), compiler_params=_cp(("parallel",)))(dz_next, da, w)

    xhat, rstd, g = ln

    def kern(dzn_ref, da_ref, w_ref, xhat_ref, rstd_ref, g_ref, dz_ref, dg_ref, db_ref):
        @pl.when(pl.program_id(0) == 0)
        def _():
            dg_ref[...] = jnp.zeros_like(dg_ref)
            db_ref[...] = jnp.zeros_like(db_ref)

        dx = dx_of(dzn_ref, da_ref, w_ref)
        xh = xhat_ref[...]
        dg_ref[...] += _fold8(dx * xh)
        db_ref[...] += _fold8(dx)
        dz_ref[...] = _ln_bwd(dx * g_ref[...], xh, rstd_ref[...])

    return pl.pallas_call(
        kern, name=name, grid=(n // tm,),
        in_specs=base_specs + [pl.BlockSpec((tm, D), row), pl.BlockSpec((tm, 1), row), pl.BlockSpec((1, D), fix)],
        out_specs=[pl.BlockSpec((tm, D), row), pl.BlockSpec((8, D), fix), pl.BlockSpec((8, D), fix)],
        out_shape=[_sds((n, D), F32), _sds((8, D), F32), _sds((8, D), F32)],
        compiler_params=_cp(("arbitrary",)))(dz_next, da, w, xhat, rstd, g)


def _loss_lnbwd(x, target, xhat, rstd, g, name, tm=512):
    n = x.shape[0]
    tm = min(tm, n)
    row = lambda i: (i, 0)
    fix = lambda i: (0, 0)

    def kern(x_ref, t_ref, xhat_ref, rstd_ref, g_ref, sq_ref, dz_ref, dg_ref, db_ref):
        @pl.when(pl.program_id(0) == 0)
        def _():
            sq_ref[...] = jnp.zeros_like(sq_ref)
            dg_ref[...] = jnp.zeros_like(dg_ref)
            db_ref[...] = jnp.zeros_like(db_ref)

        err = x_ref[...] - t_ref[...]
        sq_ref[...] += _fold8(err * err)
        dx = err * (1.0 / D)
        xh = xhat_ref[...]
        dg_ref[...] += _fold8(dx * xh)
        db_ref[...] += _fold8(dx)
        dz_ref[...] = _ln_bwd(dx * g_ref[...], xh, rstd_ref[...])

    return pl.pallas_call(
        kern, name=name, grid=(n // tm,),
        in_specs=[pl.BlockSpec((tm, D), row), pl.BlockSpec((tm, D), row), pl.BlockSpec((tm, D), row),
                  pl.BlockSpec((tm, 1), row), pl.BlockSpec((1, D), fix)],
        out_specs=[pl.BlockSpec((8, D), fix), pl.BlockSpec((tm, D), row), pl.BlockSpec((8, D), fix),
                   pl.BlockSpec((8, D), fix)],
        out_shape=[_sds((8, D), F32), _sds((n, D), F32), _sds((8, D), F32), _sds((8, D), F32)],
        compiler_params=_cp(("arbitrary",)))(x, target, xhat, rstd, g)


def _softmax_rows(s):
    s = s - jnp.max(s, axis=-1, keepdims=True)
    e = jnp.exp(s)
    return e / jnp.sum(e, axis=-1, keepdims=True)


def _attn_fwd(q, kv, bd, name, tm=512):
    n = q.shape[0]
    s_len = n // bd
    m_len = kv.shape[0] // bd
    tm = min(tm, s_len)
    nt = s_len // tm
    scale = X_HD ** -0.5

    def kern(q_ref, k_ref, v_ref, o_ref):
        for h in range(X_HEADS):
            cs = slice(h * X_HD, (h + 1) * X_HD)
            s = lax.dot_general(q_ref[:, cs], k_ref[:, cs], (((1,), (1,)), ((), ())), preferred_element_type=F32)
            p = _softmax_rows(s * scale)
            o_ref[:, cs] = jnp.dot(p.astype(BF16), v_ref[:, cs], preferred_element_type=F32).astype(BF16)

    return pl.pallas_call(
        kern, name=name, grid=(bd, nt),
        in_specs=[pl.BlockSpec((tm, D), lambda b, i: (b * nt + i, 0)),
                  pl.BlockSpec((m_len, D), lambda b, i: (b, 0)), pl.BlockSpec((m_len, D), lambda b, i: (b, 1))],
        out_specs=pl.BlockSpec((tm, D), lambda b, i: (b * nt + i, 0)),
        out_shape=_sds((n, D), BF16),
        compiler_params=_cp(("parallel", "parallel")))(q, kv, kv)


def _attn_bwd(q, kv, do, bd, name, tm=512):
    n = q.shape[0]
    s_len = n // bd
    m_len = kv.shape[0] // bd
    tm = min(tm, s_len)
    nt = s_len // tm
    scale = X_HD ** -0.5

    def kern(q_ref, k_ref, v_ref, do_ref, dq_ref, dkv_ref):
        @pl.when(pl.program_id(1) == 0)
        def _():
            dkv_ref[...] = jnp.zeros_like(dkv_ref)

        for h in range(X_HEADS):
            cs = slice(h * X_HD, (h + 1) * X_HD)
            vs = slice(D + h * X_HD, D + (h + 1) * X_HD)
            qh, kh, vh, doh = q_ref[:, cs], k_ref[:, cs], v_ref[:, cs], do_ref[:, cs]
            s = lax.dot_general(qh, kh, (((1,), (1,)), ((), ())), preferred_element_type=F32)
            p = _softmax_rows(s * scale)
            pb = p.astype(BF16)
            dp = lax.dot_general(doh, vh, (((1,), (1,)), ((), ())), preferred_element_type=F32)
            dkv_ref[:, vs] += lax.dot_general(pb, doh, (((0,), (0,)), ((), ())), preferred_element_type=F32)
            ds = (p * (dp - jnp.sum(dp * p, axis=-1, keepdims=True)) * scale).astype(BF16)
            dq_ref[:, cs] = jnp.dot(ds, kh, preferred_element_type=F32).astype(BF16)
            dkv_ref[:, cs] += lax.dot_general(ds, qh, (((0,), (0,)), ((), ())), preferred_element_type=F32)

    return pl.pallas_call(
        kern, name=name, grid=(bd, nt),
        in_specs=[pl.BlockSpec((tm, D), lambda b, i: (b * nt + i, 0)),
                  pl.BlockSpec((m_len, D), lambda b, i: (b, 0)), pl.BlockSpec((m_len, D), lambda b, i: (b, 1)),
                  pl.BlockSpec((tm, D), lambda b, i: (b * nt + i, 0))],
        out_specs=[pl.BlockSpec((tm, D), lambda b, i: (b * nt + i, 0)),
                   pl.BlockSpec((m_len, 2 * D), lambda b, i: (b, 0))],
        out_shape=[_sds((n, D), BF16), _sds((bd * m_len, 2 * D), F32)],
        compiler_params=_cp(("parallel", "arbitrary")))(q, kv, kv, do)


C_AV, C_AG, C_BU, C_BV, C_CB, C_CC, C_CX = 0, 384, 768, 1024, 1280, 1664, 2048


def _causal_conv(win_ref, r0, halo, w_ref, n_taps):
    parts = []
    off = halo - (n_taps - 1)
    for cb in range(3):
        cs = slice(cb * 128, (cb + 1) * 128)
        win = win_ref[pl.ds(r0, CHUNK + halo), cs]
        acc = jnp.zeros((CHUNK, 128), F32)
        for k in range(n_taps):
            acc = acc + win[off + k:off + k + CHUNK, :] * w_ref[k:k + 1, cs]
        parts.append(acc)
    return jnp.concatenate(parts, axis=1)


def _anticausal_conv(win_ref, r0, halo, w_ref, n_taps):
    parts = []
    for cb in range(3):
        cs = slice(cb * 128, (cb + 1) * 128)
        win = win_ref[pl.ds(r0, CHUNK + halo), cs]
        acc = jnp.zeros((CHUNK, 128), F32)
        for k in range(n_taps):
            o = n_taps - 1 - k
            acc = acc + win[o:o + CHUNK, :] * w_ref[k:k + 1, cs]
        parts.append(acc)
    return jnp.concatenate(parts, axis=1)


def _head_of_lane():
    return lax.broadcasted_iota(jnp.int32, (1, D_B), 1) // HEAD


def _spatial_mix(wm_ref, vb, head):
    mixed = jnp.zeros((CHUNK, D_B), F32)
    for h in range(B_HEADS):
        mh = jnp.dot(wm_ref[h], vb, preferred_element_type=F32)
        mixed = jnp.where(head == h, mh, mixed)
    return mixed


def _mixer_fwd(proj, p, bd, name, rider=None):
    n = proj.shape[0]
    s_len = n // bd
    n_chunks = s_len // CHUNK

    def kern(proj_ref, caw_ref, cab_ref, lag_ref, lab_ref, lvg_ref, lvb_ref, wm_ref, bsx_ref, ccw_ref, cat_ref,
             gs_ref, ccs_ref):
        gs_ref[0:HALO_A, :] = jnp.zeros((HALO_A, D_A), F32)
        ccs_ref[0:HALO_C, :] = jnp.zeros((HALO_C, D_C), F32)
        head = _head_of_lane()

        def chunk(i, carry):
            r0 = pl.multiple_of(i * CHUNK, CHUNK)
            rows = pl.ds(r0, CHUNK)
            ld = lambda c0, w: proj_ref[rows, c0:c0 + w].astype(F32)
            gs_ref[pl.ds(r0 + HALO_A, CHUNK), :] = ld(C_AV, D_A) * jax.nn.sigmoid(ld(C_AG, D_A))
            ca = _causal_conv(gs_ref, r0, HALO_A, caw_ref, KA) + cab_ref[...]
            lna = _ln(ca)[0] * lag_ref[...] + lab_ref[...]
            cat_ref[rows, 0:D_A] = (lna * jax.nn.sigmoid(lna)).astype(BF16)
            u = _gelu(ld(C_BU, D_B))
            v = _ln(_gelu(ld(C_BV, D_B)))[0] * lvg_ref[...] + lvb_ref[...]
            mixed = _spatial_mix(wm_ref, v.astype(BF16), head) + bsx_ref[...]
            cat_ref[rows, D_A:D_A + D_B] = (u * mixed).astype(BF16)
            ccs_ref[pl.ds(r0 + HALO_C, CHUNK), :] = ld(C_CC, D_C) * ld(C_CX, D_C)
            conv = _causal_conv(ccs_ref, r0, HALO_C, ccw_ref, KC)
            cat_ref[rows, D_A + D_B:D] = (ld(C_CB, D_C) * conv).astype(BF16)
            return carry

        lax.fori_loop(0, n_chunks, chunk, 0)

    fix2 = lambda b: (0, 0)
    args = [proj, p["caw"], p["cab"], p["lag"], p["lab"], p["lvg"], p["lvb"], p["wm"], p["bsx"], p["ccw"]]
    in_specs = [pl.BlockSpec((s_len, IN_W), lambda b: (b, 0))]
    for a in args[1:]:
        in_specs.append(pl.BlockSpec(a.shape, (lambda b: (0, 0, 0)) if a.ndim == 3 else fix2))
    res, rode = _call(
        kern, name, (bd,), in_specs, pl.BlockSpec((s_len, D), lambda b: (b, 0)), _sds((n, D), BF16), args,
        ("parallel",), scratch=[pltpu.VMEM((s_len + HALO_A, D_A), F32), pltpu.VMEM((s_len + HALO_C, D_C), F32)],
        rider=rider)
    return res if rider is None else (res, rode)


def _mixer_bwd(proj, dcat, p, bd, name, rider=None):
    n = proj.shape[0]
    s_len = n // bd
    n_chunks = s_len // CHUNK

    def kern(proj_ref, dcat_ref, caw_ref, cab_ref, lag_ref, lab_ref, lvg_ref, lvb_ref, wm_ref, bsx_ref, ccw_ref,
             dproj_ref, dcaw_ref, dcab_ref, dlag_ref, dlab_ref, dlvg_ref, dlvb_ref, dws_ref, dbs_ref, dccw_ref,
             gs_ref, dcas_ref, ccs_ref, dcs_ref, a_caw, a_cab, a_lag, a_lab, a_lvg, a_lvb, a_ccw):
        gs_ref[0:HALO_A, :] = jnp.zeros((HALO_A, D_A), F32)
        ccs_ref[0:HALO_C, :] = jnp.zeros((HALO_C, D_C), F32)
        dcas_ref[s_len:s_len + HALO_A, :] = jnp.zeros((HALO_A, D_A), F32)
        dcs_ref[s_len:s_len + HALO_C, :] = jnp.zeros((HALO_C, D_C), F32)
        for acc in (a_caw, a_cab, a_lag, a_lab, a_lvg, a_lvb, a_ccw, dws_ref, dbs_ref):
            acc[...] = jnp.zeros_like(acc)
        head = _head_of_lane()
        lane128 = lax.broadcasted_iota(jnp.int32, (1, CHUNK), 1)

        def pass1(i, carry):
            r0 = pl.multiple_of(i * CHUNK, CHUNK)
            rows = pl.ds(r0, CHUNK)
            ld = lambda c0, w: proj_ref[rows, c0:c0 + w].astype(F32)
            dld = lambda c0, w: dcat_ref[rows, c0:c0 + w].astype(F32)
            gs_ref[pl.ds(r0 + HALO_A, CHUNK), :] = ld(C_AV, D_A) * jax.nn.sigmoid(ld(C_AG, D_A))
            ca = _causal_conv(gs_ref, r0, HALO_A, caw_ref, KA) + cab_ref[...]
            xh, rstd = _ln(ca)
            lna = xh * lag_ref[...] + lab_ref[...]
            sg = jax.nn.sigmoid(lna)
            dlna = dld(0, D_A) * (sg * (1.0 + lna * (1.0 - sg)))
            a_lag[...] += _fold8(dlna * xh)
            a_lab[...] += _fold8(dlna)
            dca = _ln_bwd(dlna * lag_ref[...], xh, rstd)
            dcas_ref[rows, :] = dca
            a_cab[...] += _fold8(dca)
            for cb in range(3):
                cs = slice(cb * 128, (cb + 1) * 128)
                win = gs_ref[pl.ds(r0, CHUNK + HALO_A), cs]
                dcab = dca[:, cs]
                for k in range(KA):
                    o = HALO_A - (KA - 1) + k
                    a_caw[k * 8:(k + 1) * 8, cs] += _fold8(dcab * win[o:o + CHUNK, :])
            pu, pv = ld(C_BU, D_B), ld(C_BV, D_B)
            u = _gelu(pu)
            vxh, vrstd = _ln(_gelu(pv))
            v = vxh * lvg_ref[...] + lvb_ref[...]
            vb = v.astype(BF16)
            mixed = _spatial_mix(wm_ref, vb, head) + bsx_ref[...]
            dbo = dld(D_A, D_B)
            dproj_ref[rows, C_BU:C_BU + D_B] = (dbo * mixed * _gelu_grad(pu)).astype(BF16)
            dmixed = dbo * u
            dv = jnp.zeros((CHUNK, D_B), F32)
            bsum = jnp.zeros((CHUNK, CHUNK), F32)
            for h in range(B_HEADS):
                dmh = jnp.where(head == h, dmixed, 0.0)
                dmb = dmh.astype(BF16)
                dvh = lax.dot_general(wm_ref[h], dmb, (((0,), (0,)), ((), ())), preferred_element_type=F32)
                dv = jnp.where(head == h, dvh, dv)
                dws_ref[h] += lax.dot_general(dmb, vb, (((1,), (1,)), ((), ())), preferred_element_type=F32)
                bsum = bsum + jnp.where(lane128 == h, jnp.sum(dmh, axis=-1, keepdims=True), 0.0)
            dbs_ref[...] += bsum
            a_lvg[...] += _fold8(dv * vxh)
            a_lvb[...] += _fold8(dv)
            dgv = _ln_bwd(dv * lvg_ref[...], vxh, vrstd)
            dproj_ref[rows, C_BV:C_BV + D_B] = (dgv * _gelu_grad(pv)).astype(BF16)
            ccs_ref[pl.ds(r0 + HALO_C, CHUNK), :] = ld(C_CC, D_C) * ld(C_CX, D_C)
            conv = _causal_conv(ccs_ref, r0, HALO_C, ccw_ref, KC)
            dco = dld(D_A + D_B, D_C)
            dproj_ref[rows, C_CB:C_CB + D_C] = (dco * conv).astype(BF16)
            dconv = dco * ld(C_CB, D_C)
            dcs_ref[rows, :] = dconv
            for cb in range(3):
                cs = slice(cb * 128, (cb + 1) * 128)
                win = ccs_ref[pl.ds(r0, CHUNK + HALO_C), cs]
                for k in range(KC):
                    o = HALO_C - (KC - 1) + k
                    a_ccw[k * 8:(k + 1) * 8, cs] += _fold8(dconv[:, cs] * win[o:o + CHUNK, :])
            return carry

        lax.fori_loop(0, n_chunks, pass1, 0)

        def pass2(i, carry):
            r0 = pl.multiple_of(i * CHUNK, CHUNK)
            rows = pl.ds(r0, CHUNK)
            ld = lambda c0, w: proj_ref[rows, c0:c0 + w].astype(F32)
            dg = _anticausal_conv(dcas_ref, r0, HALO_A, caw_ref, KA)
            pa = ld(C_AV, D_A)
            sg = jax.nn.sigmoid(ld(C_AG, D_A))
            dproj_ref[rows, C_AV:C_AV + D_A] = (dg * sg).astype(BF16)
            dproj_ref[rows, C_AG:C_AG + D_A] = (dg * pa * sg * (1.0 - sg)).astype(BF16)
            dcc = _anticausal_conv(dcs_ref, r0, HALO_C, ccw_ref, KC)
            dproj_ref[rows, C_CC:C_CC + D_C] = (dcc * ld(C_CX, D_C)).astype(BF16)
            dproj_ref[rows, C_CX:C_CX + D_C] = (dcc * ld(C_CC, D_C)).astype(BF16)
            return carry

        lax.fori_loop(0, n_chunks, pass2, 0)

        for k in range(KA):
            dcaw_ref[k:k + 1, :] = jnp.sum(a_caw[k * 8:(k + 1) * 8, :], axis=0, keepdims=True)
        dcaw_ref[KA:KA + 1, :] = jnp.zeros((1, D_A), F32)
        for k in range(8):
            if k < KC:
                dccw_ref[k:k + 1, :] = jnp.sum(a_ccw[k * 8:(k + 1) * 8, :], axis=0, keepdims=True)
            else:
                dccw_ref[k:k + 1, :] = jnp.zeros((1, D_C), F32)
        dcab_ref[...] = a_cab[...]
        dlag_ref[...] = a_lag[...]
        dlab_ref[...] = a_lab[...]
        dlvg_ref[...] = a_lvg[...]
        dlvb_ref[...] = a_lvb[...]

    fix2 = lambda b: (0, 0)
    args = [proj, dcat, p["caw"], p["cab"], p["lag"], p["lab"], p["lvg"], p["lvb"], p["wm"], p["bsx"], p["ccw"]]
    once = pl.Buffered(1)
    in_specs = [pl.BlockSpec((s_len, IN_W), lambda b: (b, 0), pipeline_mode=once),
                pl.BlockSpec((s_len, D), lambda b: (b, 0), pipeline_mode=once)]
    for a in args[2:]:
        in_specs.append(pl.BlockSpec(a.shape, (lambda b: (0, 0, 0)) if a.ndim == 3 else fix2))

    def per_seq(*shape):
        nd = len(shape)
        return (pl.BlockSpec((None,) + shape, lambda b: (b,) + (0,) * nd), _sds((bd,) + shape, F32))

    outs = [(pl.BlockSpec((s_len, IN_W), lambda b: (b, 0), pipeline_mode=once), _sds((n, IN_W), BF16)),
            per_seq(32, D_A), per_seq(8, D_A), per_seq(8, D_A), per_seq(8, D_A), per_seq(8, D_B), per_seq(8, D_B),
            per_seq(B_HEADS, CHUNK, CHUNK), per_seq(CHUNK, CHUNK), per_seq(8, D_C)]
    res, rode = _call(
        kern, name, (bd,), in_specs, [o[0] for o in outs], [o[1] for o in outs], args, ("parallel",),
        scratch=[pltpu.VMEM((s_len + HALO_A, D_A), F32), pltpu.VMEM((s_len + HALO_A, D_A), F32),
                 pltpu.VMEM((s_len + HALO_C, D_C), F32), pltpu.VMEM((s_len + HALO_C, D_C), F32),
                 pltpu.VMEM((KA * 8, D_A), F32), pltpu.VMEM((8, D_A), F32), pltpu.VMEM((8, D_A), F32),
                 pltpu.VMEM((8, D_A), F32), pltpu.VMEM((8, D_B), F32), pltpu.VMEM((8, D_B), F32),
                 pltpu.VMEM((KC * 8, D_C), F32)],
        rider=rider)
    return res if rider is None else (res, rode)


def _adamw(w, m, v, g_parts, name):
    shape = w.shape
    c = shape[-1]
    r = math.prod(shape[:-1])
    as2d = lambda a: a.reshape(r, c)
    tr = 512 if r % 512 == 0 else r
    slots = [(a.reshape(a.shape[0], r, c), p) for a in g_parts for p in range(a.shape[0])]
    n_g = len(slots)

    def kern(*refs):
        w_ref, m_ref, v_ref = refs[:3]
        g_refs = refs[3:3 + n_g]
        go_ref, d_ref, mo_ref, vo_ref = refs[3 + n_g:]
        g = g_refs[0][...].astype(F32)
        for gr in g_refs[1:]:
            g = g + gr[...].astype(F32)
        mn = B1 * m_ref[...] + (1.0 - B1) * g
        vn = B2 * v_ref[...] + (1.0 - B2) * jnp.square(g)
        m_hat = mn / (1.0 - B1 ** STEP)
        v_hat = vn / (1.0 - B2 ** STEP)
        go_ref[...] = g
        mo_ref[...] = mn
        vo_ref[...] = vn
        d_ref[...] = -LR * (m_hat / (jnp.sqrt(v_hat) + ADAM_EPS) + WD * w_ref[...])

    spec = pl.BlockSpec((tr, c), lambda i: (i, 0))
    g_specs = [pl.BlockSpec((None, tr, c), functools.partial(lambda p, i: (p, i, 0), p)) for _, p in slots]
    outs = pl.pallas_call(
        kern, name=name, grid=(r // tr,), in_specs=[spec] * 3 + g_specs, out_specs=[spec] * 4,
        out_shape=[_sds((r, c), F32)] * 4, compiler_params=_cp(("parallel",)))(
            as2d(w), as2d(m), as2d(v), *[a for a, _ in slots])
    return [o.reshape(shape) for o in outs]


def _place():
    x, y, c = lax.axis_index("x"), lax.axis_index("y"), lax.axis_index("c")
    other_chips = [(1 - x, y), (x, 1 - y), (1 - x, 1 - y)]
    return x, y, c, other_chips


def _gather_rider(arrays):
    n_arr = len(arrays)

    def parts(ins, outs, sems):
        send_sems, recv_sems, local_sems = sems
        x, y, c, chips = _place()
        me, sibling = (x, y, c), (x, y, 1 - c)
        slot = lambda px, py, pc: 4 * px + 2 * py + pc

        def copy(a, k, block, to, from_input=False):
            dst = outs[a].at[slot(*block)]
            return pltpu.make_async_remote_copy(
                src_ref=ins[a] if from_input else dst, dst_ref=dst, send_sem=send_sems.at[k, a],
                recv_sem=recv_sems.at[k, a], device_id=to, device_id_type=MESH)

        mine = [pltpu.make_async_copy(ins[a], outs[a].at[slot(*me)], local_sems.at[a]) for a in range(n_arr)]
        first = []
        for a in range(n_arr):
            first.append(copy(a, 0, me, sibling, True))
            first += [copy(a, 1 + j, me, (*chip, c), True) for j, chip in enumerate(chips)]
        return copy, mine, first, me, sibling, chips, c

    def start(ins, outs, sems):
        _, mine, first, *_ = parts(ins, outs, sems)
        for cp in mine + first:
            cp.start()

    def finish(ins, outs, sems):
        copy, mine, first, me, sibling, chips, c = parts(ins, outs, sems)
        passed = []
        for j, chip in enumerate(chips):
            for a in range(n_arr):
                copy(a, 1 + j, (*chip, c), me).wait_recv()
                passed.append(copy(a, 4 + j, (*chip, c), sibling))
                passed[-1].start()
        for a in range(n_arr):
            copy(a, 0, sibling, me).wait_recv()
            for j, chip in enumerate(chips):
                copy(a, 4 + j, (*chip, 1 - c), me).wait_recv()
        for cp in first + passed:
            cp.wait_send()
        for cp in mine:
            cp.wait()

    return _Rider(list(arrays), [_sds((N_DEV,) + a.shape, a.dtype) for a in arrays],
                  [pltpu.SemaphoreType.DMA((7, n_arr)), pltpu.SemaphoreType.DMA((7, n_arr)),
                   pltpu.SemaphoreType.DMA((n_arr,))], start, finish)


def _exchange_rider(arrays):
    n_arr = len(arrays)

    def copies(ins, outs, sems):
        send_sems, recv_sems = sems
        x, y, c, _ = _place()
        return [pltpu.make_async_remote_copy(
            src_ref=ins[a].at[1 - c], dst_ref=outs[a], send_sem=send_sems.at[a], recv_sem=recv_sems.at[a],
            device_id=(x, y, 1 - c), device_id_type=MESH) for a in range(n_arr)]

    def start(ins, outs, sems):
        for cp in copies(ins, outs, sems):
            cp.start()

    def finish(ins, outs, sems):
        for cp in copies(ins, outs, sems):
            cp.wait()

    return _Rider(list(arrays), [_sds(a.shape[1:], a.dtype) for a in arrays],
                  [pltpu.SemaphoreType.DMA((n_arr,)), pltpu.SemaphoreType.DMA((n_arr,))], start, finish)


def _pair_add(mine, theirs, core, name):
    _, r, c = mine.shape
    tr = 512 if r % 512 == 0 else r

    def kern(core_ref, a_ref, b_ref, o_ref, ob_ref):
        s = a_ref[...] + b_ref[...]
        o_ref[...] = s
        ob_ref[...] = s.astype(BF16)

    row = lambda i, core_ref: (i, 0)
    return pl.pallas_call(
        kern, name=name,
        grid_spec=pltpu.PrefetchScalarGridSpec(
            num_scalar_prefetch=1, grid=(r // tr,),
            in_specs=[pl.BlockSpec((None, tr, c), lambda i, core_ref: (core_ref[0], i, 0)), pl.BlockSpec((tr, c), row)],
            out_specs=[pl.BlockSpec((tr, c), row), pl.BlockSpec((tr, c), row)]),
        out_shape=[_sds((r, c), F32), _sds((r, c), BF16)], compiler_params=_cp(("parallel",)))(core, mine, theirs)


def _scatter_rider(sums, sums_bf16):
    n_arr = len(sums)

    def copies(ins, outs, sems):
        send_sems, recv_sems, local_sems = sems
        x, y, c, chips = _place()
        own = [pltpu.make_async_copy(ins[a].at[2 * x + y], outs[a], local_sems.at[a]) for a in range(n_arr)]
        remote = []
        for a in range(n_arr):
            for j, (px, py) in enumerate(chips):
                remote.append(pltpu.make_async_remote_copy(
                    src_ref=ins[n_arr + a].at[2 * px + py], dst_ref=outs[n_arr + a].at[j], send_sem=send_sems.at[j, a],
                    recv_sem=recv_sems.at[j, a], device_id=(px, py, c), device_id_type=MESH))
        return own + remote

    def start(ins, outs, sems):
        for cp in copies(ins, outs, sems):
            cp.start()

    def finish(ins, outs, sems):
        for cp in copies(ins, outs, sems):
            cp.wait()

    return _Rider(list(sums) + list(sums_bf16),
                  [_sds(a.shape[1:], a.dtype) for a in sums] + [_sds((3,) + a.shape[1:], a.dtype) for a in sums_bf16],
                  [pltpu.SemaphoreType.DMA((3, n_arr)), pltpu.SemaphoreType.DMA((3, n_arr)),
                   pltpu.SemaphoreType.DMA((n_arr,))], start, finish)


def _mixer_params(conv_a_w, conv_a_b, ln_a_g, ln_a_b, ln_v_g, ln_v_b, w_s, b_s, conv_c_w):
    causal = jnp.tril(jnp.ones((CHUNK, CHUNK), dtype=bool))
    row = lambda a: a.reshape(1, -1)
    return dict(
        caw=jnp.pad(conv_a_w, ((0, 32 - KA), (0, 0))), cab=row(conv_a_b), lag=row(ln_a_g), lab=row(ln_a_b),
        lvg=row(ln_v_g), lvb=row(ln_v_b), wm=jnp.where(causal[None], w_s, 0.0).astype(BF16),
        bsx=jnp.repeat(b_s.T, HEAD, axis=1), ccw=jnp.pad(conv_c_w, ((0, 8 - KC), (0, 0))))


class _Schedule:
    def __init__(self, big=None):
        self.big = big

    def weights(self, l):
        return {k: v[l] for k, v in self.big.items()}

    def forward_riders(self, l):
        return None, None

    def forward_rode(self, l, on_mixer, on_ff1):
        pass

    def ff2_bwd_rider(self, l):
        return None

    def mixer_bwd_rider(self, l, rode_on_ff2_bwd):
        return None

    def mixer_bwd_rode(self, l, rode):
        pass

    def layer_grads(self, l, grads):
        pass


def _local_step(x, mem, target, sched, small, bd):
    row = lambda a: a.reshape(1, -1)
    saved = []
    for l in range(DEPTH):
        big = sched.weights(l)
        rider_a, rider_b = sched.forward_riders(l)
        mp = _mixer_params(big["conv_a_w"], *[small[k][l] for k in ("conv_a_b", "ln_a_g", "ln_a_b", "ln_v_g", "ln_v_b",
                                                                    "w_s", "b_s")], big["conv_c_w"])
        proj = _mm_nn(x, big["w_in"], BF16, f"in_proj_{l}")
        cat = _mixer_fwd(proj, mp, bd, f"mixer_fwd_{l}", rider=rider_a)
        rode_a = None
        if rider_a is not None:
            cat, rode_a = cat
        x1, xh1, rs1 = _mm_res_ln(cat, big["w_out"], x, row(small["ln1_g"][l]), row(small["ln1_b"][l]),
                                  f"out_proj_ln1_{l}")
        q = _mm_nn(x1, big["w_q"], BF16, f"q_proj_{l}")
        kv = _mm_nn(mem, big["w_kv"], BF16, f"kv_proj_{l}")
        o = _attn_fwd(q, kv, bd, f"attn_fwd_{l}")
        x2, xh2, rs2 = _mm_res_ln(o, big["w_o"], x1, row(small["ln2_g"][l]), row(small["ln2_b"][l]), f"o_proj_ln2_{l}")
        h = _mm_nn(x2, big["w_ff1"], BF16, f"ff1_{l}", tn=1024, rider=rider_b)
        rode_b = None
        if rider_b is not None:
            h, rode_b = h
        sched.forward_rode(l, rode_a, rode_b)
        x3, xh3, rs3 = _mm_res_ln(h, big["w_ff2"], x2, row(small["ln3_g"][l]), row(small["ln3_b"][l]),
                                  f"ff2_ln3_{l}", relu2=True)
        saved.append(dict(mp=mp, x=x, proj=proj, cat=cat, x1=x1, xh1=xh1, rs1=rs1, q=q, kv=kv, o=o, x2=x2, xh2=xh2,
                          rs2=rs2, h=h, xh3=xh3, rs3=rs3))
        x = x3

    grads = {k: [None] * DEPTH for k in WEIGHTS}
    s = saved[-1]
    sq, dz3, dg, db = _loss_lnbwd(x, target, s["xh3"], s["rs3"], row(small["ln3_g"][DEPTH - 1]), "loss_ln3_bwd")
    grad_x = None
    causal = jnp.tril(jnp.ones((CHUNK, CHUNK), dtype=bool))
    for l in reversed(range(DEPTH)):
        s = saved[l]
        big = sched.weights(l)
        grads["ln3_g"][l], grads["ln3_b"][l] = jnp.sum(dg, axis=0), jnp.sum(db, axis=0)
        rider = sched.ff2_bwd_rider(l)
        dh = _mm_nt(dz3, big["w_ff2"], BF16, f"ff2_bwd_{l}", relu2_grad_of=s["h"], rider=rider)
        rode = None
        if rider is not None:
            dh, rode = dh
        mixer_rider = sched.mixer_bwd_rider(l, rode)
        grads["w_ff2"][l] = _mm_tn(s["h"], dz3, f"ff2_wgrad_{l}", relu2=True, t1=2048)
        grads["w_ff1"][l] = _mm_tn(s["x2"], dh, f"ff1_wgrad_{l}")
        dz2, dg, db = _bwd_in(dz3, dh, big["w_ff1"], f"ff1_bwd_ln2_{l}", ln=(s["xh2"], s["rs2"], row(small["ln2_g"][l])))
        grads["ln2_g"][l], grads["ln2_b"][l] = jnp.sum(dg, axis=0), jnp.sum(db, axis=0)
        do = _mm_nt(dz2, big["w_o"], BF16, f"o_proj_bwd_{l}")
        grads["w_o"][l] = _mm_tn(s["o"], dz2, f"o_proj_wgrad_{l}")
        dq, dkv = _attn_bwd(s["q"], s["kv"], do, bd, f"attn_bwd_{l}")
        grads["w_q"][l] = _mm_tn(s["x1"], dq, f"q_wgrad_{l}")
        grads["w_kv"][l] = _mm_tn(mem, dkv, f"kv_wgrad_{l}")
        dz1, dg, db = _bwd_in(dz2, dq, big["w_q"], f"q_bwd_ln1_{l}", ln=(s["xh1"], s["rs1"], row(small["ln1_g"][l])))
        grads["ln1_g"][l], grads["ln1_b"][l] = jnp.sum(dg, axis=0), jnp.sum(db, axis=0)
        dcat = _mm_nt(dz1, big["w_out"], BF16, f"out_proj_bwd_{l}")
        grads["w_out"][l] = _mm_tn(s["cat"], dz1, f"out_proj_wgrad_{l}")
        res = _mixer_bwd(s["proj"], dcat, s["mp"], bd, f"mixer_bwd_{l}", rider=mixer_rider)
        if mixer_rider is not None:
            res, rode = res
            sched.mixer_bwd_rode(l, rode)
        (dproj, dcaw, dcab, dlag, dlab, dlvg, dlvb, dws, dbs, dccw) = res
        grads["conv_a_w"][l] = jnp.sum(dcaw, axis=0)[:KA]
        grads["conv_a_b"][l] = jnp.sum(dcab, axis=(0, 1))
        grads["ln_a_g"][l] = jnp.sum(dlag, axis=(0, 1))
        grads["ln_a_b"][l] = jnp.sum(dlab, axis=(0, 1))
        grads["ln_v_g"][l] = jnp.sum(dlvg, axis=(0, 1))
        grads["ln_v_b"][l] = jnp.sum(dlvb, axis=(0, 1))
        grads["w_s"][l] = jnp.where(causal[None], jnp.sum(dws, axis=0), 0.0)
        grads["b_s"][l] = jnp.sum(dbs, axis=0)[:, :B_HEADS].T
        grads["conv_c_w"][l] = jnp.sum(dccw, axis=0)[:KC]
        grads["w_in"][l] = _mm_tn(s["x"], dproj, f"in_proj_wgrad_{l}")
        if l > 0:
            p = saved[l - 1]
            dz3, dg, db = _bwd_in(dz1, dproj, big["w_in"], f"in_proj_bwd_ln3_{l}",
                                  ln=(p["xh3"], p["rs3"], row(small["ln3_g"][l - 1])))
        else:
            grad_x = _bwd_in(dz1, dproj, big["w_in"], "in_proj_bwd_0")
        sched.layer_grads(l, {k: v[l] for k, v in grads.items()})
    return sq, grad_x, grads


WEIGHTS = ("w_in", "conv_a_w", "conv_a_b", "ln_a_g", "ln_a_b", "ln_v_g", "ln_v_b", "w_s", "b_s", "conv_c_w", "w_out",
           "ln1_g", "ln1_b", "w_q", "w_kv", "w_o", "ln2_g", "ln2_b", "w_ff1", "w_ff2", "ln3_g", "ln3_b")
COL_SHARDED = ("w_in", "w_kv", "w_ff1")
ROW_SHARDED = ("w_out", "w_q", "w_o", "w_ff2")
BIG = COL_SHARDED + ROW_SHARDED
REPLICATED = tuple(k for k in WEIGHTS if k not in BIG and k not in ("conv_a_w", "conv_c_w"))
PACK_LANES = 128


CONV_ROWS = 32 + 8
GATHER_ON_MIXER = ("w_ff1", "w_kv")
GATHER_ON_FF1 = ("w_ff2", "w_in", "w_out", "w_q", "w_o", "conv")


def _gathered_to_full(g, col_sharded):
    _, a, b = g.shape
    if col_sharded:
        return g.transpose(1, 0, 2).reshape(a, N_DEV * b)
    return g.reshape(N_DEV * a, b)


def _full_to_owner_major(g, col_sharded):
    a, b = g.shape
    if col_sharded:
        return g.reshape(a, 4, 2, b // N_DEV).transpose(2, 1, 0, 3)
    return g.reshape(4, 2, a // N_DEV, b).transpose(1, 0, 2, 3)


def _conv_pack(conv_a, conv_c):
    pad = lambda a, rows: jnp.pad(a, [(0, 0)] * (a.ndim - 2) + [(0, rows - a.shape[-2]), (0, 0)])
    return jnp.concatenate([pad(conv_a, 32), pad(conv_c, 8)], axis=-2)


def _conv_unpack(packed):
    return packed[..., :KA, :], packed[..., 32:32 + KC, :]


class _Overlapped(_Schedule):
    def __init__(self, shards_bf16, conv_shards, core):
        self.shards, self.conv_shards, self.core = shards_bf16, conv_shards, core
        self.full = {}
        self.pending = {}
        self.owned = {}

    def _gather_arrays(self, l, names):
        return [self.conv_shards[l] if k == "conv" else self.shards[k][l] for k in names]

    def _store(self, l, names, gathered):
        full = self.full.setdefault(l, {})
        for k, g in zip(names, gathered):
            if k == "conv":
                full["conv_a_w"], full["conv_c_w"] = _conv_unpack(_gathered_to_full(g, True))
            else:
                full[k] = _gathered_to_full(g, k in COL_SHARDED)

    def weights(self, l):
        if l == 0 and 0 not in self.full:
            names = GATHER_ON_MIXER + GATHER_ON_FF1
            self._store(0, names, _ride_alone(_gather_rider(self._gather_arrays(0, names)), "weights_all_gather_0"))
        return self.full[l]

    def forward_riders(self, l):
        if l + 1 == DEPTH:
            return None, None
        return (_gather_rider(self._gather_arrays(l + 1, GATHER_ON_MIXER)),
                _gather_rider(self._gather_arrays(l + 1, GATHER_ON_FF1)))

    def forward_rode(self, l, on_mixer, on_ff1):
        if l + 1 < DEPTH:
            self._store(l + 1, GATHER_ON_MIXER, on_mixer)
            self._store(l + 1, GATHER_ON_FF1, on_ff1)

    def layer_grads(self, l, grads):
        self.pending[l] = [_full_to_owner_major(grads[k], k in COL_SHARDED) for k in BIG]
        self.pending[l].append(_full_to_owner_major(_conv_pack(grads["conv_a_w"], grads["conv_c_w"]), True))
        if l == 0:
            from_sibling = _ride_alone(_exchange_rider(self.pending[0]), "grad_pair_exchange_0")
            self.owned[0] = _ride_alone(self._sums_rider(0, from_sibling), "grad_chip_scatter_0")

    def _sums_rider(self, l, from_sibling):
        sums, sums_bf16 = [], []
        for i, (g, r) in enumerate(zip(self.pending[l], from_sibling)):
            rows, cols = math.prod(r.shape[:-1]), r.shape[-1]
            s, sb = _pair_add(g.reshape(2, rows, cols), r.reshape(rows, cols), self.core, f"grad_pair_add_{l}_{i}")
            sums.append(s.reshape(r.shape))
            sums_bf16.append(sb.reshape(r.shape))
        return _scatter_rider(sums, sums_bf16)

    def ff2_bwd_rider(self, l):
        return _exchange_rider(self.pending[l + 1]) if l + 1 < DEPTH else None

    def mixer_bwd_rider(self, l, rode_on_ff2_bwd):
        return self._sums_rider(l + 1, rode_on_ff2_bwd) if l + 1 < DEPTH else None

    def mixer_bwd_rode(self, l, rode):
        self.owned[l + 1] = rode


def _pack_rows(parts):
    flat = jnp.concatenate([p.reshape(-1, PACK_LANES) for p in parts], axis=0)
    return jnp.pad(flat, ((0, -flat.shape[0] % 8), (0, 0)))


def _unpack_rows(packed, like):
    out, r = [], 0
    for p in like:
        n = p.size // PACK_LANES
        out.append(packed[r:r + n].reshape(p.shape))
        r += n
    return out


def kernel(x, mem, w_in, conv_a_w, conv_a_b, ln_a_g, ln_a_b, ln_v_g, ln_v_b, w_s, b_s, conv_c_w, w_out, ln1_g, ln1_b, w_q, w_kv, w_o, ln2_g, ln2_b, w_ff1, w_ff2, ln3_g, ln3_b, loss_target, m_w_in, m_conv_a_w, m_conv_a_b, m_ln_a_g, m_ln_a_b, m_ln_v_g, m_ln_v_b, m_w_s, m_b_s, m_conv_c_w, m_w_out, m_ln1_g, m_ln1_b, m_w_q, m_w_kv, m_w_o, m_ln2_g, m_ln2_b, m_w_ff1, m_w_ff2, m_ln3_g, m_ln3_b, v_w_in, v_conv_a_w, v_conv_a_b, v_ln_a_g, v_ln_a_b, v_ln_v_g, v_ln_v_b, v_w_s, v_b_s, v_conv_c_w, v_w_out, v_ln1_g, v_ln1_b, v_w_q, v_w_kv, v_w_o, v_ln2_g, v_ln2_b, v_w_ff1, v_w_ff2, v_ln3_g, v_ln3_b):
    given = dict(locals())
    w = {k: given[k] for k in WEIGHTS}
    mom = {k: given["m_" + k] for k in WEIGHTS}
    var = {k: given["v_" + k] for k in WEIGHTS}
    bd, s_len, _ = x.shape
    core = lax.axis_index("c").astype(jnp.int32).reshape(1)

    conv_pack = lambda d: _conv_pack(d["conv_a_w"], d["conv_c_w"])
    sched = _Overlapped({k: w[k].astype(BF16) for k in BIG}, conv_pack(w), core)
    sq, grad_x, grads = _local_step(x.reshape(bd * s_len, D), mem.reshape(-1, D), loss_target.reshape(bd * s_len, D),
                                    sched, {k: w[k] for k in REPLICATED}, bd)
    loss = lax.psum(0.5 * jnp.sum(sq) / D, ("x", "y", "c"))

    out = {}
    n_arr = len(BIG) + 1
    for i, k in enumerate(BIG + ("conv",)):
        own = jnp.stack([sched.owned[l][i] for l in range(DEPTH)])[None]
        remote = jnp.stack([sched.owned[l][n_arr + i] for l in range(DEPTH)], axis=1)
        if k == "conv":
            conv_out = _adamw(conv_pack(w), conv_pack(mom), conv_pack(var), [own, remote], "adamw_conv")
            unpacked = [_conv_unpack(o) for o in conv_out]
            out["conv_a_w"], out["conv_c_w"] = [u[0] for u in unpacked], [u[1] for u in unpacked]
        else:
            out[k] = _adamw(w[k], mom[k], var[k], [own, remote], f"adamw_{k}")

    rep_grads = _ride_alone(_gather_rider([_pack_rows([jnp.stack(grads[k]) for k in REPLICATED])]),
                            "replicated_grads_all_gather")[0]
    rep_out = _adamw(_pack_rows([w[k] for k in REPLICATED]), _pack_rows([mom[k] for k in REPLICATED]),
                     _pack_rows([var[k] for k in REPLICATED]), [rep_grads], "adamw_replicated")
    for i, o in enumerate(rep_out):
        for k, piece in zip(REPLICATED, _unpack_rows(o, [w[k] for k in REPLICATED])):
            out.setdefault(k, [None] * 4)[i] = piece

    res = [loss, grad_x.reshape(bd, s_len, D)]
    for i in range(4):
        res += [out[k][i] for k in WEIGHTS]
    return tuple(res)
```

```python
import functools
import math

import jax
import jax.numpy as jnp
from jax import lax
from jax.experimental import pallas as pl
from jax.experimental.pallas import tpu as pltpu

F32 = jnp.float32
BF16 = jnp.bfloat16

DEPTH = 4
D = 1024
D_A, D_B, D_C = 384, 256, 384
HEAD = 64
B_HEADS = 4
CHUNK = 128
KA, KC = 31, 3
HALO_A, HALO_C = 32, 8
IN_W = 2 * D_A + 2 * D_B + 3 * D_C
X_HEADS = 4
X_HD = D // X_HEADS
D_FF = 4 * D
EPS = 1e-5
ALPHA = (2.0 * DEPTH) ** 0.25
LR, B1, B2, ADAM_EPS, WD, STEP = 0.001, 0.9, 0.999, 1e-08, 0.01, 10
INV_SQRT2 = 0.7071067811865476
INV_SQRT_2PI = 0.3989422804014327
N_DEV = 8
VMEM_LIMIT = 56 * 1024 * 1024
MESH = pl.DeviceIdType.MESH
ANY = pl.BlockSpec(memory_space=pl.ANY)


def _cp(sem=None):
    return pltpu.CompilerParams(dimension_semantics=sem, vmem_limit_bytes=VMEM_LIMIT)


def _sds(shape, dtype):
    return jax.ShapeDtypeStruct(tuple(shape), dtype)


class _Rider:
    def __init__(self, arrays, out_shape, sems, start, finish):
        self.arrays, self.out_shape, self.sems, self.start, self.finish = arrays, out_shape, sems, start, finish


def _call(kern, name, grid, in_specs, out_specs, out_shape, args, sem, scratch=(), rider=None):
    single = not isinstance(out_shape, (list, tuple))
    out_specs_l = [out_specs] if single else list(out_specs)
    out_shape_l = [out_shape] if single else list(out_shape)
    if rider is None:
        res = pl.pallas_call(kern, name=name, grid=grid, in_specs=in_specs, out_specs=out_specs_l,
                             out_shape=out_shape_l, scratch_shapes=list(scratch), compiler_params=_cp(sem))(*args)
        return (res[0] if single else list(res)), None
    n_in, n_out, n_scr = len(args), len(out_shape_l), len(scratch)
    n_rin, n_rout = len(rider.arrays), len(rider.out_shape)

    def body(*refs):
        ins, refs = refs[:n_in], refs[n_in:]
        r_ins, refs = refs[:n_rin], refs[n_rin:]
        outs, refs = refs[:n_out], refs[n_out:]
        r_outs, refs = refs[:n_rout], refs[n_rout:]
        scr, r_sems = refs[:n_scr], refs[n_scr:]
        ids = [pl.program_id(d) for d in range(len(grid))]
        first = functools.reduce(jnp.logical_and, [i == 0 for i in ids])
        last = functools.reduce(jnp.logical_and, [i == g - 1 for i, g in zip(ids, grid)])

        @pl.when(first)
        def _():
            rider.start(r_ins, r_outs, r_sems)

        kern(*ins, *outs, *scr)

        @pl.when(last)
        def _():
            rider.finish(r_ins, r_outs, r_sems)

    res = pl.pallas_call(
        body, name=name, grid=grid, in_specs=list(in_specs) + [ANY] * n_rin,
        out_specs=out_specs_l + [ANY] * n_rout, out_shape=out_shape_l + list(rider.out_shape),
        scratch_shapes=list(scratch) + list(rider.sems),
        compiler_params=_cp(("arbitrary",) * len(grid)))(*args, *rider.arrays)
    mine, theirs = list(res[:n_out]), list(res[n_out:])
    return (mine[0] if single else mine), theirs


def _ride_alone(rider, name):
    def body(*refs):
        n_rin, n_rout = len(rider.arrays), len(rider.out_shape)
        r_ins, r_outs, r_sems = refs[:n_rin], refs[n_rin:n_rin + n_rout], refs[n_rin + n_rout:]
        rider.start(r_ins, r_outs, r_sems)
        rider.finish(r_ins, r_outs, r_sems)

    return list(pl.pallas_call(
        body, name=name, in_specs=[ANY] * len(rider.arrays), out_specs=[ANY] * len(rider.out_shape),
        out_shape=list(rider.out_shape), scratch_shapes=list(rider.sems))(*rider.arrays))


def _ln(z):
    mu = jnp.mean(z, axis=-1, keepdims=True)
    zc = z - mu
    var = jnp.mean(zc * zc, axis=-1, keepdims=True)
    rstd = lax.rsqrt(var + EPS)
    return zc * rstd, rstd


def _ln_bwd(dxhat, xhat, rstd):
    m1 = jnp.mean(dxhat, axis=-1, keepdims=True)
    m2 = jnp.mean(dxhat * xhat, axis=-1, keepdims=True)
    return rstd * (dxhat - m1 - xhat * m2)


def _gelu(x):
    return 0.5 * x * (1.0 + lax.erf(x * INV_SQRT2))


def _gelu_grad(x):
    return 0.5 * (1.0 + lax.erf(x * INV_SQRT2)) + x * jnp.exp(-0.5 * x * x) * INV_SQRT_2PI


def _fold8(x):
    r, c = x.shape
    return jnp.sum(x.reshape(r // 8, 8, c), axis=0)


def _relu2(h):
    return jnp.square(jnp.maximum(h, 0.0))


def _weight_spec(block, index_map, resident):
    return pl.BlockSpec(block, index_map, pipeline_mode=pl.Buffered(1) if resident else None)


def _mm_nn(a, w, out_dtype, name, tm=512, tn=None, rider=None):
    n, k = a.shape
    m = w.shape[1]
    tm = min(tm, n)
    tn = m if tn is None else min(tn, m)

    def kern(a_ref, w_ref, o_ref):
        o_ref[...] = jnp.dot(a_ref[...].astype(BF16), w_ref[...], preferred_element_type=F32).astype(out_dtype)

    res, rode = _call(
        kern, name, (n // tm, m // tn),
        [pl.BlockSpec((tm, k), lambda i, j: (i, 0)), _weight_spec((k, tn), lambda i, j: (0, j), tn == m)],
        pl.BlockSpec((tm, tn), lambda i, j: (i, j)), _sds((n, m), out_dtype), (a, w), ("parallel", "parallel"),
        rider=rider)
    return res if rider is None else (res, rode)


def _mm_res_ln(a, w, res, g, b, name, relu2=False, tm=512, rider=None):
    n, k = a.shape
    tm = min(tm, n)

    def kern(a_ref, w_ref, res_ref, g_ref, b_ref, x_ref, xhat_ref, rstd_ref):
        av = a_ref[...]
        if relu2:
            av = _relu2(av.astype(F32))
        z = ALPHA * res_ref[...] + jnp.dot(av.astype(BF16), w_ref[...], preferred_element_type=F32)
        xhat, rstd = _ln(z)
        xhat_ref[...] = xhat
        rstd_ref[...] = rstd
        x_ref[...] = xhat * g_ref[...] + b_ref[...]

    row = lambda i: (i, 0)
    fix = lambda i: (0, 0)
    out, rode = _call(
        kern, name, (n // tm,),
        [pl.BlockSpec((tm, k), row), _weight_spec((k, D), fix, True), pl.BlockSpec((tm, D), row),
         pl.BlockSpec((1, D), fix), pl.BlockSpec((1, D), fix)],
        [pl.BlockSpec((tm, D), row), pl.BlockSpec((tm, D), row), pl.BlockSpec((tm, 1), row)],
        [_sds((n, D), F32), _sds((n, D), F32), _sds((n, 1), F32)], (a, w, res, g, b), ("parallel",), rider=rider)
    return out if rider is None else (out, rode)


def _mm_nt(a, w, out_dtype, name, relu2_grad_of=None, tm=512, tn=1024, rider=None):
    n, k = a.shape
    m = w.shape[0]
    tm = min(tm, n)
    tn = min(tn, m)
    with_h = relu2_grad_of is not None

    def kern(*refs):
        a_ref, w_ref = refs[0], refs[1]
        o_ref = refs[-1]
        r = lax.dot_general(a_ref[...].astype(BF16), w_ref[...], (((1,), (1,)), ((), ())), preferred_element_type=F32)
        if with_h:
            r = r * (2.0 * jnp.maximum(refs[2][...].astype(F32), 0.0))
        o_ref[...] = r.astype(out_dtype)

    in_specs = [pl.BlockSpec((tm, k), lambda i, j: (i, 0)), _weight_spec((tn, k), lambda i, j: (j, 0), tn == m)]
    args = [a, w]
    if with_h:
        in_specs.append(pl.BlockSpec((tm, tn), lambda i, j: (i, j)))
        args.append(relu2_grad_of)
    res, rode = _call(kern, name, (n // tm, m // tn), in_specs, pl.BlockSpec((tm, tn), lambda i, j: (i, j)),
                      _sds((n, m), out_dtype), args, ("parallel", "parallel"), rider=rider)
    return res if rider is None else (res, rode)


def _mm_tn(a, b, name, relu2=False, t1=1024, tn=2048, tk=512):
    n, k1 = a.shape
    m = b.shape[1]
    t1 = min(t1, k1)
    tn = m if m <= 2432 and m % tn else min(tn, m)
    tk = min(tk, n)

    def kern(a_ref, b_ref, o_ref):
        @pl.when(pl.program_id(2) == 0)
        def _():
            o_ref[...] = jnp.zeros_like(o_ref)

        av = a_ref[...]
        if relu2:
            av = _relu2(av.astype(F32))
        o_ref[...] += lax.dot_general(av.astype(BF16), b_ref[...].astype(BF16), (((0,), (0,)), ((), ())),
                                      preferred_element_type=F32)

    return pl.pallas_call(
        kern, name=name, grid=(k1 // t1, m // tn, n // tk),
        in_specs=[pl.BlockSpec((tk, t1), lambda i, j, k: (k, i)), pl.BlockSpec((tk, tn), lambda i, j, k: (k, j))],
        out_specs=pl.BlockSpec((t1, tn), lambda i, j, k: (i, j)),
        out_shape=_sds((k1, m), F32),
        compiler_params=_cp(("parallel", "parallel", "arbitrary")))(a, b)


def _bwd_in(dz_next, da, w, name, ln=None, tm=512, rider=None):
    n, k2 = da.shape
    tm = min(tm, n)
    row = lambda i: (i, 0)
    fix = lambda i: (0, 0)

    def dx_of(dzn_ref, da_ref, w_ref):
        return ALPHA * dzn_ref[...] + lax.dot_general(da_ref[...], w_ref[...], (((1,), (1,)), ((), ())),
                                                      preferred_element_type=F32)

    base_specs = [pl.BlockSpec((tm, D), row), pl.BlockSpec((tm, k2), row), _weight_spec((D, k2), fix, True)]
    if ln is None:
        def kern(dzn_ref, da_ref, w_ref, dx_ref):
            dx_ref[...] = dx_of(dzn_ref, da_ref, w_ref)

        out, rode = _call(kern, name, (n // tm,), base_specs, pl.BlockSpec((tm, D), row), _sds((n, D), F32),
                          (dz_next, da, w), ("parallel",), rider=rider)
        return out if rider is None else (out, rode)

    xhat, rstd, g = ln

    def kern(dzn_ref, da_ref, w_ref, xhat_ref, rstd_ref, g_ref, dz_ref, dg_ref, db_ref):
        @pl.when(pl.program_id(0) == 0)
        def _():
            dg_ref[...] = jnp.zeros_like(dg_ref)
            db_ref[...] = jnp.zeros_like(db_ref)

        dx = dx_of(dzn_ref, da_ref, w_ref)
        xh = xhat_ref[...]
        dg_ref[...] += _fold8(dx * xh)
        db_ref[...] += _fold8(dx)
        dz_ref[...] = _ln_bwd(dx * g_ref[...], xh, rstd_ref[...])

    return pl.pallas_call(
        kern, name=name, grid=(n // tm,),
        in_specs=base_specs + [pl.BlockSpec((tm, D), row), pl.BlockSpec((tm, 1), row), pl.BlockSpec((1, D), fix)],
        out_specs=[pl.BlockSpec((tm, D), row), pl.BlockSpec((8, D), fix), pl.BlockSpec((8, D), fix)],
        out_shape=[_sds((n, D), F32), _sds((8, D), F32), _sds((8, D), F32)],
        compiler_params=_cp(("arbitrary",)))(dz_next, da, w, xhat, rstd, g)


def _loss_lnbwd(x, target, xhat, rstd, g, name, tm=512):
    n = x.shape[0]
    tm = min(tm, n)
    row = lambda i: (i, 0)
    fix = lambda i: (0, 0)

    def kern(x_ref, t_ref, xhat_ref, rstd_ref, g_ref, sq_ref, dz_ref, dg_ref, db_ref):
        @pl.when(pl.program_id(0) == 0)
        def _():
            sq_ref[...] = jnp.zeros_like(sq_ref)
            dg_ref[...] = jnp.zeros_like(dg_ref)
            db_ref[...] = jnp.zeros_like(db_ref)

        err = x_ref[...] - t_ref[...]
        sq_ref[...] += _fold8(err * err)
        dx = err * (1.0 / D)
        xh = xhat_ref[...]
        dg_ref[...] += _fold8(dx * xh)
        db_ref[...] += _fold8(dx)
        dz_ref[...] = _ln_bwd(dx * g_ref[...], xh, rstd_ref[...])

    return pl.pallas_call(
        kern, name=name, grid=(n // tm,),
        in_specs=[pl.BlockSpec((tm, D), row), pl.BlockSpec((tm, D), row), pl.BlockSpec((tm, D), row),
                  pl.BlockSpec((tm, 1), row), pl.BlockSpec((1, D), fix)],
        out_specs=[pl.BlockSpec((8, D), fix), pl.BlockSpec((tm, D), row), pl.BlockSpec((8, D), fix),
                   pl.BlockSpec((8, D), fix)],
        out_shape=[_sds((8, D), F32), _sds((n, D), F32), _sds((8, D), F32), _sds((8, D), F32)],
        compiler_params=_cp(("arbitrary",)))(x, target, xhat, rstd, g)


def _softmax_rows(s):
    s = s - jnp.max(s, axis=-1, keepdims=True)
    e = jnp.exp(s)
    return e / jnp.sum(e, axis=-1, keepdims=True)


def _attn_fwd(q, kv, bd, name, tm=512):
    n = q.shape[0]
    s_len = n // bd
    m_len = kv.shape[0] // bd
    tm = min(tm, s_len)
    nt = s_len // tm
    scale = X_HD ** -0.5

    def kern(q_ref, k_ref, v_ref, o_ref):
        for h in range(X_HEADS):
            cs = slice(h * X_HD, (h + 1) * X_HD)
            s = lax.dot_general(q_ref[:, cs], k_ref[:, cs], (((1,), (1,)), ((), ())), preferred_element_type=F32)
            p = _softmax_rows(s * scale)
            o_ref[:, cs] = jnp.dot(p.astype(BF16), v_ref[:, cs], preferred_element_type=F32).astype(BF16)

    return pl.pallas_call(
        kern, name=name, grid=(bd, nt),
        in_specs=[pl.BlockSpec((tm, D), lambda b, i: (b * nt + i, 0)),
                  pl.BlockSpec((m_len, D), lambda b, i: (b, 0)), pl.BlockSpec((m_len, D), lambda b, i: (b, 1))],
        out_specs=pl.BlockSpec((tm, D), lambda b, i: (b * nt + i, 0)),
        out_shape=_sds((n, D), BF16),
        compiler_params=_cp(("parallel", "parallel")))(q, kv, kv)


def _attn_bwd(q, kv, do, bd, name, tm=512, rider=None):
    n = q.shape[0]
    s_len = n // bd
    m_len = kv.shape[0] // bd
    tm = min(tm, s_len)
    nt = s_len // tm
    scale = X_HD ** -0.5

    def kern(q_ref, k_ref, v_ref, do_ref, dq_ref, dkv_ref):
        @pl.when(pl.program_id(1) == 0)
        def _():
            dkv_ref[...] = jnp.zeros_like(dkv_ref)

        for h in range(X_HEADS):
            cs = slice(h * X_HD, (h + 1) * X_HD)
            vs = slice(D + h * X_HD, D + (h + 1) * X_HD)
            qh, kh, vh, doh = q_ref[:, cs], k_ref[:, cs], v_ref[:, cs], do_ref[:, cs]
            s = lax.dot_general(qh, kh, (((1,), (1,)), ((), ())), preferred_element_type=F32)
            p = _softmax_rows(s * scale)
            pb = p.astype(BF16)
            dp = lax.dot_general(doh, vh, (((1,), (1,)), ((), ())), preferred_element_type=F32)
            dkv_ref[:, vs] += lax.dot_general(pb, doh, (((0,), (0,)), ((), ())), preferred_element_type=F32)
            ds = (p * (dp - jnp.sum(dp * p, axis=-1, keepdims=True)) * scale).astype(BF16)
            dq_ref[:, cs] = jnp.dot(ds, kh, preferred_element_type=F32).astype(BF16)
            dkv_ref[:, cs] += lax.dot_general(ds, qh, (((0,), (0,)), ((), ())), preferred_element_type=F32)

    out, rode = _call(
        kern, name, (bd, nt),
        [pl.BlockSpec((tm, D), lambda b, i: (b * nt + i, 0)),
         pl.BlockSpec((m_len, D), lambda b, i: (b, 0)), pl.BlockSpec((m_len, D), lambda b, i: (b, 1)),
         pl.BlockSpec((tm, D), lambda b, i: (b * nt + i, 0))],
        [pl.BlockSpec((tm, D), lambda b, i: (b * nt + i, 0)), pl.BlockSpec((m_len, 2 * D), lambda b, i: (b, 0))],
        [_sds((n, D), BF16), _sds((bd * m_len, 2 * D), F32)], (q, kv, kv, do), ("parallel", "arbitrary"), rider=rider)
    return out if rider is None else (out, rode)


C_AV, C_AG, C_BU, C_BV, C_CB, C_CC, C_CX = 0, 384, 768, 1024, 1280, 1664, 2048


class _Windows:
    def __init__(self, win_ref, r0, halo, cs):
        self.win = win_ref[pl.ds(r0, CHUNK + halo), cs]
        self.n = CHUNK + halo - 8
        self.shifted = {0: self.win}

    def rows(self, o):
        b, a = o % 8, o // 8
        if b not in self.shifted:
            self.shifted[b] = self.win[b:b + self.n, :]
        return self.shifted[b][8 * a:8 * a + CHUNK, :]


def _conv_taps(win_ref, r0, halo, w_ref, n_taps, offset_of_tap):
    parts = []
    for cb in range(3):
        cs = slice(cb * 128, (cb + 1) * 128)
        win = _Windows(win_ref, r0, halo, cs)
        acc = jnp.zeros((CHUNK, 128), F32)
        for k in range(n_taps):
            acc = acc + win.rows(offset_of_tap(k)) * w_ref[k:k + 1, cs]
        parts.append(acc)
    return jnp.concatenate(parts, axis=1)


def _causal_conv(win_ref, r0, halo, w_ref, n_taps):
    return _conv_taps(win_ref, r0, halo, w_ref, n_taps, lambda k: halo - (n_taps - 1) + k)


def _anticausal_conv(win_ref, r0, halo, w_ref, n_taps):
    return _conv_taps(win_ref, r0, halo, w_ref, n_taps, lambda k: n_taps - 1 - k)


def _head_of_lane():
    return lax.broadcasted_iota(jnp.int32, (1, D_B), 1) // HEAD


def _spatial_mix(wm_ref, vb, head):
    mixed = jnp.zeros((CHUNK, D_B), F32)
    for h in range(B_HEADS):
        mh = jnp.dot(wm_ref[h], vb, preferred_element_type=F32)
        mixed = jnp.where(head == h, mh, mixed)
    return mixed


def _mixer_fwd(proj, p, bd, name, rider=None):
    n = proj.shape[0]
    s_len = n // bd
    n_chunks = s_len // CHUNK

    def kern(proj_ref, caw_ref, cab_ref, lag_ref, lab_ref, lvg_ref, lvb_ref, wm_ref, bsx_ref, ccw_ref, cat_ref,
             ca_ref, gs_ref, ccs_ref):
        gs_ref[0:HALO_A, :] = jnp.zeros((HALO_A, D_A), F32)
        ccs_ref[0:HALO_C, :] = jnp.zeros((HALO_C, D_C), F32)
        head = _head_of_lane()

        def chunk(i, carry):
            r0 = pl.multiple_of(i * CHUNK, CHUNK)
            rows = pl.ds(r0, CHUNK)
            ld = lambda c0, w: proj_ref[rows, c0:c0 + w].astype(F32)
            gs_ref[pl.ds(r0 + HALO_A, CHUNK), :] = ld(C_AV, D_A) * jax.nn.sigmoid(ld(C_AG, D_A))
            ca = _causal_conv(gs_ref, r0, HALO_A, caw_ref, KA) + cab_ref[...]
            ca_ref[rows, :] = ca
            lna = _ln(ca)[0] * lag_ref[...] + lab_ref[...]
            cat_ref[rows, 0:D_A] = (lna * jax.nn.sigmoid(lna)).astype(BF16)
            u = _gelu(ld(C_BU, D_B))
            v = _ln(_gelu(ld(C_BV, D_B)))[0] * lvg_ref[...] + lvb_ref[...]
            mixed = _spatial_mix(wm_ref, v.astype(BF16), head) + bsx_ref[...]
            cat_ref[rows, D_A:D_A + D_B] = (u * mixed).astype(BF16)
            ccs_ref[pl.ds(r0 + HALO_C, CHUNK), :] = ld(C_CC, D_C) * ld(C_CX, D_C)
            conv = _causal_conv(ccs_ref, r0, HALO_C, ccw_ref, KC)
            cat_ref[rows, D_A + D_B:D] = (ld(C_CB, D_C) * conv).astype(BF16)
            return carry

        lax.fori_loop(0, n_chunks, chunk, 0)

    fix2 = lambda b: (0, 0)
    args = [proj, p["caw"], p["cab"], p["lag"], p["lab"], p["lvg"], p["lvb"], p["wm"], p["bsx"], p["ccw"]]
    in_specs = [pl.BlockSpec((s_len, IN_W), lambda b: (b, 0))]
    for a in args[1:]:
        in_specs.append(pl.BlockSpec(a.shape, (lambda b: (0, 0, 0)) if a.ndim == 3 else fix2))
    res, rode = _call(
        kern, name, (bd,), in_specs,
        [pl.BlockSpec((s_len, D), lambda b: (b, 0)), pl.BlockSpec((s_len, D_A), lambda b: (b, 0))],
        [_sds((n, D), BF16), _sds((n, D_A), F32)], args, ("parallel",),
        scratch=[pltpu.VMEM((s_len + HALO_A, D_A), F32), pltpu.VMEM((s_len + HALO_C, D_C), F32)], rider=rider)
    return res if rider is None else (res, rode)


def _mixer_bwd(proj, dcat, ca, p, bd, name, rider=None):
    n = proj.shape[0]
    s_len = n // bd
    n_chunks = s_len // CHUNK

    def kern(proj_ref, dcat_ref, ca_ref, caw_ref, cab_ref, lag_ref, lab_ref, lvg_ref, lvb_ref, wm_ref, bsx_ref, ccw_ref,
             dproj_ref, dcaw_ref, dcab_ref, dlag_ref, dlab_ref, dlvg_ref, dlvb_ref, dws_ref, dbs_ref, dccw_ref,
             gs_ref, dcas_ref, ccs_ref, dcs_ref, a_caw, a_cab, a_lag, a_lab, a_lvg, a_lvb, a_ccw):
        gs_ref[0:HALO_A, :] = jnp.zeros((HALO_A, D_A), F32)
        ccs_ref[0:HALO_C, :] = jnp.zeros((HALO_C, D_C), F32)
        dcas_ref[s_len:s_len + HALO_A, :] = jnp.zeros((HALO_A, D_A), F32)
        dcs_ref[s_len:s_len + HALO_C, :] = jnp.zeros((HALO_C, D_C), F32)
        for acc in (a_caw, a_cab, a_lag, a_lab, a_lvg, a_lvb, a_ccw, dws_ref, dbs_ref):
            acc[...] = jnp.zeros_like(acc)
        head = _head_of_lane()
        lane128 = lax.broadcasted_iota(jnp.int32, (1, CHUNK), 1)

        def pass1(i, carry):
            r0 = pl.multiple_of(i * CHUNK, CHUNK)
            rows = pl.ds(r0, CHUNK)
            ld = lambda c0, w: proj_ref[rows, c0:c0 + w].astype(F32)
            dld = lambda c0, w: dcat_ref[rows, c0:c0 + w].astype(F32)
            gs_ref[pl.ds(r0 + HALO_A, CHUNK), :] = ld(C_AV, D_A) * jax.nn.sigmoid(ld(C_AG, D_A))
            xh, rstd = _ln(ca_ref[rows, :])
            lna = xh * lag_ref[...] + lab_ref[...]
            sg = jax.nn.sigmoid(lna)
            dlna = dld(0, D_A) * (sg * (1.0 + lna * (1.0 - sg)))
            a_lag[...] += _fold8(dlna * xh)
            a_lab[...] += _fold8(dlna)
            dca = _ln_bwd(dlna * lag_ref[...], xh, rstd)
            dcas_ref[rows, :] = dca
            a_cab[...] += _fold8(dca)
            for cb in range(3):
                cs = slice(cb * 128, (cb + 1) * 128)
                win = _Windows(gs_ref, r0, HALO_A, cs)
                dcab = dca[:, cs]
                for k in range(KA):
                    a_caw[k * 8:(k + 1) * 8, cs] += _fold8(dcab * win.rows(HALO_A - (KA - 1) + k))
            pu, pv = ld(C_BU, D_B), ld(C_BV, D_B)
            u = _gelu(pu)
            vxh, vrstd = _ln(_gelu(pv))
            v = vxh * lvg_ref[...] + lvb_ref[...]
            vb = v.astype(BF16)
            mixed = _spatial_mix(wm_ref, vb, head) + bsx_ref[...]
            dbo = dld(D_A, D_B)
            dproj_ref[rows, C_BU:C_BU + D_B] = (dbo * mixed * _gelu_grad(pu)).astype(BF16)
            dmixed = dbo * u
            dv = jnp.zeros((CHUNK, D_B), F32)
            bsum = jnp.zeros((CHUNK, CHUNK), F32)
            for h in range(B_HEADS):
                dmh = jnp.where(head == h, dmixed, 0.0)
                dmb = dmh.astype(BF16)
                dvh = lax.dot_general(wm_ref[h], dmb, (((0,), (0,)), ((), ())), preferred_element_type=F32)
                dv = jnp.where(head == h, dvh, dv)
                dws_ref[h] += lax.dot_general(dmb, vb, (((1,), (1,)), ((), ())), preferred_element_type=F32)
                bsum = bsum + jnp.where(lane128 == h, jnp.sum(dmh, axis=-1, keepdims=True), 0.0)
            dbs_ref[...] += bsum
            a_lvg[...] += _fold8(dv * vxh)
            a_lvb[...] += _fold8(dv)
            dgv = _ln_bwd(dv * lvg_ref[...], vxh, vrstd)
            dproj_ref[rows, C_BV:C_BV + D_B] = (dgv * _gelu_grad(pv)).astype(BF16)
            ccs_ref[pl.ds(r0 + HALO_C, CHUNK), :] = ld(C_CC, D_C) * ld(C_CX, D_C)
            conv = _causal_conv(ccs_ref, r0, HALO_C, ccw_ref, KC)
            dco = dld(D_A + D_B, D_C)
            dproj_ref[rows, C_CB:C_CB + D_C] = (dco * conv).astype(BF16)
            dconv = dco * ld(C_CB, D_C)
            dcs_ref[rows, :] = dconv
            for cb in range(3):
                cs = slice(cb * 128, (cb + 1) * 128)
                win = _Windows(ccs_ref, r0, HALO_C, cs)
                for k in range(KC):
                    a_ccw[k * 8:(k + 1) * 8, cs] += _fold8(dconv[:, cs] * win.rows(HALO_C - (KC - 1) + k))
            return carry

        lax.fori_loop(0, n_chunks, pass1, 0)

        def pass2(i, carry):
            r0 = pl.multiple_of(i * CHUNK, CHUNK)
            rows = pl.ds(r0, CHUNK)
            ld = lambda c0, w: proj_ref[rows, c0:c0 + w].astype(F32)
            dg = _anticausal_conv(dcas_ref, r0, HALO_A, caw_ref, KA)
            pa = ld(C_AV, D_A)
            sg = jax.nn.sigmoid(ld(C_AG, D_A))
            dproj_ref[rows, C_AV:C_AV + D_A] = (dg * sg).astype(BF16)
            dproj_ref[rows, C_AG:C_AG + D_A] = (dg * pa * sg * (1.0 - sg)).astype(BF16)
            dcc = _anticausal_conv(dcs_ref, r0, HALO_C, ccw_ref, KC)
            dproj_ref[rows, C_CC:C_CC + D_C] = (dcc * ld(C_CX, D_C)).astype(BF16)
            dproj_ref[rows, C_CX:C_CX + D_C] = (dcc * ld(C_CC, D_C)).astype(BF16)
            return carry

        lax.fori_loop(0, n_chunks, pass2, 0)

        for k in range(KA):
            dcaw_ref[k:k + 1, :] = jnp.sum(a_caw[k * 8:(k + 1) * 8, :], axis=0, keepdims=True)
        dcaw_ref[KA:KA + 1, :] = jnp.zeros((1, D_A), F32)
        for k in range(8):
            if k < KC:
                dccw_ref[k:k + 1, :] = jnp.sum(a_ccw[k * 8:(k + 1) * 8, :], axis=0, keepdims=True)
            else:
                dccw_ref[k:k + 1, :] = jnp.zeros((1, D_C), F32)
        dcab_ref[...] = a_cab[...]
        dlag_ref[...] = a_lag[...]
        dlab_ref[...] = a_lab[...]
        dlvg_ref[...] = a_lvg[...]
        dlvb_ref[...] = a_lvb[...]

    fix2 = lambda b: (0, 0)
    args = [proj, dcat, ca, p["caw"], p["cab"], p["lag"], p["lab"], p["lvg"], p["lvb"], p["wm"], p["bsx"], p["ccw"]]
    once = pl.Buffered(1)
    in_specs = [pl.BlockSpec((s_len, IN_W), lambda b: (b, 0), pipeline_mode=once),
                pl.BlockSpec((s_len, D), lambda b: (b, 0), pipeline_mode=once),
                pl.BlockSpec((s_len, D_A), lambda b: (b, 0), pipeline_mode=once)]
    for a in args[3:]:
        in_specs.append(pl.BlockSpec(a.shape, (lambda b: (0, 0, 0)) if a.ndim == 3 else fix2))

    def per_seq(*shape):
        nd = len(shape)
        return (pl.BlockSpec((None,) + shape, lambda b: (b,) + (0,) * nd), _sds((bd,) + shape, F32))

    outs = [(pl.BlockSpec((s_len, IN_W), lambda b: (b, 0), pipeline_mode=once), _sds((n, IN_W), BF16)),
            per_seq(32, D_A), per_seq(8, D_A), per_seq(8, D_A), per_seq(8, D_A), per_seq(8, D_B), per_seq(8, D_B),
            per_seq(B_HEADS, CHUNK, CHUNK), per_seq(CHUNK, CHUNK), per_seq(8, D_C)]
    res, rode = _call(
        kern, name, (bd,), in_specs, [o[0] for o in outs], [o[1] for o in outs], args, ("parallel",),
        scratch=[pltpu.VMEM((s_len + HALO_A, D_A), F32), pltpu.VMEM((s_len + HALO_A, D_A), F32),
                 pltpu.VMEM((s_len + HALO_C, D_C), F32), pltpu.VMEM((s_len + HALO_C, D_C), F32),
                 pltpu.VMEM((KA * 8, D_A), F32), pltpu.VMEM((8, D_A), F32), pltpu.VMEM((8, D_A), F32),
                 pltpu.VMEM((8, D_A), F32), pltpu.VMEM((8, D_B), F32), pltpu.VMEM((8, D_B), F32),
                 pltpu.VMEM((KC * 8, D_C), F32)],
        rider=rider)
    return res if rider is None else (res, rode)


def _adamw(w, m, v, g_parts, name):
    shape = w.shape
    c = shape[-1]
    r = math.prod(shape[:-1])
    as2d = lambda a: a.reshape(r, c)
    tr = 512 if r % 512 == 0 else r
    slots = [(a.reshape(a.shape[0], r, c), p) for a in g_parts for p in range(a.shape[0])]
    n_g = len(slots)

    def kern(*refs):
        w_ref, m_ref, v_ref = refs[:3]
        g_refs = refs[3:3 + n_g]
        go_ref, d_ref, mo_ref, vo_ref = refs[3 + n_g:]
        g = g_refs[0][...].astype(F32)
        for gr in g_refs[1:]:
            g = g + gr[...].astype(F32)
        mn = B1 * m_ref[...] + (1.0 - B1) * g
        vn = B2 * v_ref[...] + (1.0 - B2) * jnp.square(g)
        m_hat = mn / (1.0 - B1 ** STEP)
        v_hat = vn / (1.0 - B2 ** STEP)
        go_ref[...] = g
        mo_ref[...] = mn
        vo_ref[...] = vn
        d_ref[...] = -LR * (m_hat / (jnp.sqrt(v_hat) + ADAM_EPS) + WD * w_ref[...])

    spec = pl.BlockSpec((tr, c), lambda i: (i, 0))
    g_specs = [pl.BlockSpec((None, tr, c), functools.partial(lambda p, i: (p, i, 0), p)) for _, p in slots]
    outs = pl.pallas_call(
        kern, name=name, grid=(r // tr,), in_specs=[spec] * 3 + g_specs, out_specs=[spec] * 4,
        out_shape=[_sds((r, c), F32)] * 4, compiler_params=_cp(("parallel",)))(
            as2d(w), as2d(m), as2d(v), *[a for a, _ in slots])
    return [o.reshape(shape) for o in outs]


def _place():
    x, y, c = lax.axis_index("x"), lax.axis_index("y"), lax.axis_index("c")
    other_chips = [(1 - x, y), (x, 1 - y), (1 - x, 1 - y)]
    return x, y, c, other_chips


def _gather_rider(arrays):
    n_arr = len(arrays)

    def parts(ins, outs, sems):
        send_sems, recv_sems, local_sems = sems
        x, y, c, chips = _place()
        me, sibling = (x, y, c), (x, y, 1 - c)
        slot = lambda px, py, pc: 4 * px + 2 * py + pc

        def copy(a, k, block, to, from_input=False):
            dst = outs[a].at[slot(*block)]
            return pltpu.make_async_remote_copy(
                src_ref=ins[a] if from_input else dst, dst_ref=dst, send_sem=send_sems.at[k, a],
                recv_sem=recv_sems.at[k, a], device_id=to, device_id_type=MESH)

        mine = [pltpu.make_async_copy(ins[a], outs[a].at[slot(*me)], local_sems.at[a]) for a in range(n_arr)]
        first = []
        for a in range(n_arr):
            first.append(copy(a, 0, me, sibling, True))
            first += [copy(a, 1 + j, me, (*chip, c), True) for j, chip in enumerate(chips)]
        return copy, mine, first, me, sibling, chips, c

    def start(ins, outs, sems):
        _, mine, first, *_ = parts(ins, outs, sems)
        for cp in mine + first:
            cp.start()

    def finish(ins, outs, sems):
        copy, mine, first, me, sibling, chips, c = parts(ins, outs, sems)
        passed = []
        for j, chip in enumerate(chips):
            for a in range(n_arr):
                copy(a, 1 + j, (*chip, c), me).wait_recv()
                passed.append(copy(a, 4 + j, (*chip, c), sibling))
                passed[-1].start()
        for a in range(n_arr):
            copy(a, 0, sibling, me).wait_recv()
            for j, chip in enumerate(chips):
                copy(a, 4 + j, (*chip, 1 - c), me).wait_recv()
        for cp in first + passed:
            cp.wait_send()
        for cp in mine:
            cp.wait()

    return _Rider(list(arrays), [_sds((N_DEV,) + a.shape, a.dtype) for a in arrays],
                  [pltpu.SemaphoreType.DMA((7, n_arr)), pltpu.SemaphoreType.DMA((7, n_arr)),
                   pltpu.SemaphoreType.DMA((n_arr,))], start, finish)


def _exchange_rider(arrays):
    n_arr = len(arrays)

    def copies(ins, outs, sems):
        send_sems, recv_sems = sems
        x, y, c, _ = _place()
        return [pltpu.make_async_remote_copy(
            src_ref=ins[a].at[1 - c], dst_ref=outs[a], send_sem=send_sems.at[a], recv_sem=recv_sems.at[a],
            device_id=(x, y, 1 - c), device_id_type=MESH) for a in range(n_arr)]

    def start(ins, outs, sems):
        for cp in copies(ins, outs, sems):
            cp.start()

    def finish(ins, outs, sems):
        for cp in copies(ins, outs, sems):
            cp.wait()

    return _Rider(list(arrays), [_sds(a.shape[1:], a.dtype) for a in arrays],
                  [pltpu.SemaphoreType.DMA((n_arr,)), pltpu.SemaphoreType.DMA((n_arr,))], start, finish)


def _pair_add(mine, theirs, core, name):
    _, r, c = mine.shape
    tr = 512 if r % 512 == 0 else r

    def kern(core_ref, a_ref, b_ref, o_ref, ob_ref):
        s = a_ref[...] + b_ref[...]
        o_ref[...] = s
        ob_ref[...] = s.astype(BF16)

    row = lambda i, core_ref: (i, 0)
    return pl.pallas_call(
        kern, name=name,
        grid_spec=pltpu.PrefetchScalarGridSpec(
            num_scalar_prefetch=1, grid=(r // tr,),
            in_specs=[pl.BlockSpec((None, tr, c), lambda i, core_ref: (core_ref[0], i, 0)), pl.BlockSpec((tr, c), row)],
            out_specs=[pl.BlockSpec((tr, c), row), pl.BlockSpec((tr, c), row)]),
        out_shape=[_sds((r, c), F32), _sds((r, c), BF16)], compiler_params=_cp(("parallel",)))(core, mine, theirs)


def _scatter_rider(sums, sums_bf16):
    n_arr = len(sums)

    def copies(ins, outs, sems):
        send_sems, recv_sems, local_sems = sems
        x, y, c, chips = _place()
        own = [pltpu.make_async_copy(ins[a].at[2 * x + y], outs[a], local_sems.at[a]) for a in range(n_arr)]
        remote = []
        for a in range(n_arr):
            for j, (px, py) in enumerate(chips):
                remote.append(pltpu.make_async_remote_copy(
                    src_ref=ins[n_arr + a].at[2 * px + py], dst_ref=outs[n_arr + a].at[j], send_sem=send_sems.at[j, a],
                    recv_sem=recv_sems.at[j, a], device_id=(px, py, c), device_id_type=MESH))
        return own + remote

    def start(ins, outs, sems):
        for cp in copies(ins, outs, sems):
            cp.start()

    def finish(ins, outs, sems):
        for cp in copies(ins, outs, sems):
            cp.wait()

    return _Rider(list(sums) + list(sums_bf16),
                  [_sds(a.shape[1:], a.dtype) for a in sums] + [_sds((3,) + a.shape[1:], a.dtype) for a in sums_bf16],
                  [pltpu.SemaphoreType.DMA((3, n_arr)), pltpu.SemaphoreType.DMA((3, n_arr)),
                   pltpu.SemaphoreType.DMA((n_arr,))], start, finish)


def _mixer_params(conv_a_w, conv_a_b, ln_a_g, ln_a_b, ln_v_g, ln_v_b, w_s, b_s, conv_c_w):
    causal = jnp.tril(jnp.ones((CHUNK, CHUNK), dtype=bool))
    row = lambda a: a.reshape(1, -1)
    return dict(
        caw=jnp.pad(conv_a_w, ((0, 32 - KA), (0, 0))), cab=row(conv_a_b), lag=row(ln_a_g), lab=row(ln_a_b),
        lvg=row(ln_v_g), lvb=row(ln_v_b), wm=jnp.where(causal[None], w_s, 0.0).astype(BF16),
        bsx=jnp.repeat(b_s.T, HEAD, axis=1), ccw=jnp.pad(conv_c_w, ((0, 8 - KC), (0, 0))))


class _Schedule:
    def __init__(self, big=None):
        self.big = big

    def weights(self, l):
        return {k: v[l] for k, v in self.big.items()}

    def rider(self, stage, l):
        return None

    def rode(self, stage, l, results):
        pass

    def note_grads(self, l, grads):
        pass

    def finish(self):
        pass


def _local_step(x, mem, target, sched, small, bd):
    row = lambda a: a.reshape(1, -1)

    def ride(stage, l, fn, *args, **kw):
        rider = sched.rider(stage, l)
        res = fn(*args, rider=rider, **kw)
        if rider is not None:
            res, results = res
            sched.rode(stage, l, results)
        return res

    saved = []
    for l in range(DEPTH):
        big = sched.weights(l)
        mp = _mixer_params(big["conv_a_w"], *[small[k][l] for k in ("conv_a_b", "ln_a_g", "ln_a_b", "ln_v_g", "ln_v_b",
                                                                    "w_s", "b_s")], big["conv_c_w"])
        proj = ride("in_proj", l, _mm_nn, x, big["w_in"], BF16, f"in_proj_{l}")
        cat, ca = ride("mixer_fwd", l, _mixer_fwd, proj, mp, bd, f"mixer_fwd_{l}")
        x1, xh1, rs1 = _mm_res_ln(cat, big["w_out"], x, row(small["ln1_g"][l]), row(small["ln1_b"][l]),
                                  f"out_proj_ln1_{l}")
        q = _mm_nn(x1, big["w_q"], BF16, f"q_proj_{l}", tm=1024)
        kv = _mm_nn(mem, big["w_kv"], BF16, f"kv_proj_{l}")
        o = _attn_fwd(q, kv, bd, f"attn_fwd_{l}")
        x2, xh2, rs2 = _mm_res_ln(o, big["w_o"], x1, row(small["ln2_g"][l]), row(small["ln2_b"][l]), f"o_proj_ln2_{l}")
        h = ride("ff1", l, _mm_nn, x2, big["w_ff1"], BF16, f"ff1_{l}")
        x3, xh3, rs3 = ride("ff2_ln3", l, _mm_res_ln, h, big["w_ff2"], x2, row(small["ln3_g"][l]),
                            row(small["ln3_b"][l]), f"ff2_ln3_{l}", relu2=True)
        saved.append(dict(mp=mp, x=x, proj=proj, cat=cat, ca=ca, x1=x1, xh1=xh1, rs1=rs1, q=q, kv=kv, o=o, x2=x2, xh2=xh2,
                          rs2=rs2, h=h, xh3=xh3, rs3=rs3))
        x = x3

    grads = {k: [None] * DEPTH for k in WEIGHTS}
    s = saved[-1]
    sq, dz3, dg, db = _loss_lnbwd(x, target, s["xh3"], s["rs3"], row(small["ln3_g"][DEPTH - 1]), "loss_ln3_bwd")
    grad_x = None
    causal = jnp.tril(jnp.ones((CHUNK, CHUNK), dtype=bool))
    for l in reversed(range(DEPTH)):
        s = saved[l]
        big = sched.weights(l)
        grads["ln3_g"][l], grads["ln3_b"][l] = jnp.sum(dg, axis=0), jnp.sum(db, axis=0)
        dh = ride("ff2_bwd", l, _mm_nt, dz3, big["w_ff2"], BF16, f"ff2_bwd_{l}", relu2_grad_of=s["h"], tn=D_FF)
        grads["w_ff2"][l] = _mm_tn(s["h"], dz3, f"ff2_wgrad_{l}", relu2=True, t1=2048)
        grads["w_ff1"][l] = _mm_tn(s["x2"], dh, f"ff1_wgrad_{l}")
        sched.note_grads(l, {k: grads[k][l] for k in ("w_ff1", "w_ff2")})
        dz2, dg, db = _bwd_in(dz3, dh, big["w_ff1"], f"ff1_bwd_ln2_{l}", ln=(s["xh2"], s["rs2"], row(small["ln2_g"][l])))
        grads["ln2_g"][l], grads["ln2_b"][l] = jnp.sum(dg, axis=0), jnp.sum(db, axis=0)
        do = _mm_nt(dz2, big["w_o"], BF16, f"o_proj_bwd_{l}", tm=1024)
        grads["w_o"][l] = _mm_tn(s["o"], dz2, f"o_proj_wgrad_{l}")
        dq, dkv = ride("attn_bwd", l, _attn_bwd, s["q"], s["kv"], do, bd, f"attn_bwd_{l}")
        grads["w_q"][l] = _mm_tn(s["x1"], dq, f"q_wgrad_{l}")
        grads["w_kv"][l] = _mm_tn(mem, dkv, f"kv_wgrad_{l}")
        dz1, dg, db = _bwd_in(dz2, dq, big["w_q"], f"q_bwd_ln1_{l}", ln=(s["xh1"], s["rs1"], row(small["ln1_g"][l])))
        grads["ln1_g"][l], grads["ln1_b"][l] = jnp.sum(dg, axis=0), jnp.sum(db, axis=0)
        dcat = _mm_nt(dz1, big["w_out"], BF16, f"out_proj_bwd_{l}", tm=1024)
        grads["w_out"][l] = _mm_tn(s["cat"], dz1, f"out_proj_wgrad_{l}")
        (dproj, dcaw, dcab, dlag, dlab, dlvg, dlvb, dws, dbs, dccw) = ride(
            "mixer_bwd", l, _mixer_bwd, s["proj"], dcat, s["ca"], s["mp"], bd, f"mixer_bwd_{l}")
        grads["conv_a_w"][l] = jnp.sum(dcaw, axis=0)[:KA]
        grads["conv_a_b"][l] = jnp.sum(dcab, axis=(0, 1))
        grads["ln_a_g"][l] = jnp.sum(dlag, axis=(0, 1))
        grads["ln_a_b"][l] = jnp.sum(dlab, axis=(0, 1))
        grads["ln_v_g"][l] = jnp.sum(dlvg, axis=(0, 1))
        grads["ln_v_b"][l] = jnp.sum(dlvb, axis=(0, 1))
        grads["w_s"][l] = jnp.where(causal[None], jnp.sum(dws, axis=0), 0.0)
        grads["b_s"][l] = jnp.sum(dbs, axis=0)[:, :B_HEADS].T
        grads["conv_c_w"][l] = jnp.sum(dccw, axis=0)[:KC]
        grads["w_in"][l] = _mm_tn(s["x"], dproj, f"in_proj_wgrad_{l}")
        sched.note_grads(l, {k: v[l] for k, v in grads.items() if k not in ("w_ff1", "w_ff2")})
        if l > 0:
            p = saved[l - 1]
            dz3, dg, db = _bwd_in(dz1, dproj, big["w_in"], f"in_proj_bwd_ln3_{l}",
                                  ln=(p["xh3"], p["rs3"], row(small["ln3_g"][l - 1])))
        else:
            grad_x = ride("in_proj_bwd", 0, _bwd_in, dz1, dproj, big["w_in"], "in_proj_bwd_0")
    sched.finish()
    return sq, grad_x, grads


WEIGHTS = ("w_in", "conv_a_w", "conv_a_b", "ln_a_g", "ln_a_b", "ln_v_g", "ln_v_b", "w_s", "b_s", "conv_c_w", "w_out",
           "ln1_g", "ln1_b", "w_q", "w_kv", "w_o", "ln2_g", "ln2_b", "w_ff1", "w_ff2", "ln3_g", "ln3_b")
COL_SHARDED = ("w_in", "w_kv", "w_ff1")
ROW_SHARDED = ("w_out", "w_q", "w_o", "w_ff2")
BIG = COL_SHARDED + ROW_SHARDED
REPLICATED = tuple(k for k in WEIGHTS if k not in BIG and k not in ("conv_a_w", "conv_c_w"))
PACK_LANES = 128


CONV_ROWS = 32 + 8
GATHER_LAYER0 = {"first": ("w_in", "conv"), "in_proj": ("w_out", "w_q", "w_kv", "w_o"), "mixer_fwd": ("w_ff1", "w_ff2")}
GATHER_NEXT = {"ff1": ("w_ff2", "w_in", "w_out", "w_q", "w_o", "conv"), "ff2_ln3": ("w_ff1", "w_kv")}
GRADS_EARLY = ("w_ff1", "w_ff2")
GRADS_LATE = ("w_in", "w_kv", "w_out", "w_q", "w_o", "conv")


def _gathered_to_full(g, col_sharded):
    _, a, b = g.shape
    if col_sharded:
        return g.transpose(1, 0, 2).reshape(a, N_DEV * b)
    return g.reshape(N_DEV * a, b)


def _full_to_owner_major(g, col_sharded):
    a, b = g.shape
    if col_sharded:
        return g.reshape(a, 4, 2, b // N_DEV).transpose(2, 1, 0, 3)
    return g.reshape(4, 2, a // N_DEV, b).transpose(1, 0, 2, 3)


def _conv_pack(conv_a, conv_c):
    pad = lambda a, rows: jnp.pad(a, [(0, 0)] * (a.ndim - 2) + [(0, rows - a.shape[-2]), (0, 0)])
    return jnp.concatenate([pad(conv_a, 32), pad(conv_c, 8)], axis=-2)


def _conv_unpack(packed):
    return packed[..., :KA, :], packed[..., 32:32 + KC, :]


class _Overlapped(_Schedule):
    def __init__(self, shards_bf16, conv_shards, core):
        self.shards, self.conv_shards, self.core = shards_bf16, conv_shards, core
        self.full = {l: {} for l in range(DEPTH)}
        self.grads = {l: {} for l in range(DEPTH)}
        self.owner_major = {}
        self.from_sibling = {}
        self.scattering = None
        self.own, self.remote = {}, {}
        self.replicated = None

    def _gather(self, l, names):
        return _gather_rider([self.conv_shards[l] if k == "conv" else self.shards[k][l] for k in names])

    def _store(self, l, names, gathered):
        for k, g in zip(names, gathered):
            if k == "conv":
                self.full[l]["conv_a_w"], self.full[l]["conv_c_w"] = _conv_unpack(_gathered_to_full(g, True))
            else:
                self.full[l][k] = _gathered_to_full(g, k in COL_SHARDED)

    def weights(self, l):
        if l == 0 and not self.full[0]:
            names = GATHER_LAYER0["first"]
            self._store(0, names, _ride_alone(self._gather(0, names), "weights_all_gather_first"))
        return self.full[l]

    def note_grads(self, l, grads):
        self.grads[l].update(grads)

    def _owner_major(self, l, k):
        if (l, k) not in self.owner_major:
            g = self.grads[l]
            if k == "conv":
                self.owner_major[(l, k)] = _full_to_owner_major(_conv_pack(g["conv_a_w"], g["conv_c_w"]), True)
            else:
                self.owner_major[(l, k)] = _full_to_owner_major(g[k], k in COL_SHARDED)
        return self.owner_major[(l, k)]

    def _exchange(self, l, names):
        return _exchange_rider([self._owner_major(l, k) for k in names])

    def _scatter(self, groups):
        sums, sums_bf16, self.scattering = [], [], []
        for l, names in groups:
            for k, r in zip(names, self.from_sibling.pop((l, names))):
                rows, cols = math.prod(r.shape[:-1]), r.shape[-1]
                s, sb = _pair_add(self._owner_major(l, k).reshape(2, rows, cols), r.reshape(rows, cols), self.core,
                                  f"grad_pair_add_{l}_{k}")
                sums.append(s.reshape(r.shape))
                sums_bf16.append(sb.reshape(r.shape))
                self.scattering.append((l, k))
        return _scatter_rider(sums, sums_bf16)

    def _scattered(self, results):
        n = len(self.scattering)
        for i, key in enumerate(self.scattering):
            self.own[key], self.remote[key] = results[i], results[n + i]

    def rider(self, stage, l):
        if l == 0 and stage in ("in_proj", "mixer_fwd"):
            return self._gather(0, GATHER_LAYER0[stage])
        if stage in GATHER_NEXT and l + 1 < DEPTH:
            return self._gather(l + 1, GATHER_NEXT[stage])
        if stage == "ff2_bwd" and l + 1 < DEPTH:
            return self._exchange(l + 1, GRADS_LATE)
        if stage == "attn_bwd":
            return self._exchange(l, GRADS_EARLY)
        if stage == "mixer_bwd":
            return self._scatter([(l, GRADS_EARLY)] + ([(l + 1, GRADS_LATE)] if l + 1 < DEPTH else []))
        if stage == "in_proj_bwd":
            packed = _pack_rows([jnp.stack([self.grads[i][k] for i in range(DEPTH)]) for k in REPLICATED])
            return _gather_rider([packed])
        return None

    def rode(self, stage, l, results):
        if l == 0 and stage in ("in_proj", "mixer_fwd"):
            self._store(0, GATHER_LAYER0[stage], results)
        elif stage in GATHER_NEXT:
            self._store(l + 1, GATHER_NEXT[stage], results)
        elif stage == "ff2_bwd":
            self.from_sibling[(l + 1, GRADS_LATE)] = results
        elif stage == "attn_bwd":
            self.from_sibling[(l, GRADS_EARLY)] = results
        elif stage == "mixer_bwd":
            self._scattered(results)
        elif stage == "in_proj_bwd":
            self.replicated = results[0]

    def finish(self):
        self.from_sibling[(0, GRADS_LATE)] = _ride_alone(self._exchange(0, GRADS_LATE), "grad_pair_exchange_last")
        self._scattered(_ride_alone(self._scatter([(0, GRADS_LATE)]), "grad_chip_scatter_last"))


def _pack_rows(parts):
    flat = jnp.concatenate([p.reshape(-1, PACK_LANES) for p in parts], axis=0)
    return jnp.pad(flat, ((0, -flat.shape[0] % 8), (0, 0)))


def _unpack_rows(packed, like):
    out, r = [], 0
    for p in like:
        n = p.size // PACK_LANES
        out.append(packed[r:r + n].reshape(p.shape))
        r += n
    return out


def kernel(x, mem, w_in, conv_a_w, conv_a_b, ln_a_g, ln_a_b, ln_v_g, ln_v_b, w_s, b_s, conv_c_w, w_out, ln1_g, ln1_b, w_q, w_kv, w_o, ln2_g, ln2_b, w_ff1, w_ff2, ln3_g, ln3_b, loss_target, m_w_in, m_conv_a_w, m_conv_a_b, m_ln_a_g, m_ln_a_b, m_ln_v_g, m_ln_v_b, m_w_s, m_b_s, m_conv_c_w, m_w_out, m_ln1_g, m_ln1_b, m_w_q, m_w_kv, m_w_o, m_ln2_g, m_ln2_b, m_w_ff1, m_w_ff2, m_ln3_g, m_ln3_b, v_w_in, v_conv_a_w, v_conv_a_b, v_ln_a_g, v_ln_a_b, v_ln_v_g, v_ln_v_b, v_w_s, v_b_s, v_conv_c_w, v_w_out, v_ln1_g, v_ln1_b, v_w_q, v_w_kv, v_w_o, v_ln2_g, v_ln2_b, v_w_ff1, v_w_ff2, v_ln3_g, v_ln3_b):
    given = dict(locals())
    w = {k: given[k] for k in WEIGHTS}
    mom = {k: given["m_" + k] for k in WEIGHTS}
    var = {k: given["v_" + k] for k in WEIGHTS}
    bd, s_len, _ = x.shape
    core = lax.axis_index("c").astype(jnp.int32).reshape(1)

    conv_pack = lambda d: _conv_pack(d["conv_a_w"], d["conv_c_w"])
    sched = _Overlapped({k: w[k].astype(BF16) for k in BIG}, conv_pack(w), core)
    sq, grad_x, grads = _local_step(x.reshape(bd * s_len, D), mem.reshape(-1, D), loss_target.reshape(bd * s_len, D),
                                    sched, {k: w[k] for k in REPLICATED}, bd)
    loss = lax.psum(0.5 * jnp.sum(sq) / D, ("x", "y", "c"))

    out = {}
    for k in BIG + ("conv",):
        own = jnp.stack([sched.own[(l, k)] for l in range(DEPTH)])[None]
        remote = jnp.stack([sched.remote[(l, k)] for l in range(DEPTH)], axis=1)
        if k == "conv":
            conv_out = _adamw(conv_pack(w), conv_pack(mom), conv_pack(var), [own, remote], "adamw_conv")
            unpacked = [_conv_unpack(o) for o in conv_out]
            out["conv_a_w"], out["conv_c_w"] = [u[0] for u in unpacked], [u[1] for u in unpacked]
        else:
            out[k] = _adamw(w[k], mom[k], var[k], [own, remote], f"adamw_{k}")

    rep_out = _adamw(_pack_rows([w[k] for k in REPLICATED]), _pack_rows([mom[k] for k in REPLICATED]),
                     _pack_rows([var[k] for k in REPLICATED]), [sched.replicated], "adamw_replicated")
    for i, o in enumerate(rep_out):
        for k, piece in zip(REPLICATED, _unpack_rows(o, [w[k] for k in REPLICATED])):
            out.setdefault(k, [None] * 4)[i] = piece

    res = [loss, grad_x.reshape(bd, s_len, D)]
    for i in range(4):
        res += [out[k][i] for k in WEIGHTS]
    return tuple(res)
```

```python
import functools
import math

import jax
import jax.numpy as jnp
from jax import lax
from jax.experimental import pallas as pl
from jax.experimental.pallas import tpu as pltpu

F32 = jnp.float32
BF16 = jnp.bfloat16

DEPTH = 4
D = 1024
D_A, D_B, D_C = 384, 256, 384
HEAD = 64
B_HEADS = 4
CHUNK = 128
KA, KC = 31, 3
HALO_A, HALO_C = 32, 8
IN_W = 2 * D_A + 2 * D_B + 3 * D_C
X_HEADS = 4
X_HD = D // X_HEADS
D_FF = 4 * D
EPS = 1e-5
ALPHA = (2.0 * DEPTH) ** 0.25
LR, B1, B2, ADAM_EPS, WD, STEP = 0.001, 0.9, 0.999, 1e-08, 0.01, 10
INV_SQRT2 = 0.7071067811865476
INV_SQRT_2PI = 0.3989422804014327
N_DEV = 8
VMEM_LIMIT = 56 * 1024 * 1024
MESH = pl.DeviceIdType.MESH
ANY = pl.BlockSpec(memory_space=pl.ANY)


def _cp(sem=None):
    return pltpu.CompilerParams(dimension_semantics=sem, vmem_limit_bytes=VMEM_LIMIT)


def _sds(shape, dtype):
    return jax.ShapeDtypeStruct(tuple(shape), dtype)


class _Rider:
    def __init__(self, arrays, out_shape, sems, start, finish):
        self.arrays, self.out_shape, self.sems, self.start, self.finish = arrays, out_shape, sems, start, finish


def _call(kern, name, grid, in_specs, out_specs, out_shape, args, sem, scratch=(), rider=None):
    single = not isinstance(out_shape, (list, tuple))
    out_specs_l = [out_specs] if single else list(out_specs)
    out_shape_l = [out_shape] if single else list(out_shape)
    if rider is None:
        res = pl.pallas_call(kern, name=name, grid=grid, in_specs=in_specs, out_specs=out_specs_l,
                             out_shape=out_shape_l, scratch_shapes=list(scratch), compiler_params=_cp(sem))(*args)
        return (res[0] if single else list(res)), None
    n_in, n_out, n_scr = len(args), len(out_shape_l), len(scratch)
    n_rin, n_rout = len(rider.arrays), len(rider.out_shape)

    def body(*refs):
        ins, refs = refs[:n_in], refs[n_in:]
        r_ins, refs = refs[:n_rin], refs[n_rin:]
        outs, refs = refs[:n_out], refs[n_out:]
        r_outs, refs = refs[:n_rout], refs[n_rout:]
        scr, r_sems = refs[:n_scr], refs[n_scr:]
        ids = [pl.program_id(d) for d in range(len(grid))]
        first = functools.reduce(jnp.logical_and, [i == 0 for i in ids])
        last = functools.reduce(jnp.logical_and, [i == g - 1 for i, g in zip(ids, grid)])

        @pl.when(first)
        def _():
            rider.start(r_ins, r_outs, r_sems)

        kern(*ins, *outs, *scr)

        @pl.when(last)
        def _():
            rider.finish(r_ins, r_outs, r_sems)

    res = pl.pallas_call(
        body, name=name, grid=grid, in_specs=list(in_specs) + [ANY] * n_rin,
        out_specs=out_specs_l + [ANY] * n_rout, out_shape=out_shape_l + list(rider.out_shape),
        scratch_shapes=list(scratch) + list(rider.sems),
        compiler_params=_cp(("arbitrary",) * len(grid)))(*args, *rider.arrays)
    mine, theirs = list(res[:n_out]), list(res[n_out:])
    return (mine[0] if single else mine), theirs


def _ride_alone(rider, name):
    def body(*refs):
        n_rin, n_rout = len(rider.arrays), len(rider.out_shape)
        r_ins, r_outs, r_sems = refs[:n_rin], refs[n_rin:n_rin + n_rout], refs[n_rin + n_rout:]
        rider.start(r_ins, r_outs, r_sems)
        rider.finish(r_ins, r_outs, r_sems)

    return list(pl.pallas_call(
        body, name=name, in_specs=[ANY] * len(rider.arrays), out_specs=[ANY] * len(rider.out_shape),
        out_shape=list(rider.out_shape), scratch_shapes=list(rider.sems))(*rider.arrays))


def _ln(z):
    mu = jnp.mean(z, axis=-1, keepdims=True)
    zc = z - mu
    var = jnp.mean(zc * zc, axis=-1, keepdims=True)
    rstd = lax.rsqrt(var + EPS)
    return zc * rstd, rstd


def _ln_bwd(dxhat, xhat, rstd):
    m1 = jnp.mean(dxhat, axis=-1, keepdims=True)
    m2 = jnp.mean(dxhat * xhat, axis=-1, keepdims=True)
    return rstd * (dxhat - m1 - xhat * m2)


def _gelu(x):
    return 0.5 * x * (1.0 + lax.erf(x * INV_SQRT2))


def _gelu_grad(x):
    return 0.5 * (1.0 + lax.erf(x * INV_SQRT2)) + x * jnp.exp(-0.5 * x * x) * INV_SQRT_2PI


def _fold8(x):
    r, c = x.shape
    return jnp.sum(x.reshape(r // 8, 8, c), axis=0)


def _relu2(h):
    return jnp.square(jnp.maximum(h, 0.0))


def _weight_spec(block, index_map, resident):
    return pl.BlockSpec(block, index_map, pipeline_mode=pl.Buffered(1) if resident else None)


def _mm_nn(a, w, out_dtype, name, tm=512, tn=None, rider=None):
    n, k = a.shape
    m = w.shape[1]
    tm = min(tm, n)
    tn = m if tn is None else min(tn, m)

    def kern(a_ref, w_ref, o_ref):
        o_ref[...] = jnp.dot(a_ref[...].astype(BF16), w_ref[...], preferred_element_type=F32).astype(out_dtype)

    res, rode = _call(
        kern, name, (n // tm, m // tn),
        [pl.BlockSpec((tm, k), lambda i, j: (i, 0)), _weight_spec((k, tn), lambda i, j: (0, j), tn == m)],
        pl.BlockSpec((tm, tn), lambda i, j: (i, j)), _sds((n, m), out_dtype), (a, w), ("parallel", "parallel"),
        rider=rider)
    return res if rider is None else (res, rode)


def _mm_res_ln(a, w, res, g, b, name, relu2=False, tm=512, rider=None, then=None):
    n, k = a.shape
    tm = min(tm, n)

    m2 = None if then is None else then.shape[1]

    def kern(*refs):
        a_ref, w_ref, res_ref, g_ref, b_ref = refs[:5]
        x_ref, xhat_ref, rstd_ref = refs[-3:] if then is None else refs[-4:-1]
        av = a_ref[...]
        if relu2:
            av = _relu2(av.astype(F32))
        z = ALPHA * res_ref[...] + jnp.dot(av.astype(BF16), w_ref[...], preferred_element_type=F32)
        xhat, rstd = _ln(z)
        xhat_ref[...] = xhat
        rstd_ref[...] = rstd
        x = xhat * g_ref[...] + b_ref[...]
        x_ref[...] = x
        if then is not None:
            refs[-1][...] = jnp.dot(x.astype(BF16), refs[5][...], preferred_element_type=F32).astype(BF16)

    row = lambda i: (i, 0)
    fix = lambda i: (0, 0)
    in_specs = [pl.BlockSpec((tm, k), row), _weight_spec((k, D), fix, True), pl.BlockSpec((tm, D), row),
                pl.BlockSpec((1, D), fix), pl.BlockSpec((1, D), fix)]
    out_specs = [pl.BlockSpec((tm, D), row), pl.BlockSpec((tm, D), row), pl.BlockSpec((tm, 1), row)]
    out_shape = [_sds((n, D), F32), _sds((n, D), F32), _sds((n, 1), F32)]
    args = (a, w, res, g, b)
    if then is not None:
        in_specs.append(_weight_spec((D, m2), fix, True))
        out_specs.append(pl.BlockSpec((tm, m2), row))
        out_shape.append(_sds((n, m2), BF16))
        args += (then,)
    out, rode = _call(kern, name, (n // tm,), in_specs, out_specs, out_shape, args, ("parallel",), rider=rider)
    return out if rider is None else (out, rode)


def _mm_nt(a, w, out_dtype, name, relu2_grad_of=None, tm=512, tn=1024, rider=None):
    n, k = a.shape
    m = w.shape[0]
    tm = min(tm, n)
    tn = min(tn, m)
    with_h = relu2_grad_of is not None

    def kern(*refs):
        a_ref, w_ref = refs[0], refs[1]
        o_ref = refs[-1]
        r = lax.dot_general(a_ref[...].astype(BF16), w_ref[...], (((1,), (1,)), ((), ())), preferred_element_type=F32)
        if with_h:
            r = r * (2.0 * jnp.maximum(refs[2][...].astype(F32), 0.0))
        o_ref[...] = r.astype(out_dtype)

    in_specs = [pl.BlockSpec((tm, k), lambda i, j: (i, 0)), _weight_spec((tn, k), lambda i, j: (j, 0), tn == m)]
    args = [a, w]
    if with_h:
        in_specs.append(pl.BlockSpec((tm, tn), lambda i, j: (i, j)))
        args.append(relu2_grad_of)
    res, rode = _call(kern, name, (n // tm, m // tn), in_specs, pl.BlockSpec((tm, tn), lambda i, j: (i, j)),
                      _sds((n, m), out_dtype), args, ("parallel", "parallel"), rider=rider)
    return res if rider is None else (res, rode)


def _mm_tn_by_owner(a, b, name, tk=512):
    n, k1 = a.shape
    m = b.shape[1]
    s = m // N_DEV
    tk = min(tk, n)

    def kern(a_ref, b_ref, o_ref):
        @pl.when(pl.program_id(0) == 0)
        def _():
            o_ref[...] = jnp.zeros_like(o_ref)

        av = a_ref[...].astype(BF16)
        for j in range(N_DEV):
            o_ref[j // 2, j % 2] += lax.dot_general(av, b_ref[:, j * s:(j + 1) * s].astype(BF16),
                                                    (((0,), (0,)), ((), ())), preferred_element_type=F32)

    return pl.pallas_call(
        kern, name=name, grid=(n // tk,),
        in_specs=[pl.BlockSpec((tk, k1), lambda k: (k, 0)), pl.BlockSpec((tk, m), lambda k: (k, 0))],
        out_specs=pl.BlockSpec((4, 2, k1, s), lambda k: (0, 0, 0, 0), pipeline_mode=pl.Buffered(1)),
        out_shape=_sds((4, 2, k1, s), F32), compiler_params=_cp(("arbitrary",)))(a, b)


def _mm_tn(a, b, name, relu2=False, t1=1024, tn=2048, tk=512):
    n, k1 = a.shape
    m = b.shape[1]
    t1 = min(t1, k1)
    tn = m if m <= 2432 and m % tn else min(tn, m)
    tk = min(tk, n)

    def kern(a_ref, b_ref, o_ref):
        @pl.when(pl.program_id(2) == 0)
        def _():
            o_ref[...] = jnp.zeros_like(o_ref)

        av = a_ref[...]
        if relu2:
            av = _relu2(av.astype(F32))
        o_ref[...] += lax.dot_general(av.astype(BF16), b_ref[...].astype(BF16), (((0,), (0,)), ((), ())),
                                      preferred_element_type=F32)

    return pl.pallas_call(
        kern, name=name, grid=(k1 // t1, m // tn, n // tk),
        in_specs=[pl.BlockSpec((tk, t1), lambda i, j, k: (k, i)), pl.BlockSpec((tk, tn), lambda i, j, k: (k, j))],
        out_specs=pl.BlockSpec((t1, tn), lambda i, j, k: (i, j)),
        out_shape=_sds((k1, m), F32),
        compiler_params=_cp(("parallel", "parallel", "arbitrary")))(a, b)


def _bwd_in(dz_next, da, w, name, ln=None, tm=512, rider=None, then=None):
    n, k2 = da.shape
    tm = min(tm, n)
    row = lambda i: (i, 0)
    fix = lambda i: (0, 0)

    def dx_of(dzn_ref, da_ref, w_ref):
        return ALPHA * dzn_ref[...] + lax.dot_general(da_ref[...], w_ref[...], (((1,), (1,)), ((), ())),
                                                      preferred_element_type=F32)

    base_specs = [pl.BlockSpec((tm, D), row), pl.BlockSpec((tm, k2), row), _weight_spec((D, k2), fix, True)]
    if ln is None:
        def kern(dzn_ref, da_ref, w_ref, dx_ref):
            dx_ref[...] = dx_of(dzn_ref, da_ref, w_ref)

        out, rode = _call(kern, name, (n // tm,), base_specs, pl.BlockSpec((tm, D), row), _sds((n, D), F32),
                          (dz_next, da, w), ("parallel",), rider=rider)
        return out if rider is None else (out, rode)

    xhat, rstd, g = ln

    def kern(*refs):
        dzn_ref, da_ref, w_ref, xhat_ref, rstd_ref, g_ref = refs[:6]
        dz_ref, dg_ref, db_ref = refs[-3:] if then is None else refs[-4:-1]

        @pl.when(pl.program_id(0) == 0)
        def _():
            dg_ref[...] = jnp.zeros_like(dg_ref)
            db_ref[...] = jnp.zeros_like(db_ref)

        dx = dx_of(dzn_ref, da_ref, w_ref)
        xh = xhat_ref[...]
        dg_ref[...] += _fold8(dx * xh)
        db_ref[...] += _fold8(dx)
        dz = _ln_bwd(dx * g_ref[...], xh, rstd_ref[...])
        dz_ref[...] = dz
        if then is not None:
            refs[-1][...] = lax.dot_general(dz.astype(BF16), refs[6][...], (((1,), (1,)), ((), ())),
                                            preferred_element_type=F32).astype(BF16)

    in_specs = base_specs + [pl.BlockSpec((tm, D), row), pl.BlockSpec((tm, 1), row), pl.BlockSpec((1, D), fix)]
    out_specs = [pl.BlockSpec((tm, D), row), pl.BlockSpec((8, D), fix), pl.BlockSpec((8, D), fix)]
    out_shape = [_sds((n, D), F32), _sds((8, D), F32), _sds((8, D), F32)]
    args = (dz_next, da, w, xhat, rstd, g)
    if then is not None:
        m3 = then.shape[0]
        in_specs.append(_weight_spec((m3, D), fix, True))
        out_specs.append(pl.BlockSpec((tm, m3), row))
        out_shape.append(_sds((n, m3), BF16))
        args += (then,)
    return pl.pallas_call(kern, name=name, grid=(n // tm,), in_specs=in_specs, out_specs=out_specs,
                          out_shape=out_shape, compiler_params=_cp(("arbitrary",)))(*args)


def _loss_lnbwd(x, target, xhat, rstd, g, name, tm=512):
    n = x.shape[0]
    tm = min(tm, n)
    row = lambda i: (i, 0)
    fix = lambda i: (0, 0)

    def kern(x_ref, t_ref, xhat_ref, rstd_ref, g_ref, sq_ref, dz_ref, dg_ref, db_ref):
        @pl.when(pl.program_id(0) == 0)
        def _():
            sq_ref[...] = jnp.zeros_like(sq_ref)
            dg_ref[...] = jnp.zeros_like(dg_ref)
            db_ref[...] = jnp.zeros_like(db_ref)

        err = x_ref[...] - t_ref[...]
        sq_ref[...] += _fold8(err * err)
        dx = err * (1.0 / D)
        xh = xhat_ref[...]
        dg_ref[...] += _fold8(dx * xh)
        db_ref[...] += _fold8(dx)
        dz_ref[...] = _ln_bwd(dx * g_ref[...], xh, rstd_ref[...])

    return pl.pallas_call(
        kern, name=name, grid=(n // tm,),
        in_specs=[pl.BlockSpec((tm, D), row), pl.BlockSpec((tm, D), row), pl.BlockSpec((tm, D), row),
                  pl.BlockSpec((tm, 1), row), pl.BlockSpec((1, D), fix)],
        out_specs=[pl.BlockSpec((8, D), fix), pl.BlockSpec((tm, D), row), pl.BlockSpec((8, D), fix),
                   pl.BlockSpec((8, D), fix)],
        out_shape=[_sds((8, D), F32), _sds((n, D), F32), _sds((8, D), F32), _sds((8, D), F32)],
        compiler_params=_cp(("arbitrary",)))(x, target, xhat, rstd, g)


def _softmax_rows(s):
    s = s - jnp.max(s, axis=-1, keepdims=True)
    e = jnp.exp(s)
    return e / jnp.sum(e, axis=-1, keepdims=True)


def _attn_fwd(q, kv, bd, name, tm=512):
    n = q.shape[0]
    s_len = n // bd
    m_len = kv.shape[0] // bd
    tm = min(tm, s_len)
    nt = s_len // tm
    scale = X_HD ** -0.5

    def kern(q_ref, k_ref, v_ref, o_ref):
        for h in range(X_HEADS):
            cs = slice(h * X_HD, (h + 1) * X_HD)
            s = lax.dot_general(q_ref[:, cs], k_ref[:, cs], (((1,), (1,)), ((), ())), preferred_element_type=F32)
            p = _softmax_rows(s * scale)
            o_ref[:, cs] = jnp.dot(p.astype(BF16), v_ref[:, cs], preferred_element_type=F32).astype(BF16)

    return pl.pallas_call(
        kern, name=name, grid=(bd, nt),
        in_specs=[pl.BlockSpec((tm, D), lambda b, i: (b * nt + i, 0)),
                  pl.BlockSpec((m_len, D), lambda b, i: (b, 0)), pl.BlockSpec((m_len, D), lambda b, i: (b, 1))],
        out_specs=pl.BlockSpec((tm, D), lambda b, i: (b * nt + i, 0)),
        out_shape=_sds((n, D), BF16),
        compiler_params=_cp(("parallel", "parallel")))(q, kv, kv)


def _attn_bwd(q, kv, do, bd, name, tm=512, rider=None):
    n = q.shape[0]
    s_len = n // bd
    m_len = kv.shape[0] // bd
    tm = min(tm, s_len)
    nt = s_len // tm
    scale = X_HD ** -0.5

    def kern(q_ref, k_ref, v_ref, do_ref, dq_ref, dkv_ref):
        @pl.when(pl.program_id(1) == 0)
        def _():
            dkv_ref[...] = jnp.zeros_like(dkv_ref)

        for h in range(X_HEADS):
            cs = slice(h * X_HD, (h + 1) * X_HD)
            vs = slice(D + h * X_HD, D + (h + 1) * X_HD)
            qh, kh, vh, doh = q_ref[:, cs], k_ref[:, cs], v_ref[:, cs], do_ref[:, cs]
            s = lax.dot_general(qh, kh, (((1,), (1,)), ((), ())), preferred_element_type=F32)
            p = _softmax_rows(s * scale)
            pb = p.astype(BF16)
            dp = lax.dot_general(doh, vh, (((1,), (1,)), ((), ())), preferred_element_type=F32)
            dkv_ref[:, vs] += lax.dot_general(pb, doh, (((0,), (0,)), ((), ())), preferred_element_type=F32)
            ds = (p * (dp - jnp.sum(dp * p, axis=-1, keepdims=True)) * scale).astype(BF16)
            dq_ref[:, cs] = jnp.dot(ds, kh, preferred_element_type=F32).astype(BF16)
            dkv_ref[:, cs] += lax.dot_general(ds, qh, (((0,), (0,)), ((), ())), preferred_element_type=F32)

    out, rode = _call(
        kern, name, (bd, nt),
        [pl.BlockSpec((tm, D), lambda b, i: (b * nt + i, 0)),
         pl.BlockSpec((m_len, D), lambda b, i: (b, 0)), pl.BlockSpec((m_len, D), lambda b, i: (b, 1)),
         pl.BlockSpec((tm, D), lambda b, i: (b * nt + i, 0))],
        [pl.BlockSpec((tm, D), lambda b, i: (b * nt + i, 0)), pl.BlockSpec((m_len, 2 * D), lambda b, i: (b, 0))],
        [_sds((n, D), BF16), _sds((bd * m_len, 2 * D), F32)], (q, kv, kv, do), ("parallel", "arbitrary"), rider=rider)
    return out if rider is None else (out, rode)


C_AV, C_AG, C_BU, C_BV, C_CB, C_CC, C_CX = 0, 384, 768, 1024, 1280, 1664, 2048


class _Windows:
    def __init__(self, win_ref, r0, halo, cs):
        self.win = win_ref[pl.ds(r0, CHUNK + halo), cs]
        self.n = CHUNK + halo - 8
        self.shifted = {0: self.win}

    def rows(self, o):
        b, a = o % 8, o // 8
        if b not in self.shifted:
            self.shifted[b] = self.win[b:b + self.n, :]
        return self.shifted[b][8 * a:8 * a + CHUNK, :]


def _conv_taps(win_ref, r0, halo, w_ref, n_taps, offset_of_tap):
    parts = []
    for cb in range(3):
        cs = slice(cb * 128, (cb + 1) * 128)
        win = _Windows(win_ref, r0, halo, cs)
        acc = jnp.zeros((CHUNK, 128), F32)
        for k in range(n_taps):
            acc = acc + win.rows(offset_of_tap(k)) * w_ref[k:k + 1, cs]
        parts.append(acc)
    return jnp.concatenate(parts, axis=1)


def _causal_conv(win_ref, r0, halo, w_ref, n_taps):
    return _conv_taps(win_ref, r0, halo, w_ref, n_taps, lambda k: halo - (n_taps - 1) + k)


def _anticausal_conv(win_ref, r0, halo, w_ref, n_taps):
    return _conv_taps(win_ref, r0, halo, w_ref, n_taps, lambda k: n_taps - 1 - k)


def _head_of_lane():
    return lax.broadcasted_iota(jnp.int32, (1, D_B), 1) // HEAD


def _spatial_mix(wm_ref, vb, head):
    mixed = jnp.zeros((CHUNK, D_B), F32)
    for h in range(B_HEADS):
        mh = jnp.dot(wm_ref[h], vb, preferred_element_type=F32)
        mixed = jnp.where(head == h, mh, mixed)
    return mixed


def _mixer_fwd(proj, p, bd, name, rider=None):
    n = proj.shape[0]
    s_len = n // bd
    n_chunks = s_len // CHUNK

    def kern(proj_ref, caw_ref, cab_ref, lag_ref, lab_ref, lvg_ref, lvb_ref, wm_ref, bsx_ref, ccw_ref, cat_ref,
             ca_ref, gs_ref, ccs_ref):
        gs_ref[0:HALO_A, :] = jnp.zeros((HALO_A, D_A), F32)
        ccs_ref[0:HALO_C, :] = jnp.zeros((HALO_C, D_C), F32)
        head = _head_of_lane()

        def chunk(i, carry):
            r0 = pl.multiple_of(i * CHUNK, CHUNK)
            rows = pl.ds(r0, CHUNK)
            ld = lambda c0, w: proj_ref[rows, c0:c0 + w].astype(F32)
            gs_ref[pl.ds(r0 + HALO_A, CHUNK), :] = ld(C_AV, D_A) * jax.nn.sigmoid(ld(C_AG, D_A))
            ca = _causal_conv(gs_ref, r0, HALO_A, caw_ref, KA) + cab_ref[...]
            ca_ref[rows, :] = ca
            lna = _ln(ca)[0] * lag_ref[...] + lab_ref[...]
            cat_ref[rows, 0:D_A] = (lna * jax.nn.sigmoid(lna)).astype(BF16)
            u = _gelu(ld(C_BU, D_B))
            v = _ln(_gelu(ld(C_BV, D_B)))[0] * lvg_ref[...] + lvb_ref[...]
            mixed = _spatial_mix(wm_ref, v.astype(BF16), head) + bsx_ref[...]
            cat_ref[rows, D_A:D_A + D_B] = (u * mixed).astype(BF16)
            ccs_ref[pl.ds(r0 + HALO_C, CHUNK), :] = ld(C_CC, D_C) * ld(C_CX, D_C)
            conv = _causal_conv(ccs_ref, r0, HALO_C, ccw_ref, KC)
            cat_ref[rows, D_A + D_B:D] = (ld(C_CB, D_C) * conv).astype(BF16)
            return carry

        lax.fori_loop(0, n_chunks, chunk, 0)

    fix2 = lambda b: (0, 0)
    args = [proj, p["caw"], p["cab"], p["lag"], p["lab"], p["lvg"], p["lvb"], p["wm"], p["bsx"], p["ccw"]]
    in_specs = [pl.BlockSpec((s_len, IN_W), lambda b: (b, 0))]
    for a in args[1:]:
        in_specs.append(pl.BlockSpec(a.shape, (lambda b: (0, 0, 0)) if a.ndim == 3 else fix2))
    res, rode = _call(
        kern, name, (bd,), in_specs,
        [pl.BlockSpec((s_len, D), lambda b: (b, 0)), pl.BlockSpec((s_len, D_A), lambda b: (b, 0))],
        [_sds((n, D), BF16), _sds((n, D_A), F32)], args, ("parallel",),
        scratch=[pltpu.VMEM((s_len + HALO_A, D_A), F32), pltpu.VMEM((s_len + HALO_C, D_C), F32)], rider=rider)
    return res if rider is None else (res, rode)


def _mixer_bwd(proj, dcat, ca, p, bd, name, rider=None):
    n = proj.shape[0]
    s_len = n // bd
    n_chunks = s_len // CHUNK

    def kern(proj_ref, dcat_ref, ca_ref, caw_ref, cab_ref, lag_ref, lab_ref, lvg_ref, lvb_ref, wm_ref, bsx_ref, ccw_ref,
             dproj_ref, dcaw_ref, dcab_ref, dlag_ref, dlab_ref, dlvg_ref, dlvb_ref, dws_ref, dbs_ref, dccw_ref,
             gs_ref, dcas_ref, ccs_ref, dcs_ref, a_caw, a_cab, a_lag, a_lab, a_lvg, a_lvb, a_ccw):
        gs_ref[0:HALO_A, :] = jnp.zeros((HALO_A, D_A), F32)
        ccs_ref[0:HALO_C, :] = jnp.zeros((HALO_C, D_C), F32)
        dcas_ref[s_len:s_len + HALO_A, :] = jnp.zeros((HALO_A, D_A), F32)
        dcs_ref[s_len:s_len + HALO_C, :] = jnp.zeros((HALO_C, D_C), F32)
        for acc in (a_caw, a_cab, a_lag, a_lab, a_lvg, a_lvb, a_ccw, dws_ref, dbs_ref):
            acc[...] = jnp.zeros_like(acc)
        head = _head_of_lane()
        lane128 = lax.broadcasted_iota(jnp.int32, (1, CHUNK), 1)

        def pass1(i, carry):
            r0 = pl.multiple_of(i * CHUNK, CHUNK)
            rows = pl.ds(r0, CHUNK)
            ld = lambda c0, w: proj_ref[rows, c0:c0 + w].astype(F32)
            dld = lambda c0, w: dcat_ref[rows, c0:c0 + w].astype(F32)
            gs_ref[pl.ds(r0 + HALO_A, CHUNK), :] = ld(C_AV, D_A) * jax.nn.sigmoid(ld(C_AG, D_A))
            xh, rstd = _ln(ca_ref[rows, :])
            lna = xh * lag_ref[...] + lab_ref[...]
            sg = jax.nn.sigmoid(lna)
            dlna = dld(0, D_A) * (sg * (1.0 + lna * (1.0 - sg)))
            a_lag[...] += _fold8(dlna * xh)
            a_lab[...] += _fold8(dlna)
            dca = _ln_bwd(dlna * lag_ref[...], xh, rstd)
            dcas_ref[rows, :] = dca
            a_cab[...] += _fold8(dca)
            for cb in range(3):
                cs = slice(cb * 128, (cb + 1) * 128)
                win = _Windows(gs_ref, r0, HALO_A, cs)
                dcab = dca[:, cs]
                for k in range(KA):
                    a_caw[k * 8:(k + 1) * 8, cs] += _fold8(dcab * win.rows(HALO_A - (KA - 1) + k))
            pu, pv = ld(C_BU, D_B), ld(C_BV, D_B)
            u = _gelu(pu)
            vxh, vrstd = _ln(_gelu(pv))
            v = vxh * lvg_ref[...] + lvb_ref[...]
            vb = v.astype(BF16)
            mixed = _spatial_mix(wm_ref, vb, head) + bsx_ref[...]
            dbo = dld(D_A, D_B)
            dproj_ref[rows, C_BU:C_BU + D_B] = (dbo * mixed * _gelu_grad(pu)).astype(BF16)
            dmixed = dbo * u
            dv = jnp.zeros((CHUNK, D_B), F32)
            bsum = jnp.zeros((CHUNK, CHUNK), F32)
            for h in range(B_HEADS):
                dmh = jnp.where(head == h, dmixed, 0.0)
                dmb = dmh.astype(BF16)
                dvh = lax.dot_general(wm_ref[h], dmb, (((0,), (0,)), ((), ())), preferred_element_type=F32)
                dv = jnp.where(head == h, dvh, dv)
                dws_ref[h] += lax.dot_general(dmb, vb, (((1,), (1,)), ((), ())), preferred_element_type=F32)
                bsum = bsum + jnp.where(lane128 == h, jnp.sum(dmh, axis=-1, keepdims=True), 0.0)
            dbs_ref[...] += bsum
            a_lvg[...] += _fold8(dv * vxh)
            a_lvb[...] += _fold8(dv)
            dgv = _ln_bwd(dv * lvg_ref[...], vxh, vrstd)
            dproj_ref[rows, C_BV:C_BV + D_B] = (dgv * _gelu_grad(pv)).astype(BF16)
            ccs_ref[pl.ds(r0 + HALO_C, CHUNK), :] = ld(C_CC, D_C) * ld(C_CX, D_C)
            conv = _causal_conv(ccs_ref, r0, HALO_C, ccw_ref, KC)
            dco = dld(D_A + D_B, D_C)
            dproj_ref[rows, C_CB:C_CB + D_C] = (dco * conv).astype(BF16)
            dconv = dco * ld(C_CB, D_C)
            dcs_ref[rows, :] = dconv
            for cb in range(3):
                cs = slice(cb * 128, (cb + 1) * 128)
                win = _Windows(ccs_ref, r0, HALO_C, cs)
                for k in range(KC):
                    a_ccw[k * 8:(k + 1) * 8, cs] += _fold8(dconv[:, cs] * win.rows(HALO_C - (KC - 1) + k))
            return carry

        lax.fori_loop(0, n_chunks, pass1, 0)

        def pass2(i, carry):
            r0 = pl.multiple_of(i * CHUNK, CHUNK)
            rows = pl.ds(r0, CHUNK)
            ld = lambda c0, w: proj_ref[rows, c0:c0 + w].astype(F32)
            dg = _anticausal_conv(dcas_ref, r0, HALO_A, caw_ref, KA)
            pa = ld(C_AV, D_A)
            sg = jax.nn.sigmoid(ld(C_AG, D_A))
            dproj_ref[rows, C_AV:C_AV + D_A] = (dg * sg).astype(BF16)
            dproj_ref[rows, C_AG:C_AG + D_A] = (dg * pa * sg * (1.0 - sg)).astype(BF16)
            dcc = _anticausal_conv(dcs_ref, r0, HALO_C, ccw_ref, KC)
            dproj_ref[rows, C_CC:C_CC + D_C] = (dcc * ld(C_CX, D_C)).astype(BF16)
            dproj_ref[rows, C_CX:C_CX + D_C] = (dcc * ld(C_CC, D_C)).astype(BF16)
            return carry

        lax.fori_loop(0, n_chunks, pass2, 0)

        for k in range(KA):
            dcaw_ref[k:k + 1, :] = jnp.sum(a_caw[k * 8:(k + 1) * 8, :], axis=0, keepdims=True)
        dcaw_ref[KA:KA + 1, :] = jnp.zeros((1, D_A), F32)
        for k in range(8):
            if k < KC:
                dccw_ref[k:k + 1, :] = jnp.sum(a_ccw[k * 8:(k + 1) * 8, :], axis=0, keepdims=True)
            else:
                dccw_ref[k:k + 1, :] = jnp.zeros((1, D_C), F32)
        dcab_ref[...] = a_cab[...]
        dlag_ref[...] = a_lag[...]
        dlab_ref[...] = a_lab[...]
        dlvg_ref[...] = a_lvg[...]
        dlvb_ref[...] = a_lvb[...]

    fix2 = lambda b: (0, 0)
    args = [proj, dcat, ca, p["caw"], p["cab"], p["lag"], p["lab"], p["lvg"], p["lvb"], p["wm"], p["bsx"], p["ccw"]]
    once = pl.Buffered(1)
    in_specs = [pl.BlockSpec((s_len, IN_W), lambda b: (b, 0), pipeline_mode=once),
                pl.BlockSpec((s_len, D), lambda b: (b, 0), pipeline_mode=once),
                pl.BlockSpec((s_len, D_A), lambda b: (b, 0), pipeline_mode=once)]
    for a in args[3:]:
        in_specs.append(pl.BlockSpec(a.shape, (lambda b: (0, 0, 0)) if a.ndim == 3 else fix2))

    def per_seq(*shape):
        nd = len(shape)
        return (pl.BlockSpec((None,) + shape, lambda b: (b,) + (0,) * nd), _sds((bd,) + shape, F32))

    outs = [(pl.BlockSpec((s_len, IN_W), lambda b: (b, 0), pipeline_mode=once), _sds((n, IN_W), BF16)),
            per_seq(32, D_A), per_seq(8, D_A), per_seq(8, D_A), per_seq(8, D_A), per_seq(8, D_B), per_seq(8, D_B),
            per_seq(B_HEADS, CHUNK, CHUNK), per_seq(CHUNK, CHUNK), per_seq(8, D_C)]
    res, rode = _call(
        kern, name, (bd,), in_specs, [o[0] for o in outs], [o[1] for o in outs], args, ("parallel",),
        scratch=[pltpu.VMEM((s_len + HALO_A, D_A), F32), pltpu.VMEM((s_len + HALO_A, D_A), F32),
                 pltpu.VMEM((s_len + HALO_C, D_C), F32), pltpu.VMEM((s_len + HALO_C, D_C), F32),
                 pltpu.VMEM((KA * 8, D_A), F32), pltpu.VMEM((8, D_A), F32), pltpu.VMEM((8, D_A), F32),
                 pltpu.VMEM((8, D_A), F32), pltpu.VMEM((8, D_B), F32), pltpu.VMEM((8, D_B), F32),
                 pltpu.VMEM((KC * 8, D_C), F32)],
        rider=rider)
    return res if rider is None else (res, rode)


def _adamw(w, m, v, g_parts, name):
    shape = w.shape
    c = shape[-1]
    r = math.prod(shape[:-1])
    as2d = lambda a: a.reshape(r, c)
    tr = 512 if r % 512 == 0 else r
    slots = [(a.reshape(a.shape[0], r, c), p) for a in g_parts for p in range(a.shape[0])]
    n_g = len(slots)

    def kern(*refs):
        w_ref, m_ref, v_ref = refs[:3]
        g_refs = refs[3:3 + n_g]
        go_ref, d_ref, mo_ref, vo_ref = refs[3 + n_g:]
        g = g_refs[0][...].astype(F32)
        for gr in g_refs[1:]:
            g = g + gr[...].astype(F32)
        mn = B1 * m_ref[...] + (1.0 - B1) * g
        vn = B2 * v_ref[...] + (1.0 - B2) * jnp.square(g)
        m_hat = mn / (1.0 - B1 ** STEP)
        v_hat = vn / (1.0 - B2 ** STEP)
        go_ref[...] = g
        mo_ref[...] = mn
        vo_ref[...] = vn
        d_ref[...] = -LR * (m_hat / (jnp.sqrt(v_hat) + ADAM_EPS) + WD * w_ref[...])

    spec = pl.BlockSpec((tr, c), lambda i: (i, 0))
    g_specs = [pl.BlockSpec((None, tr, c), functools.partial(lambda p, i: (p, i, 0), p)) for _, p in slots]
    outs = pl.pallas_call(
        kern, name=name, grid=(r // tr,), in_specs=[spec] * 3 + g_specs, out_specs=[spec] * 4,
        out_shape=[_sds((r, c), F32)] * 4, compiler_params=_cp(("parallel",)))(
            as2d(w), as2d(m), as2d(v), *[a for a, _ in slots])
    return [o.reshape(shape) for o in outs]


def _place():
    x, y, c = lax.axis_index("x"), lax.axis_index("y"), lax.axis_index("c")
    other_chips = [(1 - x, y), (x, 1 - y), (1 - x, 1 - y)]
    return x, y, c, other_chips


def _gather_rider(arrays):
    n_arr = len(arrays)

    def parts(ins, outs, sems):
        send_sems, recv_sems, local_sems = sems
        x, y, c, chips = _place()
        me, sibling = (x, y, c), (x, y, 1 - c)
        slot = lambda px, py, pc: 4 * px + 2 * py + pc

        def copy(a, k, block, to, from_input=False):
            dst = outs[a].at[slot(*block)]
            return pltpu.make_async_remote_copy(
                src_ref=ins[a] if from_input else dst, dst_ref=dst, send_sem=send_sems.at[k, a],
                recv_sem=recv_sems.at[k, a], device_id=to, device_id_type=MESH)

        mine = [pltpu.make_async_copy(ins[a], outs[a].at[slot(*me)], local_sems.at[a]) for a in range(n_arr)]
        first = []
        for a in range(n_arr):
            first.append(copy(a, 0, me, sibling, True))
            first += [copy(a, 1 + j, me, (*chip, c), True) for j, chip in enumerate(chips)]
        return copy, mine, first, me, sibling, chips, c

    def start(ins, outs, sems):
        _, mine, first, *_ = parts(ins, outs, sems)
        for cp in mine + first:
            cp.start()

    def finish(ins, outs, sems):
        copy, mine, first, me, sibling, chips, c = parts(ins, outs, sems)
        passed = []
        for j, chip in enumerate(chips):
            for a in range(n_arr):
                copy(a, 1 + j, (*chip, c), me).wait_recv()
                passed.append(copy(a, 4 + j, (*chip, c), sibling))
                passed[-1].start()
        for a in range(n_arr):
            copy(a, 0, sibling, me).wait_recv()
            for j, chip in enumerate(chips):
                copy(a, 4 + j, (*chip, 1 - c), me).wait_recv()
        for cp in first + passed:
            cp.wait_send()
        for cp in mine:
            cp.wait()

    return _Rider(list(arrays), [_sds((N_DEV,) + a.shape, a.dtype) for a in arrays],
                  [pltpu.SemaphoreType.DMA((7, n_arr)), pltpu.SemaphoreType.DMA((7, n_arr)),
                   pltpu.SemaphoreType.DMA((n_arr,))], start, finish)


def _exchange_rider(arrays):
    n_arr = len(arrays)

    def copies(ins, outs, sems):
        send_sems, recv_sems = sems
        x, y, c, _ = _place()
        return [pltpu.make_async_remote_copy(
            src_ref=ins[a].at[:, 1 - c], dst_ref=outs[a], send_sem=send_sems.at[a], recv_sem=recv_sems.at[a],
            device_id=(x, y, 1 - c), device_id_type=MESH) for a in range(n_arr)]

    def start(ins, outs, sems):
        for cp in copies(ins, outs, sems):
            cp.start()

    def finish(ins, outs, sems):
        for cp in copies(ins, outs, sems):
            cp.wait()

    return _Rider(list(arrays), [_sds(a.shape[:1] + a.shape[2:], a.dtype) for a in arrays],
                  [pltpu.SemaphoreType.DMA((n_arr,)), pltpu.SemaphoreType.DMA((n_arr,))], start, finish)


def _pair_add(mine, theirs, core, name):
    _, _, r, c = mine.shape
    tr = 512 if r % 512 == 0 else r

    def kern(core_ref, a_ref, b_ref, o_ref, ob_ref):
        s = a_ref[...] + b_ref[...]
        o_ref[...] = s
        ob_ref[...] = s.astype(BF16)

    row = lambda t, i, core_ref: (t, i, 0)
    return pl.pallas_call(
        kern, name=name,
        grid_spec=pltpu.PrefetchScalarGridSpec(
            num_scalar_prefetch=1, grid=(4, r // tr),
            in_specs=[pl.BlockSpec((None, None, tr, c), lambda t, i, core_ref: (t, core_ref[0], i, 0)),
                      pl.BlockSpec((None, tr, c), row)],
            out_specs=[pl.BlockSpec((None, tr, c), row), pl.BlockSpec((None, tr, c), row)]),
        out_shape=[_sds((4, r, c), F32), _sds((4, r, c), BF16)],
        compiler_params=_cp(("parallel", "parallel")))(core, mine, theirs)


def _scatter_rider(sums, sums_bf16):
    n_arr = len(sums)

    def copies(ins, outs, sems):
        send_sems, recv_sems, local_sems = sems
        x, y, c, chips = _place()
        own = [pltpu.make_async_copy(ins[a].at[2 * x + y], outs[a], local_sems.at[a]) for a in range(n_arr)]
        remote = []
        for a in range(n_arr):
            for j, (px, py) in enumerate(chips):
                remote.append(pltpu.make_async_remote_copy(
                    src_ref=ins[n_arr + a].at[2 * px + py], dst_ref=outs[n_arr + a].at[j], send_sem=send_sems.at[j, a],
                    recv_sem=recv_sems.at[j, a], device_id=(px, py, c), device_id_type=MESH))
        return own + remote

    def start(ins, outs, sems):
        for cp in copies(ins, outs, sems):
            cp.start()

    def finish(ins, outs, sems):
        for cp in copies(ins, outs, sems):
            cp.wait()

    return _Rider(list(sums) + list(sums_bf16),
                  [_sds(a.shape[1:], a.dtype) for a in sums] + [_sds((3,) + a.shape[1:], a.dtype) for a in sums_bf16],
                  [pltpu.SemaphoreType.DMA((3, n_arr)), pltpu.SemaphoreType.DMA((3, n_arr)),
                   pltpu.SemaphoreType.DMA((n_arr,))], start, finish)


def _mixer_params(conv_a_w, conv_a_b, ln_a_g, ln_a_b, ln_v_g, ln_v_b, w_s, b_s, conv_c_w):
    causal = jnp.tril(jnp.ones((CHUNK, CHUNK), dtype=bool))
    row = lambda a: a.reshape(1, -1)
    return dict(
        caw=jnp.pad(conv_a_w, ((0, 32 - KA), (0, 0))), cab=row(conv_a_b), lag=row(ln_a_g), lab=row(ln_a_b),
        lvg=row(ln_v_g), lvb=row(ln_v_b), wm=jnp.where(causal[None], w_s, 0.0).astype(BF16),
        bsx=jnp.repeat(b_s.T, HEAD, axis=1), ccw=jnp.pad(conv_c_w, ((0, 8 - KC), (0, 0))))


class _Schedule:
    def __init__(self, big=None):
        self.big = big

    def weights(self, l):
        return {k: v[l] for k, v in self.big.items()}

    def rider(self, stage, l):
        return None

    def rode(self, stage, l, results):
        pass

    def note_grads(self, l, grads):
        pass

    def finish(self):
        pass


def _local_step(x, mem, target, sched, small, bd):
    row = lambda a: a.reshape(1, -1)

    def ride(stage, l, fn, *args, **kw):
        rider = sched.rider(stage, l)
        res = fn(*args, rider=rider, **kw)
        if rider is not None:
            res, results = res
            sched.rode(stage, l, results)
        return res

    saved = []
    for l in range(DEPTH):
        big = sched.weights(l)
        mp = _mixer_params(big["conv_a_w"], *[small[k][l] for k in ("conv_a_b", "ln_a_g", "ln_a_b", "ln_v_g", "ln_v_b",
                                                                    "w_s", "b_s")], big["conv_c_w"])
        proj = ride("in_proj", l, _mm_nn, x, big["w_in"], BF16, f"in_proj_{l}")
        cat, ca = ride("mixer_fwd", l, _mixer_fwd, proj, mp, bd, f"mixer_fwd_{l}")
        x1, xh1, rs1, q = _mm_res_ln(cat, big["w_out"], x, row(small["ln1_g"][l]), row(small["ln1_b"][l]),
                                     f"out_proj_ln1_q_{l}", then=big["w_q"])
        kv = _mm_nn(mem, big["w_kv"], BF16, f"kv_proj_{l}")
        o = _attn_fwd(q, kv, bd, f"attn_fwd_{l}")
        x2, xh2, rs2, h = ride("ff1", l, _mm_res_ln, o, big["w_o"], x1, row(small["ln2_g"][l]), row(small["ln2_b"][l]),
                               f"o_proj_ln2_ff1_{l}", then=big["w_ff1"])
        x3, xh3, rs3 = ride("ff2_ln3", l, _mm_res_ln, h, big["w_ff2"], x2, row(small["ln3_g"][l]),
                            row(small["ln3_b"][l]), f"ff2_ln3_{l}", relu2=True)
        saved.append(dict(mp=mp, x=x, proj=proj, cat=cat, ca=ca, x1=x1, xh1=xh1, rs1=rs1, q=q, kv=kv, o=o, x2=x2, xh2=xh2,
                          rs2=rs2, h=h, xh3=xh3, rs3=rs3))
        x = x3

    grads = {k: [None] * DEPTH for k in WEIGHTS}
    s = saved[-1]
    sq, dz3, dg, db = _loss_lnbwd(x, target, s["xh3"], s["rs3"], row(small["ln3_g"][DEPTH - 1]), "loss_ln3_bwd")
    grad_x = None
    causal = jnp.tril(jnp.ones((CHUNK, CHUNK), dtype=bool))
    for l in reversed(range(DEPTH)):
        s = saved[l]
        big = sched.weights(l)
        grads["ln3_g"][l], grads["ln3_b"][l] = jnp.sum(dg, axis=0), jnp.sum(db, axis=0)
        dh = ride("ff2_bwd", l, _mm_nt, dz3, big["w_ff2"], BF16, f"ff2_bwd_{l}", relu2_grad_of=s["h"], tn=D_FF)
        grads["w_ff2"][l] = _mm_tn(s["h"], dz3, f"ff2_wgrad_{l}", relu2=True, t1=2048)
        grads["w_ff1"][l] = _mm_tn_by_owner(s["x2"], dh, f"ff1_wgrad_{l}")
        sched.note_grads(l, {k: grads[k][l] for k in ("w_ff1", "w_ff2")})
        dz2, dg, db, do = _bwd_in(dz3, dh, big["w_ff1"], f"ff1_bwd_ln2_o_bwd_{l}",
                                  ln=(s["xh2"], s["rs2"], row(small["ln2_g"][l])), then=big["w_o"])
        grads["ln2_g"][l], grads["ln2_b"][l] = jnp.sum(dg, axis=0), jnp.sum(db, axis=0)
        grads["w_o"][l] = _mm_tn(s["o"], dz2, f"o_proj_wgrad_{l}")
        dq, dkv = ride("attn_bwd", l, _attn_bwd, s["q"], s["kv"], do, bd, f"attn_bwd_{l}")
        grads["w_q"][l] = _mm_tn(s["x1"], dq, f"q_wgrad_{l}")
        grads["w_kv"][l] = _mm_tn_by_owner(mem, dkv, f"kv_wgrad_{l}")
        dz1, dg, db, dcat = _bwd_in(dz2, dq, big["w_q"], f"q_bwd_ln1_out_bwd_{l}",
                                    ln=(s["xh1"], s["rs1"], row(small["ln1_g"][l])), then=big["w_out"])
        grads["ln1_g"][l], grads["ln1_b"][l] = jnp.sum(dg, axis=0), jnp.sum(db, axis=0)
        grads["w_out"][l] = _mm_tn(s["cat"], dz1, f"out_proj_wgrad_{l}")
        (dproj, dcaw, dcab, dlag, dlab, dlvg, dlvb, dws, dbs, dccw) = ride(
            "mixer_bwd", l, _mixer_bwd, s["proj"], dcat, s["ca"], s["mp"], bd, f"mixer_bwd_{l}")
        grads["conv_a_w"][l] = jnp.sum(dcaw, axis=0)[:KA]
        grads["conv_a_b"][l] = jnp.sum(dcab, axis=(0, 1))
        grads["ln_a_g"][l] = jnp.sum(dlag, axis=(0, 1))
        grads["ln_a_b"][l] = jnp.sum(dlab, axis=(0, 1))
        grads["ln_v_g"][l] = jnp.sum(dlvg, axis=(0, 1))
        grads["ln_v_b"][l] = jnp.sum(dlvb, axis=(0, 1))
        grads["w_s"][l] = jnp.where(causal[None], jnp.sum(dws, axis=0), 0.0)
        grads["b_s"][l] = jnp.sum(dbs, axis=0)[:, :B_HEADS].T
        grads["conv_c_w"][l] = jnp.sum(dccw, axis=0)[:KC]
        grads["w_in"][l] = _mm_tn(s["x"], dproj, f"in_proj_wgrad_{l}")
        sched.note_grads(l, {k: v[l] for k, v in grads.items() if k not in ("w_ff1", "w_ff2")})
        if l > 0:
            p = saved[l - 1]
            dz3, dg, db = _bwd_in(dz1, dproj, big["w_in"], f"in_proj_bwd_ln3_{l}",
                                  ln=(p["xh3"], p["rs3"], row(small["ln3_g"][l - 1])))
        else:
            grad_x = ride("in_proj_bwd", 0, _bwd_in, dz1, dproj, big["w_in"], "in_proj_bwd_0")
    sched.finish()
    return sq, grad_x, grads


WEIGHTS = ("w_in", "conv_a_w", "conv_a_b", "ln_a_g", "ln_a_b", "ln_v_g", "ln_v_b", "w_s", "b_s", "conv_c_w", "w_out",
           "ln1_g", "ln1_b", "w_q", "w_kv", "w_o", "ln2_g", "ln2_b", "w_ff1", "w_ff2", "ln3_g", "ln3_b")
COL_SHARDED = ("w_in", "w_kv", "w_ff1")
ROW_SHARDED = ("w_out", "w_q", "w_o", "w_ff2")
BIG = COL_SHARDED + ROW_SHARDED
REPLICATED = tuple(k for k in WEIGHTS if k not in BIG and k not in ("conv_a_w", "conv_c_w"))
PACK_LANES = 128


CONV_ROWS = 32 + 8
GATHER_LAYER0 = {"first": ("w_in", "conv"), "in_proj": ("w_out", "w_q", "w_kv", "w_o"), "mixer_fwd": ("w_ff1", "w_ff2")}
GATHER_NEXT = {"ff1": ("w_ff2", "w_in", "w_out", "w_q", "w_o", "conv"), "ff2_ln3": ("w_ff1", "w_kv")}
GRADS_EARLY = ("w_ff1", "w_ff2")
GRADS_LATE = ("w_in", "w_kv", "w_out", "w_q", "w_o", "conv")


def _gathered_to_full(g, col_sharded):
    _, a, b = g.shape
    if col_sharded:
        return g.transpose(1, 0, 2).reshape(a, N_DEV * b)
    return g.reshape(N_DEV * a, b)


def _full_to_owner_major(g, col_sharded):
    if g.ndim == 4:
        return g
    a, b = g.shape
    if col_sharded:
        return g.reshape(a, 4, 2, b // N_DEV).transpose(1, 2, 0, 3)
    return g.reshape(4, 2, a // N_DEV, b)


def _conv_pack(conv_a, conv_c):
    pad = lambda a, rows: jnp.pad(a, [(0, 0)] * (a.ndim - 2) + [(0, rows - a.shape[-2]), (0, 0)])
    return jnp.concatenate([pad(conv_a, 32), pad(conv_c, 8)], axis=-2)


def _conv_unpack(packed):
    return packed[..., :KA, :], packed[..., 32:32 + KC, :]


class _Overlapped(_Schedule):
    def __init__(self, shards_bf16, conv_shards, core):
        self.shards, self.conv_shards, self.core = shards_bf16, conv_shards, core
        self.full = {l: {} for l in range(DEPTH)}
        self.grads = {l: {} for l in range(DEPTH)}
        self.owner_major = {}
        self.from_sibling = {}
        self.scattering = None
        self.own, self.remote = {}, {}
        self.replicated = None

    def _gather(self, l, names):
        return _gather_rider([self.conv_shards[l] if k == "conv" else self.shards[k][l] for k in names])

    def _store(self, l, names, gathered):
        for k, g in zip(names, gathered):
            if k == "conv":
                self.full[l]["conv_a_w"], self.full[l]["conv_c_w"] = _conv_unpack(_gathered_to_full(g, True))
            else:
                self.full[l][k] = _gathered_to_full(g, k in COL_SHARDED)

    def weights(self, l):
        if l == 0 and not self.full[0]:
            names = GATHER_LAYER0["first"]
            self._store(0, names, _ride_alone(self._gather(0, names), "weights_all_gather_first"))
        return self.full[l]

    def note_grads(self, l, grads):
        self.grads[l].update(grads)

    def _owner_major(self, l, k):
        if (l, k) not in self.owner_major:
            g = self.grads[l]
            if k == "conv":
                self.owner_major[(l, k)] = _full_to_owner_major(_conv_pack(g["conv_a_w"], g["conv_c_w"]), True)
            else:
                self.owner_major[(l, k)] = _full_to_owner_major(g[k], k in COL_SHARDED)
        return self.owner_major[(l, k)]

    def _exchange(self, l, names):
        return _exchange_rider([self._owner_major(l, k) for k in names])

    def _scatter(self, groups):
        sums, sums_bf16, self.scattering = [], [], []
        for l, names in groups:
            for k, r in zip(names, self.from_sibling.pop((l, names))):
                s, sb = _pair_add(self._owner_major(l, k), r, self.core, f"grad_pair_add_{l}_{k}")
                sums.append(s)
                sums_bf16.append(sb)
                self.scattering.append((l, k))
        return _scatter_rider(sums, sums_bf16)

    def _scattered(self, results):
        n = len(self.scattering)
        for i, key in enumerate(self.scattering):
            self.own[key], self.remote[key] = results[i], results[n + i]

    def rider(self, stage, l):
        if l == 0 and stage in ("in_proj", "mixer_fwd"):
            return self._gather(0, GATHER_LAYER0[stage])
        if stage in GATHER_NEXT and l + 1 < DEPTH:
            return self._gather(l + 1, GATHER_NEXT[stage])
        if stage == "ff2_bwd" and l + 1 < DEPTH:
            return self._exchange(l + 1, GRADS_LATE)
        if stage == "attn_bwd":
            return self._exchange(l, GRADS_EARLY)
        if stage == "mixer_bwd":
            return self._scatter([(l, GRADS_EARLY)] + ([(l + 1, GRADS_LATE)] if l + 1 < DEPTH else []))
        if stage == "in_proj_bwd":
            packed = _pack_rows([jnp.stack([self.grads[i][k] for i in range(DEPTH)]) for k in REPLICATED])
            return _gather_rider([packed])
        return None

    def rode(self, stage, l, results):
        if l == 0 and stage in ("in_proj", "mixer_fwd"):
            self._store(0, GATHER_LAYER0[stage], results)
        elif stage in GATHER_NEXT:
            self._store(l + 1, GATHER_NEXT[stage], results)
        elif stage == "ff2_bwd":
            self.from_sibling[(l + 1, GRADS_LATE)] = results
        elif stage == "attn_bwd":
            self.from_sibling[(l, GRADS_EARLY)] = results
        elif stage == "mixer_bwd":
            self._scattered(results)
        elif stage == "in_proj_bwd":
            self.replicated = results[0]

    def finish(self):
        self.from_sibling[(0, GRADS_LATE)] = _ride_alone(self._exchange(0, GRADS_LATE), "grad_pair_exchange_last")
        self._scattered(_ride_alone(self._scatter([(0, GRADS_LATE)]), "grad_chip_scatter_last"))


def _pack_rows(parts):
    flat = jnp.concatenate([p.reshape(-1, PACK_LANES) for p in parts], axis=0)
    return jnp.pad(flat, ((0, -flat.shape[0] % 8), (0, 0)))


def _unpack_rows(packed, like):
    out, r = [], 0
    for p in like:
        n = p.size // PACK_LANES
        out.append(packed[r:r + n].reshape(p.shape))
        r += n
    return out


def kernel(x, mem, w_in, conv_a_w, conv_a_b, ln_a_g, ln_a_b, ln_v_g, ln_v_b, w_s, b_s, conv_c_w, w_out, ln1_g, ln1_b, w_q, w_kv, w_o, ln2_g, ln2_b, w_ff1, w_ff2, ln3_g, ln3_b, loss_target, m_w_in, m_conv_a_w, m_conv_a_b, m_ln_a_g, m_ln_a_b, m_ln_v_g, m_ln_v_b, m_w_s, m_b_s, m_conv_c_w, m_w_out, m_ln1_g, m_ln1_b, m_w_q, m_w_kv, m_w_o, m_ln2_g, m_ln2_b, m_w_ff1, m_w_ff2, m_ln3_g, m_ln3_b, v_w_in, v_conv_a_w, v_conv_a_b, v_ln_a_g, v_ln_a_b, v_ln_v_g, v_ln_v_b, v_w_s, v_b_s, v_conv_c_w, v_w_out, v_ln1_g, v_ln1_b, v_w_q, v_w_kv, v_w_o, v_ln2_g, v_ln2_b, v_w_ff1, v_w_ff2, v_ln3_g, v_ln3_b):
    given = dict(locals())
    w = {k: given[k] for k in WEIGHTS}
    mom = {k: given["m_" + k] for k in WEIGHTS}
    var = {k: given["v_" + k] for k in WEIGHTS}
    bd, s_len, _ = x.shape
    core = lax.axis_index("c").astype(jnp.int32).reshape(1)

    conv_pack = lambda d: _conv_pack(d["conv_a_w"], d["conv_c_w"])
    sched = _Overlapped({k: w[k].astype(BF16) for k in BIG}, conv_pack(w), core)
    sq, grad_x, grads = _local_step(x.reshape(bd * s_len, D), mem.reshape(-1, D), loss_target.reshape(bd * s_len, D),
                                    sched, {k: w[k] for k in REPLICATED}, bd)
    loss = lax.psum(0.5 * jnp.sum(sq) / D, ("x", "y", "c"))

    out = {}
    for k in BIG + ("conv",):
        own = jnp.stack([sched.own[(l, k)] for l in range(DEPTH)])[None]
        remote = jnp.stack([sched.remote[(l, k)] for l in range(DEPTH)], axis=1)
        if k == "conv":
            conv_out = _adamw(conv_pack(w), conv_pack(mom), conv_pack(var), [own, remote], "adamw_conv")
            unpacked = [_conv_unpack(o) for o in conv_out]
            out["conv_a_w"], out["conv_c_w"] = [u[0] for u in unpacked], [u[1] for u in unpacked]
        else:
            out[k] = _adamw(w[k], mom[k], var[k], [own, remote], f"adamw_{k}")

    rep_out = _adamw(_pack_rows([w[k] for k in REPLICATED]), _pack_rows([mom[k] for k in REPLICATED]),
                     _pack_rows([var[k] for k in REPLICATED]), [sched.replicated], "adamw_replicated")
    for i, o in enumerate(rep_out):
        for k, piece in zip(REPLICATED, _unpack_rows(o, [w[k] for k in REPLICATED])):
            out.setdefault(k, [None] * 4)[i] = piece

    res = [loss, grad_x.reshape(bd, s_len, D)]
    for i in range(4):
        res += [out[k][i] for k in WEIGHTS]
    return tuple(res)
```

```python
import functools
import math

import jax
import jax.numpy as jnp
from jax import lax
from jax.experimental import pallas as pl
from jax.experimental.pallas import tpu as pltpu

F32 = jnp.float32
BF16 = jnp.bfloat16

DEPTH = 4
D = 1024
D_A, D_B, D_C = 384, 256, 384
HEAD = 64
B_HEADS = 4
CHUNK = 128
KA, KC = 31, 3
HALO_A, HALO_C = 32, 8
IN_W = 2 * D_A + 2 * D_B + 3 * D_C
X_HEADS = 4
X_HD = D // X_HEADS
D_FF = 4 * D
EPS = 1e-5
ALPHA = (2.0 * DEPTH) ** 0.25
LR, B1, B2, ADAM_EPS, WD, STEP = 0.001, 0.9, 0.999, 1e-08, 0.01, 10
INV_SQRT2 = 0.7071067811865476
INV_SQRT_2PI = 0.3989422804014327
N_DEV = 8
VMEM_LIMIT = 56 * 1024 * 1024
MESH = pl.DeviceIdType.MESH
ANY = pl.BlockSpec(memory_space=pl.ANY)


def _cp(sem=None):
    return pltpu.CompilerParams(dimension_semantics=sem, vmem_limit_bytes=VMEM_LIMIT)


def _sds(shape, dtype):
    return jax.ShapeDtypeStruct(tuple(shape), dtype)


class _Rider:
    def __init__(self, arrays, out_shape, sems, start, finish):
        self.arrays, self.out_shape, self.sems, self.start, self.finish = arrays, out_shape, sems, start, finish


def _call(kern, name, grid, in_specs, out_specs, out_shape, args, sem, scratch=(), rider=None):
    single = not isinstance(out_shape, (list, tuple))
    out_specs_l = [out_specs] if single else list(out_specs)
    out_shape_l = [out_shape] if single else list(out_shape)
    if rider is None:
        res = pl.pallas_call(kern, name=name, grid=grid, in_specs=in_specs, out_specs=out_specs_l,
                             out_shape=out_shape_l, scratch_shapes=list(scratch), compiler_params=_cp(sem))(*args)
        return (res[0] if single else list(res)), None
    n_in, n_out, n_scr = len(args), len(out_shape_l), len(scratch)
    n_rin, n_rout = len(rider.arrays), len(rider.out_shape)

    def body(*refs):
        ins, refs = refs[:n_in], refs[n_in:]
        r_ins, refs = refs[:n_rin], refs[n_rin:]
        outs, refs = refs[:n_out], refs[n_out:]
        r_outs, refs = refs[:n_rout], refs[n_rout:]
        scr, r_sems = refs[:n_scr], refs[n_scr:]
        ids = [pl.program_id(d) for d in range(len(grid))]
        first = functools.reduce(jnp.logical_and, [i == 0 for i in ids])
        last = functools.reduce(jnp.logical_and, [i == g - 1 for i, g in zip(ids, grid)])

        @pl.when(first)
        def _():
            rider.start(r_ins, r_outs, r_sems)

        kern(*ins, *outs, *scr)

        @pl.when(last)
        def _():
            rider.finish(r_ins, r_outs, r_sems)

    res = pl.pallas_call(
        body, name=name, grid=grid, in_specs=list(in_specs) + [ANY] * n_rin,
        out_specs=out_specs_l + [ANY] * n_rout, out_shape=out_shape_l + list(rider.out_shape),
        scratch_shapes=list(scratch) + list(rider.sems),
        compiler_params=_cp(("arbitrary",) * len(grid)))(*args, *rider.arrays)
    mine, theirs = list(res[:n_out]), list(res[n_out:])
    return (mine[0] if single else mine), theirs


def _join_riders(riders):
    if len(riders) == 1:
        return riders[0]

    def parts(seq, attr):
        out, at = [], 0
        for r in riders:
            n = len(getattr(r, attr))
            out.append(seq[at:at + n])
            at += n
        return out

    def each(method):
        def run(ins, outs, sems):
            for r, i, o, s in zip(riders, parts(ins, "arrays"), parts(outs, "out_shape"), parts(sems, "sems")):
                getattr(r, method)(i, o, s)
        return run

    return _Rider(sum([r.arrays for r in riders], []), sum([r.out_shape for r in riders], []),
                  sum([r.sems for r in riders], []), each("start"), each("finish"))


def _ride_alone(rider, name):
    def body(*refs):
        n_rin, n_rout = len(rider.arrays), len(rider.out_shape)
        r_ins, r_outs, r_sems = refs[:n_rin], refs[n_rin:n_rin + n_rout], refs[n_rin + n_rout:]
        rider.start(r_ins, r_outs, r_sems)
        rider.finish(r_ins, r_outs, r_sems)

    return list(pl.pallas_call(
        body, name=name, in_specs=[ANY] * len(rider.arrays), out_specs=[ANY] * len(rider.out_shape),
        out_shape=list(rider.out_shape), scratch_shapes=list(rider.sems))(*rider.arrays))


def _ln(z):
    mu = jnp.mean(z, axis=-1, keepdims=True)
    zc = z - mu
    var = jnp.mean(zc * zc, axis=-1, keepdims=True)
    rstd = lax.rsqrt(var + EPS)
    return zc * rstd, rstd


def _ln_bwd(dxhat, xhat, rstd):
    m1 = jnp.mean(dxhat, axis=-1, keepdims=True)
    m2 = jnp.mean(dxhat * xhat, axis=-1, keepdims=True)
    return rstd * (dxhat - m1 - xhat * m2)


def _gelu(x):
    return 0.5 * x * (1.0 + lax.erf(x * INV_SQRT2))


def _gelu_and_grad(x):
    cdf = 0.5 * (1.0 + lax.erf(x * INV_SQRT2))
    return x * cdf, cdf + x * jnp.exp(-0.5 * x * x) * INV_SQRT_2PI


def _fold8(x):
    r, c = x.shape
    return jnp.sum(x.reshape(r // 8, 8, c), axis=0)


def _relu2(h):
    return jnp.square(jnp.maximum(h, 0.0))


def _weight_spec(block, index_map, resident):
    return pl.BlockSpec(block, index_map, pipeline_mode=pl.Buffered(1) if resident else None)


def _mm_nn(a, w, out_dtype, name, tm=512, tn=None, rider=None):
    n, k = a.shape
    m = w.shape[1]
    tm = min(tm, n)
    tn = m if tn is None else min(tn, m)

    def kern(a_ref, w_ref, o_ref):
        o_ref[...] = jnp.dot(a_ref[...].astype(BF16), w_ref[...], preferred_element_type=F32).astype(out_dtype)

    res, rode = _call(
        kern, name, (n // tm, m // tn),
        [pl.BlockSpec((tm, k), lambda i, j: (i, 0)), _weight_spec((k, tn), lambda i, j: (0, j), tn == m)],
        pl.BlockSpec((tm, tn), lambda i, j: (i, j)), _sds((n, m), out_dtype), (a, w), ("parallel", "parallel"),
        rider=rider)
    return res if rider is None else (res, rode)


def _mm_res_ln(a, w, res, g, b, name, relu2=False, tm=512, rider=None, then=None):
    n, k = a.shape
    tm = min(tm, n)

    m2 = None if then is None else then.shape[1]

    def kern(*refs):
        a_ref, w_ref, res_ref, g_ref, b_ref = refs[:5]
        x_ref, xhat_ref, rstd_ref = refs[-3:] if then is None else refs[-4:-1]
        av = a_ref[...]
        if relu2:
            av = _relu2(av.astype(F32))
        z = ALPHA * res_ref[...] + jnp.dot(av.astype(BF16), w_ref[...], preferred_element_type=F32)
        xhat, rstd = _ln(z)
        xhat_ref[...] = xhat
        rstd_ref[...] = rstd
        x = xhat * g_ref[...] + b_ref[...]
        x_ref[...] = x
        if then is not None:
            refs[-1][...] = jnp.dot(x.astype(BF16), refs[5][...], preferred_element_type=F32).astype(BF16)

    row = lambda i: (i, 0)
    fix = lambda i: (0, 0)
    in_specs = [pl.BlockSpec((tm, k), row), _weight_spec((k, D), fix, True), pl.BlockSpec((tm, D), row),
                pl.BlockSpec((1, D), fix), pl.BlockSpec((1, D), fix)]
    out_specs = [pl.BlockSpec((tm, D), row), pl.BlockSpec((tm, D), row), pl.BlockSpec((tm, 1), row)]
    out_shape = [_sds((n, D), F32), _sds((n, D), F32), _sds((n, 1), F32)]
    args = (a, w, res, g, b)
    if then is not None:
        in_specs.append(_weight_spec((D, m2), fix, True))
        out_specs.append(pl.BlockSpec((tm, m2), row))
        out_shape.append(_sds((n, m2), BF16))
        args += (then,)
    out, rode = _call(kern, name, (n // tm,), in_specs, out_specs, out_shape, args, ("parallel",), rider=rider)
    return out if rider is None else (out, rode)


def _mm_nt(a, w, out_dtype, name, relu2_grad_of=None, tm=512, tn=1024, rider=None):
    n, k = a.shape
    m = w.shape[0]
    tm = min(tm, n)
    tn = min(tn, m)
    with_h = relu2_grad_of is not None

    def kern(*refs):
        a_ref, w_ref = refs[0], refs[1]
        o_ref = refs[-1]
        r = lax.dot_general(a_ref[...].astype(BF16), w_ref[...], (((1,), (1,)), ((), ())), preferred_element_type=F32)
        if with_h:
            r = r * (2.0 * jnp.maximum(refs[2][...].astype(F32), 0.0))
        o_ref[...] = r.astype(out_dtype)

    in_specs = [pl.BlockSpec((tm, k), lambda i, j: (i, 0)), _weight_spec((tn, k), lambda i, j: (j, 0), tn == m)]
    args = [a, w]
    if with_h:
        in_specs.append(pl.BlockSpec((tm, tn), lambda i, j: (i, j)))
        args.append(relu2_grad_of)
    res, rode = _call(kern, name, (n // tm, m // tn), in_specs, pl.BlockSpec((tm, tn), lambda i, j: (i, j)),
                      _sds((n, m), out_dtype), args, ("parallel", "parallel"), rider=rider)
    return res if rider is None else (res, rode)


def _mm_tn_by_owner(a, b, name, tk=512):
    n, k1 = a.shape
    m = b.shape[1]
    s = m // N_DEV
    tk = min(tk, n)

    def kern(a_ref, b_ref, o_ref):
        @pl.when(pl.program_id(0) == 0)
        def _():
            o_ref[...] = jnp.zeros_like(o_ref)

        av = a_ref[...].astype(BF16)
        for j in range(N_DEV):
            o_ref[j // 2, j % 2] += lax.dot_general(av, b_ref[:, j * s:(j + 1) * s].astype(BF16),
                                                    (((0,), (0,)), ((), ())), preferred_element_type=F32)

    return pl.pallas_call(
        kern, name=name, grid=(n // tk,),
        in_specs=[pl.BlockSpec((tk, k1), lambda k: (k, 0)), pl.BlockSpec((tk, m), lambda k: (k, 0))],
        out_specs=pl.BlockSpec((4, 2, k1, s), lambda k: (0, 0, 0, 0), pipeline_mode=pl.Buffered(1)),
        out_shape=_sds((4, 2, k1, s), F32), compiler_params=_cp(("arbitrary",)))(a, b)


def _mm_tn(a, b, name, relu2=False, t1=1024, tn=2048, tk=512):
    n, k1 = a.shape
    m = b.shape[1]
    t1 = min(t1, k1)
    tn = m if m <= 2432 and m % tn else min(tn, m)
    tk = min(tk, n)

    def kern(a_ref, b_ref, o_ref):
        @pl.when(pl.program_id(2) == 0)
        def _():
            o_ref[...] = jnp.zeros_like(o_ref)

        av = a_ref[...]
        if relu2:
            av = _relu2(av.astype(F32))
        o_ref[...] += lax.dot_general(av.astype(BF16), b_ref[...].astype(BF16), (((0,), (0,)), ((), ())),
                                      preferred_element_type=F32)

    return pl.pallas_call(
        kern, name=name, grid=(k1 // t1, m // tn, n // tk),
        in_specs=[pl.BlockSpec((tk, t1), lambda i, j, k: (k, i)), pl.BlockSpec((tk, tn), lambda i, j, k: (k, j))],
        out_specs=pl.BlockSpec((t1, tn), lambda i, j, k: (i, j)),
        out_shape=_sds((k1, m), F32),
        compiler_params=_cp(("parallel", "parallel", "arbitrary")))(a, b)


def _bwd_in(dz_next, da, w, name, ln=None, tm=512, rider=None, then=None):
    n, k2 = da.shape
    tm = min(tm, n)
    row = lambda i: (i, 0)
    fix = lambda i: (0, 0)

    def dx_of(dzn_ref, da_ref, w_ref):
        return ALPHA * dzn_ref[...] + lax.dot_general(da_ref[...], w_ref[...], (((1,), (1,)), ((), ())),
                                                      preferred_element_type=F32)

    base_specs = [pl.BlockSpec((tm, D), row), pl.BlockSpec((tm, k2), row), _weight_spec((D, k2), fix, True)]
    if ln is None:
        def kern(dzn_ref, da_ref, w_ref, dx_ref):
            dx_ref[...] = dx_of(dzn_ref, da_ref, w_ref)

        out, rode = _call(kern, name, (n // tm,), base_specs, pl.BlockSpec((tm, D), row), _sds((n, D), F32),
                          (dz_next, da, w), ("parallel",), rider=rider)
        return out if rider is None else (out, rode)

    xhat, rstd, g = ln

    def kern(*refs):
        dzn_ref, da_ref, w_ref, xhat_ref, rstd_ref, g_ref = refs[:6]
        dz_ref, dg_ref, db_ref = refs[-3:] if then is None else refs[-4:-1]

        @pl.when(pl.program_id(0) == 0)
        def _():
            dg_ref[...] = jnp.zeros_like(dg_ref)
            db_ref[...] = jnp.zeros_like(db_ref)

        dx = dx_of(dzn_ref, da_ref, w_ref)
        xh = xhat_ref[...]
        dg_ref[...] += _fold8(dx * xh)
        db_ref[...] += _fold8(dx)
        dz = _ln_bwd(dx * g_ref[...], xh, rstd_ref[...])
        dz_ref[...] = dz
        if then is not None:
            refs[-1][...] = lax.dot_general(dz.astype(BF16), refs[6][...], (((1,), (1,)), ((), ())),
                                            preferred_element_type=F32).astype(BF16)

    in_specs = base_specs + [pl.BlockSpec((tm, D), row), pl.BlockSpec((tm, 1), row), pl.BlockSpec((1, D), fix)]
    out_specs = [pl.BlockSpec((tm, D), row), pl.BlockSpec((8, D), fix), pl.BlockSpec((8, D), fix)]
    out_shape = [_sds((n, D), F32), _sds((8, D), F32), _sds((8, D), F32)]
    args = (dz_next, da, w, xhat, rstd, g)
    if then is not None:
        m3 = then.shape[0]
        in_specs.append(_weight_spec((m3, D), fix, True))
        out_specs.append(pl.BlockSpec((tm, m3), row))
        out_shape.append(_sds((n, m3), BF16))
        args += (then,)
    out, rode = _call(kern, name, (n // tm,), in_specs, out_specs, out_shape, args, ("arbitrary",), rider=rider)
    return out if rider is None else (out, rode)


def _loss_lnbwd(x, target, xhat, rstd, g, name, tm=512):
    n = x.shape[0]
    tm = min(tm, n)
    row = lambda i: (i, 0)
    fix = lambda i: (0, 0)

    def kern(x_ref, t_ref, xhat_ref, rstd_ref, g_ref, sq_ref, dz_ref, dg_ref, db_ref):
        @pl.when(pl.program_id(0) == 0)
        def _():
            sq_ref[...] = jnp.zeros_like(sq_ref)
            dg_ref[...] = jnp.zeros_like(dg_ref)
            db_ref[...] = jnp.zeros_like(db_ref)

        err = x_ref[...] - t_ref[...]
        sq_ref[...] += _fold8(err * err)
        dx = err * (1.0 / D)
        xh = xhat_ref[...]
        dg_ref[...] += _fold8(dx * xh)
        db_ref[...] += _fold8(dx)
        dz_ref[...] = _ln_bwd(dx * g_ref[...], xh, rstd_ref[...])

    return pl.pallas_call(
        kern, name=name, grid=(n // tm,),
        in_specs=[pl.BlockSpec((tm, D), row), pl.BlockSpec((tm, D), row), pl.BlockSpec((tm, D), row),
                  pl.BlockSpec((tm, 1), row), pl.BlockSpec((1, D), fix)],
        out_specs=[pl.BlockSpec((8, D), fix), pl.BlockSpec((tm, D), row), pl.BlockSpec((8, D), fix),
                   pl.BlockSpec((8, D), fix)],
        out_shape=[_sds((8, D), F32), _sds((n, D), F32), _sds((8, D), F32), _sds((8, D), F32)],
        compiler_params=_cp(("arbitrary",)))(x, target, xhat, rstd, g)


def _softmax_rows(s):
    s = s - jnp.max(s, axis=-1, keepdims=True)
    e = jnp.exp(s)
    return e / jnp.sum(e, axis=-1, keepdims=True)


def _attn_fwd(q, kv, bd, name, tm=512):
    n = q.shape[0]
    s_len = n // bd
    m_len = kv.shape[0] // bd
    tm = min(tm, s_len)
    nt = s_len // tm
    scale = X_HD ** -0.5

    def kern(q_ref, k_ref, v_ref, o_ref):
        for h in range(X_HEADS):
            cs = slice(h * X_HD, (h + 1) * X_HD)
            s = lax.dot_general(q_ref[:, cs], k_ref[:, cs], (((1,), (1,)), ((), ())), preferred_element_type=F32)
            p = _softmax_rows(s * scale)
            o_ref[:, cs] = jnp.dot(p.astype(BF16), v_ref[:, cs], preferred_element_type=F32).astype(BF16)

    return pl.pallas_call(
        kern, name=name, grid=(bd, nt),
        in_specs=[pl.BlockSpec((tm, D), lambda b, i: (b * nt + i, 0)),
                  pl.BlockSpec((m_len, D), lambda b, i: (b, 0)), pl.BlockSpec((m_len, D), lambda b, i: (b, 1))],
        out_specs=pl.BlockSpec((tm, D), lambda b, i: (b * nt + i, 0)),
        out_shape=_sds((n, D), BF16),
        compiler_params=_cp(("parallel", "parallel")))(q, kv, kv)


def _attn_bwd(q, kv, do, bd, name, tm=512, rider=None):
    n = q.shape[0]
    s_len = n // bd
    m_len = kv.shape[0] // bd
    tm = min(tm, s_len)
    nt = s_len // tm
    scale = X_HD ** -0.5

    def kern(q_ref, k_ref, v_ref, do_ref, dq_ref, dkv_ref):
        @pl.when(pl.program_id(1) == 0)
        def _():
            dkv_ref[...] = jnp.zeros_like(dkv_ref)

        for h in range(X_HEADS):
            cs = slice(h * X_HD, (h + 1) * X_HD)
            vs = slice(D + h * X_HD, D + (h + 1) * X_HD)
            qh, kh, vh, doh = q_ref[:, cs], k_ref[:, cs], v_ref[:, cs], do_ref[:, cs]
            s = lax.dot_general(qh, kh, (((1,), (1,)), ((), ())), preferred_element_type=F32)
            p = _softmax_rows(s * scale)
            pb = p.astype(BF16)
            dp = lax.dot_general(doh, vh, (((1,), (1,)), ((), ())), preferred_element_type=F32)
            dkv_ref[:, vs] += lax.dot_general(pb, doh, (((0,), (0,)), ((), ())), preferred_element_type=F32)
            ds = (p * (dp - jnp.sum(dp * p, axis=-1, keepdims=True)) * scale).astype(BF16)
            dq_ref[:, cs] = jnp.dot(ds, kh, preferred_element_type=F32).astype(BF16)
            dkv_ref[:, cs] += lax.dot_general(ds, qh, (((0,), (0,)), ((), ())), preferred_element_type=F32)

    out, rode = _call(
        kern, name, (bd, nt),
        [pl.BlockSpec((tm, D), lambda b, i: (b * nt + i, 0)),
         pl.BlockSpec((m_len, D), lambda b, i: (b, 0)), pl.BlockSpec((m_len, D), lambda b, i: (b, 1)),
         pl.BlockSpec((tm, D), lambda b, i: (b * nt + i, 0))],
        [pl.BlockSpec((tm, D), lambda b, i: (b * nt + i, 0)), pl.BlockSpec((m_len, 2 * D), lambda b, i: (b, 0))],
        [_sds((n, D), BF16), _sds((bd * m_len, 2 * D), F32)], (q, kv, kv, do), ("parallel", "arbitrary"), rider=rider)
    return out if rider is None else (out, rode)


C_AV, C_AG, C_BU, C_BV, C_CB, C_CC, C_CX = 0, 384, 768, 1024, 1280, 1664, 2048


class _Windows:
    def __init__(self, win_ref, r0, halo, cs):
        self.win = win_ref[pl.ds(r0, CHUNK + halo), cs]
        self.n = CHUNK + halo - 8
        self.shifted = {0: self.win}

    def rows(self, o):
        b, a = o % 8, o // 8
        if b not in self.shifted:
            self.shifted[b] = self.win[b:b + self.n, :]
        return self.shifted[b][8 * a:8 * a + CHUNK, :]


def _conv_taps(win_ref, r0, halo, w_ref, n_taps, offset_of_tap):
    parts = []
    for cb in range(3):
        cs = slice(cb * 128, (cb + 1) * 128)
        win = _Windows(win_ref, r0, halo, cs)
        acc = jnp.zeros((CHUNK, 128), F32)
        for k in range(n_taps):
            acc = acc + win.rows(offset_of_tap(k)) * w_ref[k:k + 1, cs]
        parts.append(acc)
    return jnp.concatenate(parts, axis=1)


def _causal_conv(win_ref, r0, halo, w_ref, n_taps):
    return _conv_taps(win_ref, r0, halo, w_ref, n_taps, lambda k: halo - (n_taps - 1) + k)


def _anticausal_conv(win_ref, r0, halo, w_ref, n_taps):
    return _conv_taps(win_ref, r0, halo, w_ref, n_taps, lambda k: n_taps - 1 - k)


def _head_of_lane():
    return lax.broadcasted_iota(jnp.int32, (1, D_B), 1) // HEAD


def _spatial_mix(wm_ref, vb, head):
    mixed = jnp.zeros((CHUNK, D_B), F32)
    for h in range(B_HEADS):
        mh = jnp.dot(wm_ref[h], vb, preferred_element_type=F32)
        mixed = jnp.where(head == h, mh, mixed)
    return mixed


def _mixer_fwd(proj, p, bd, name, rider=None):
    n = proj.shape[0]
    s_len = n // bd
    n_chunks = s_len // CHUNK

    def kern(proj_ref, caw_ref, cab_ref, lag_ref, lab_ref, lvg_ref, lvb_ref, wm_ref, bsx_ref, ccw_ref, cat_ref,
             ca_ref, gs_ref, ccs_ref):
        gs_ref[0:HALO_A, :] = jnp.zeros((HALO_A, D_A), F32)
        ccs_ref[0:HALO_C, :] = jnp.zeros((HALO_C, D_C), F32)
        head = _head_of_lane()

        def chunk(i, carry):
            r0 = pl.multiple_of(i * CHUNK, CHUNK)
            rows = pl.ds(r0, CHUNK)
            ld = lambda c0, w: proj_ref[rows, c0:c0 + w].astype(F32)
            gs_ref[pl.ds(r0 + HALO_A, CHUNK), :] = ld(C_AV, D_A) * jax.nn.sigmoid(ld(C_AG, D_A))
            ca = _causal_conv(gs_ref, r0, HALO_A, caw_ref, KA) + cab_ref[...]
            ca_ref[rows, :] = ca
            lna = _ln(ca)[0] * lag_ref[...] + lab_ref[...]
            cat_ref[rows, 0:D_A] = (lna * jax.nn.sigmoid(lna)).astype(BF16)
            u = _gelu(ld(C_BU, D_B))
            v = _ln(_gelu(ld(C_BV, D_B)))[0] * lvg_ref[...] + lvb_ref[...]
            mixed = _spatial_mix(wm_ref, v.astype(BF16), head) + bsx_ref[...]
            cat_ref[rows, D_A:D_A + D_B] = (u * mixed).astype(BF16)
            ccs_ref[pl.ds(r0 + HALO_C, CHUNK), :] = ld(C_CC, D_C) * ld(C_CX, D_C)
            conv = _causal_conv(ccs_ref, r0, HALO_C, ccw_ref, KC)
            cat_ref[rows, D_A + D_B:D] = (ld(C_CB, D_C) * conv).astype(BF16)
            return carry

        lax.fori_loop(0, n_chunks, chunk, 0)

    fix2 = lambda b: (0, 0)
    args = [proj, p["caw"], p["cab"], p["lag"], p["lab"], p["lvg"], p["lvb"], p["wm"], p["bsx"], p["ccw"]]
    in_specs = [pl.BlockSpec((s_len, IN_W), lambda b: (b, 0))]
    for a in args[1:]:
        in_specs.append(pl.BlockSpec(a.shape, (lambda b: (0, 0, 0)) if a.ndim == 3 else fix2))
    res, rode = _call(
        kern, name, (bd,), in_specs,
        [pl.BlockSpec((s_len, D), lambda b: (b, 0)), pl.BlockSpec((s_len, D_A), lambda b: (b, 0))],
        [_sds((n, D), BF16), _sds((n, D_A), F32)], args, ("parallel",),
        scratch=[pltpu.VMEM((s_len + HALO_A, D_A), F32), pltpu.VMEM((s_len + HALO_C, D_C), F32)], rider=rider)
    return res if rider is None else (res, rode)


def _mixer_bwd(proj, dcat, ca, p, bd, name, rider=None):
    n = proj.shape[0]
    s_len = n // bd
    n_chunks = s_len // CHUNK

    def kern(proj_ref, dcat_ref, ca_ref, caw_ref, cab_ref, lag_ref, lab_ref, lvg_ref, lvb_ref, wm_ref, bsx_ref, ccw_ref,
             dproj_ref, dcaw_ref, dcab_ref, dlag_ref, dlab_ref, dlvg_ref, dlvb_ref, dws_ref, dbs_ref, dccw_ref,
             gs_ref, dcas_ref, ccs_ref, dcs_ref, a_caw, a_cab, a_lag, a_lab, a_lvg, a_lvb, a_ccw):
        gs_ref[0:HALO_A, :] = jnp.zeros((HALO_A, D_A), F32)
        ccs_ref[0:HALO_C, :] = jnp.zeros((HALO_C, D_C), F32)
        dcas_ref[s_len:s_len + HALO_A, :] = jnp.zeros((HALO_A, D_A), F32)
        dcs_ref[s_len:s_len + HALO_C, :] = jnp.zeros((HALO_C, D_C), F32)
        for acc in (a_caw, a_cab, a_lag, a_lab, a_lvg, a_lvb, a_ccw, dws_ref, dbs_ref):
            acc[...] = jnp.zeros_like(acc)
        head = _head_of_lane()
        lane128 = lax.broadcasted_iota(jnp.int32, (1, CHUNK), 1)

        def pass1(i, carry):
            r0 = pl.multiple_of(i * CHUNK, CHUNK)
            rows = pl.ds(r0, CHUNK)
            ld = lambda c0, w: proj_ref[rows, c0:c0 + w].astype(F32)
            dld = lambda c0, w: dcat_ref[rows, c0:c0 + w].astype(F32)
            gs_ref[pl.ds(r0 + HALO_A, CHUNK), :] = ld(C_AV, D_A) * jax.nn.sigmoid(ld(C_AG, D_A))
            xh, rstd = _ln(ca_ref[rows, :])
            lna = xh * lag_ref[...] + lab_ref[...]
            sg = jax.nn.sigmoid(lna)
            dlna = dld(0, D_A) * (sg * (1.0 + lna * (1.0 - sg)))
            a_lag[...] += _fold8(dlna * xh)
            a_lab[...] += _fold8(dlna)
            dca = _ln_bwd(dlna * lag_ref[...], xh, rstd)
            dcas_ref[rows, :] = dca
            a_cab[...] += _fold8(dca)
            for cb in range(3):
                cs = slice(cb * 128, (cb + 1) * 128)
                win = _Windows(gs_ref, r0, HALO_A, cs)
                dcab = dca[:, cs]
                for k in range(KA):
                    a_caw[k * 8:(k + 1) * 8, cs] += _fold8(dcab * win.rows(HALO_A - (KA - 1) + k))
            pu, pv = ld(C_BU, D_B), ld(C_BV, D_B)
            u, du_dpu = _gelu_and_grad(pu)
            gv, dgv_dpv = _gelu_and_grad(pv)
            vxh, vrstd = _ln(gv)
            v = vxh * lvg_ref[...] + lvb_ref[...]
            vb = v.astype(BF16)
            mixed = _spatial_mix(wm_ref, vb, head) + bsx_ref[...]
            dbo = dld(D_A, D_B)
            dproj_ref[rows, C_BU:C_BU + D_B] = (dbo * mixed * du_dpu).astype(BF16)
            dmixed = dbo * u
            dv = jnp.zeros((CHUNK, D_B), F32)
            bsum = jnp.zeros((CHUNK, CHUNK), F32)
            for h in range(B_HEADS):
                dmh = jnp.where(head == h, dmixed, 0.0)
                dmb = dmh.astype(BF16)
                dvh = lax.dot_general(wm_ref[h], dmb, (((0,), (0,)), ((), ())), preferred_element_type=F32)
                dv = jnp.where(head == h, dvh, dv)
                dws_ref[h] += lax.dot_general(dmb, vb, (((1,), (1,)), ((), ())), preferred_element_type=F32)
                bsum = bsum + jnp.where(lane128 == h, jnp.sum(dmh, axis=-1, keepdims=True), 0.0)
            dbs_ref[...] += bsum
            a_lvg[...] += _fold8(dv * vxh)
            a_lvb[...] += _fold8(dv)
            dgv = _ln_bwd(dv * lvg_ref[...], vxh, vrstd)
            dproj_ref[rows, C_BV:C_BV + D_B] = (dgv * dgv_dpv).astype(BF16)
            ccs_ref[pl.ds(r0 + HALO_C, CHUNK), :] = ld(C_CC, D_C) * ld(C_CX, D_C)
            conv = _causal_conv(ccs_ref, r0, HALO_C, ccw_ref, KC)
            dco = dld(D_A + D_B, D_C)
            dproj_ref[rows, C_CB:C_CB + D_C] = (dco * conv).astype(BF16)
            dconv = dco * ld(C_CB, D_C)
            dcs_ref[rows, :] = dconv
            for cb in range(3):
                cs = slice(cb * 128, (cb + 1) * 128)
                win = _Windows(ccs_ref, r0, HALO_C, cs)
                for k in range(KC):
                    a_ccw[k * 8:(k + 1) * 8, cs] += _fold8(dconv[:, cs] * win.rows(HALO_C - (KC - 1) + k))
            return carry

        lax.fori_loop(0, n_chunks, pass1, 0)

        def pass2(i, carry):
            r0 = pl.multiple_of(i * CHUNK, CHUNK)
            rows = pl.ds(r0, CHUNK)
            ld = lambda c0, w: proj_ref[rows, c0:c0 + w].astype(F32)
            dg = _anticausal_conv(dcas_ref, r0, HALO_A, caw_ref, KA)
            pa = ld(C_AV, D_A)
            sg = jax.nn.sigmoid(ld(C_AG, D_A))
            dproj_ref[rows, C_AV:C_AV + D_A] = (dg * sg).astype(BF16)
            dproj_ref[rows, C_AG:C_AG + D_A] = (dg * pa * sg * (1.0 - sg)).astype(BF16)
            dcc = _anticausal_conv(dcs_ref, r0, HALO_C, ccw_ref, KC)
            dproj_ref[rows, C_CC:C_CC + D_C] = (dcc * ld(C_CX, D_C)).astype(BF16)
            dproj_ref[rows, C_CX:C_CX + D_C] = (dcc * ld(C_CC, D_C)).astype(BF16)
            return carry

        lax.fori_loop(0, n_chunks, pass2, 0)

        for k in range(KA):
            dcaw_ref[k:k + 1, :] = jnp.sum(a_caw[k * 8:(k + 1) * 8, :], axis=0, keepdims=True)
        dcaw_ref[KA:KA + 1, :] = jnp.zeros((1, D_A), F32)
        for k in range(8):
            if k < KC:
                dccw_ref[k:k + 1, :] = jnp.sum(a_ccw[k * 8:(k + 1) * 8, :], axis=0, keepdims=True)
            else:
                dccw_ref[k:k + 1, :] = jnp.zeros((1, D_C), F32)
        dcab_ref[...] = a_cab[...]
        dlag_ref[...] = a_lag[...]
        dlab_ref[...] = a_lab[...]
        dlvg_ref[...] = a_lvg[...]
        dlvb_ref[...] = a_lvb[...]

    fix2 = lambda b: (0, 0)
    args = [proj, dcat, ca, p["caw"], p["cab"], p["lag"], p["lab"], p["lvg"], p["lvb"], p["wm"], p["bsx"], p["ccw"]]
    once = pl.Buffered(1)
    in_specs = [pl.BlockSpec((s_len, IN_W), lambda b: (b, 0), pipeline_mode=once),
                pl.BlockSpec((s_len, D), lambda b: (b, 0), pipeline_mode=once),
                pl.BlockSpec((s_len, D_A), lambda b: (b, 0), pipeline_mode=once)]
    for a in args[3:]:
        in_specs.append(pl.BlockSpec(a.shape, (lambda b: (0, 0, 0)) if a.ndim == 3 else fix2))

    def per_seq(*shape):
        nd = len(shape)
        return (pl.BlockSpec((None,) + shape, lambda b: (b,) + (0,) * nd), _sds((bd,) + shape, F32))

    outs = [(pl.BlockSpec((s_len, IN_W), lambda b: (b, 0), pipeline_mode=once), _sds((n, IN_W), BF16)),
            per_seq(32, D_A), per_seq(8, D_A), per_seq(8, D_A), per_seq(8, D_A), per_seq(8, D_B), per_seq(8, D_B),
            per_seq(B_HEADS, CHUNK, CHUNK), per_seq(CHUNK, CHUNK), per_seq(8, D_C)]
    res, rode = _call(
        kern, name, (bd,), in_specs, [o[0] for o in outs], [o[1] for o in outs], args, ("parallel",),
        scratch=[pltpu.VMEM((s_len + HALO_A, D_A), F32), pltpu.VMEM((s_len + HALO_A, D_A), F32),
                 pltpu.VMEM((s_len + HALO_C, D_C), F32), pltpu.VMEM((s_len + HALO_C, D_C), F32),
                 pltpu.VMEM((KA * 8, D_A), F32), pltpu.VMEM((8, D_A), F32), pltpu.VMEM((8, D_A), F32),
                 pltpu.VMEM((8, D_A), F32), pltpu.VMEM((8, D_B), F32), pltpu.VMEM((8, D_B), F32),
                 pltpu.VMEM((KC * 8, D_C), F32)],
        rider=rider)
    return res if rider is None else (res, rode)


def _adamw_update(w, m, v, g):
    mn = B1 * m + (1.0 - B1) * g
    vn = B2 * v + (1.0 - B2) * jnp.square(g)
    m_hat = mn / (1.0 - B1 ** STEP)
    v_hat = vn / (1.0 - B2 ** STEP)
    return -LR * (m_hat / (jnp.sqrt(v_hat) + ADAM_EPS) + WD * w), mn, vn


def _adamw_layers(w, m, v, own, remote, name):
    shape = w.shape
    depth, c = shape[0], shape[-1]
    r = math.prod(shape[1:-1])
    tr = 256 if r % 256 == 0 else r
    nt = r // tr

    def kern(*refs):
        w_ref, m_ref, v_ref = refs[:3]
        own_refs, rem_refs = refs[3:3 + depth], refs[3 + depth:3 + 4 * depth]
        go_ref, d_ref, mo_ref, vo_ref = refs[3 + 4 * depth:]
        for lp in range(depth):
            @pl.when(pl.program_id(0) == lp)
            def _(lp=lp):
                g = own_refs[lp][...]
                for j in range(3):
                    g = g + rem_refs[3 * lp + j][...].astype(F32)
                go_ref[...] = g
                d_ref[...], mo_ref[...], vo_ref[...] = _adamw_update(w_ref[...], m_ref[...], v_ref[...], g)

    def rows_of(lp):
        return lambda l, i: jnp.where(l == lp, i, jnp.where(l < lp, 0, nt - 1))

    spec = pl.BlockSpec((None, tr, c), lambda l, i: (l, i, 0))
    own_specs = [pl.BlockSpec((tr, c), functools.partial(lambda f, l, i: (f(l, i), 0), rows_of(lp)))
                 for lp in range(depth)]
    rem_specs = [pl.BlockSpec((None, tr, c), functools.partial(lambda f, j, l, i: (j, f(l, i), 0), rows_of(lp), j))
                 for lp in range(depth) for j in range(3)]
    as3d = lambda a: a.reshape(depth, r, c)
    outs = pl.pallas_call(
        kern, name=name, grid=(depth, nt), in_specs=[spec] * 3 + own_specs + rem_specs, out_specs=[spec] * 4,
        out_shape=[_sds((depth, r, c), F32)] * 4, compiler_params=_cp(("arbitrary", "arbitrary")))(
            as3d(w), as3d(m), as3d(v), *[o.reshape(r, c) for o in own],
            *[x.reshape(3, r, c) for x in remote for _ in range(3)])
    return [o.reshape(shape) for o in outs]


def _adamw(w, m, v, g_parts, name):
    shape = w.shape
    c = shape[-1]
    r = math.prod(shape[:-1])
    as2d = lambda a: a.reshape(r, c)
    tr = 512 if r % 512 == 0 else r
    slots = [(a.reshape(a.shape[0], r, c), p) for a in g_parts for p in range(a.shape[0])]
    n_g = len(slots)

    def kern(*refs):
        w_ref, m_ref, v_ref = refs[:3]
        g_refs = refs[3:3 + n_g]
        go_ref, d_ref, mo_ref, vo_ref = refs[3 + n_g:]
        g = g_refs[0][...].astype(F32)
        for gr in g_refs[1:]:
            g = g + gr[...].astype(F32)
        go_ref[...] = g
        d_ref[...], mo_ref[...], vo_ref[...] = _adamw_update(w_ref[...], m_ref[...], v_ref[...], g)

    spec = pl.BlockSpec((tr, c), lambda i: (i, 0))
    g_specs = [pl.BlockSpec((None, tr, c), functools.partial(lambda p, i: (p, i, 0), p)) for _, p in slots]
    outs = pl.pallas_call(
        kern, name=name, grid=(r // tr,), in_specs=[spec] * 3 + g_specs, out_specs=[spec] * 4,
        out_shape=[_sds((r, c), F32)] * 4, compiler_params=_cp(("parallel",)))(
            as2d(w), as2d(m), as2d(v), *[a for a, _ in slots])
    return [o.reshape(shape) for o in outs]


def _place():
    x, y, c = lax.axis_index("x"), lax.axis_index("y"), lax.axis_index("c")
    other_chips = [(1 - x, y), (x, 1 - y), (1 - x, 1 - y)]
    return x, y, c, other_chips


def _gather_rider(arrays):
    n_arr = len(arrays)

    def parts(ins, outs, sems):
        send_sems, recv_sems, local_sems = sems
        x, y, c, chips = _place()
        me, sibling = (x, y, c), (x, y, 1 - c)
        slot = lambda px, py, pc: 4 * px + 2 * py + pc

        def copy(a, k, block, to, from_input=False):
            dst = outs[a].at[slot(*block)]
            return pltpu.make_async_remote_copy(
                src_ref=ins[a] if from_input else dst, dst_ref=dst, send_sem=send_sems.at[k, a],
                recv_sem=recv_sems.at[k, a], device_id=to, device_id_type=MESH)

        mine = [pltpu.make_async_copy(ins[a], outs[a].at[slot(*me)], local_sems.at[a]) for a in range(n_arr)]
        first = []
        for a in range(n_arr):
            first.append(copy(a, 0, me, sibling, True))
            first += [copy(a, 1 + j, me, (*chip, c), True) for j, chip in enumerate(chips)]
        return copy, mine, first, me, sibling, chips, c

    def start(ins, outs, sems):
        _, mine, first, *_ = parts(ins, outs, sems)
        for cp in mine + first:
            cp.start()

    def finish(ins, outs, sems):
        copy, mine, first, me, sibling, chips, c = parts(ins, outs, sems)
        passed = []
        for j, chip in enumerate(chips):
            for a in range(n_arr):
                copy(a, 1 + j, (*chip, c), me).wait_recv()
                passed.append(copy(a, 4 + j, (*chip, c), sibling))
                passed[-1].start()
        for a in range(n_arr):
            copy(a, 0, sibling, me).wait_recv()
            for j, chip in enumerate(chips):
                copy(a, 4 + j, (*chip, 1 - c), me).wait_recv()
        for cp in first + passed:
            cp.wait_send()
        for cp in mine:
            cp.wait()

    return _Rider(list(arrays), [_sds((N_DEV,) + a.shape, a.dtype) for a in arrays],
                  [pltpu.SemaphoreType.DMA((7, n_arr)), pltpu.SemaphoreType.DMA((7, n_arr)),
                   pltpu.SemaphoreType.DMA((n_arr,))], start, finish)


def _exchange_rider(arrays):
    n_arr = len(arrays)

    def copies(ins, outs, sems):
        send_sems, recv_sems = sems
        x, y, c, _ = _place()
        return [pltpu.make_async_remote_copy(
            src_ref=ins[a].at[:, 1 - c], dst_ref=outs[a], send_sem=send_sems.at[a], recv_sem=recv_sems.at[a],
            device_id=(x, y, 1 - c), device_id_type=MESH) for a in range(n_arr)]

    def start(ins, outs, sems):
        for cp in copies(ins, outs, sems):
            cp.start()

    def finish(ins, outs, sems):
        for cp in copies(ins, outs, sems):
            cp.wait()

    return _Rider(list(arrays), [_sds(a.shape[:1] + a.shape[2:], a.dtype) for a in arrays],
                  [pltpu.SemaphoreType.DMA((n_arr,)), pltpu.SemaphoreType.DMA((n_arr,))], start, finish)


def _pair_add(mine, theirs, core, name):
    _, _, r, c = mine.shape
    tr = 512 if r % 512 == 0 else r

    def kern(core_ref, a_ref, b_ref, o_ref, ob_ref):
        s = a_ref[...] + b_ref[...]
        o_ref[...] = s
        ob_ref[...] = s.astype(BF16)

    row = lambda t, i, core_ref: (t, i, 0)
    return pl.pallas_call(
        kern, name=name,
        grid_spec=pltpu.PrefetchScalarGridSpec(
            num_scalar_prefetch=1, grid=(4, r // tr),
            in_specs=[pl.BlockSpec((None, None, tr, c), lambda t, i, core_ref: (t, core_ref[0], i, 0)),
                      pl.BlockSpec((None, tr, c), row)],
            out_specs=[pl.BlockSpec((None, tr, c), row), pl.BlockSpec((None, tr, c), row)]),
        out_shape=[_sds((4, r, c), F32), _sds((4, r, c), BF16)],
        compiler_params=_cp(("parallel", "parallel")))(core, mine, theirs)


def _scatter_rider(sums, sums_bf16):
    n_arr = len(sums)

    def copies(ins, outs, sems):
        send_sems, recv_sems, local_sems = sems
        x, y, c, chips = _place()
        own = [pltpu.make_async_copy(ins[a].at[2 * x + y], outs[a], local_sems.at[a]) for a in range(n_arr)]
        remote = []
        for a in range(n_arr):
            for j, (px, py) in enumerate(chips):
                remote.append(pltpu.make_async_remote_copy(
                    src_ref=ins[n_arr + a].at[2 * px + py], dst_ref=outs[n_arr + a].at[j], send_sem=send_sems.at[j, a],
                    recv_sem=recv_sems.at[j, a], device_id=(px, py, c), device_id_type=MESH))
        return own + remote

    def start(ins, outs, sems):
        for cp in copies(ins, outs, sems):
            cp.start()

    def finish(ins, outs, sems):
        for cp in copies(ins, outs, sems):
            cp.wait()

    return _Rider(list(sums) + list(sums_bf16),
                  [_sds(a.shape[1:], a.dtype) for a in sums] + [_sds((3,) + a.shape[1:], a.dtype) for a in sums_bf16],
                  [pltpu.SemaphoreType.DMA((3, n_arr)), pltpu.SemaphoreType.DMA((3, n_arr)),
                   pltpu.SemaphoreType.DMA((n_arr,))], start, finish)


def _mixer_params(conv_a_w, conv_a_b, ln_a_g, ln_a_b, ln_v_g, ln_v_b, w_s, b_s, conv_c_w):
    causal = jnp.tril(jnp.ones((CHUNK, CHUNK), dtype=bool))
    row = lambda a: a.reshape(1, -1)
    return dict(
        caw=jnp.pad(conv_a_w, ((0, 32 - KA), (0, 0))), cab=row(conv_a_b), lag=row(ln_a_g), lab=row(ln_a_b),
        lvg=row(ln_v_g), lvb=row(ln_v_b), wm=jnp.where(causal[None], w_s, 0.0).astype(BF16),
        bsx=jnp.repeat(b_s.T, HEAD, axis=1), ccw=jnp.pad(conv_c_w, ((0, 8 - KC), (0, 0))))


class _Schedule:
    def __init__(self, big=None):
        self.big = big

    def weights(self, l):
        return {k: v[l] for k, v in self.big.items()}

    def rider(self, stage, l):
        return None

    def rode(self, stage, l, results):
        pass

    def note_grads(self, l, grads):
        pass

    def finish(self):
        pass


def _local_step(x, mem, target, sched, small, bd):
    row = lambda a: a.reshape(1, -1)

    def ride(stage, l, fn, *args, **kw):
        rider = sched.rider(stage, l)
        res = fn(*args, rider=rider, **kw)
        if rider is not None:
            res, results = res
            sched.rode(stage, l, results)
        return res

    saved = []
    for l in range(DEPTH):
        big = sched.weights(l)
        mp = _mixer_params(big["conv_a_w"], *[small[k][l] for k in ("conv_a_b", "ln_a_g", "ln_a_b", "ln_v_g", "ln_v_b",
                                                                    "w_s", "b_s")], big["conv_c_w"])
        proj = ride("in_proj", l, _mm_nn, x, big["w_in"], BF16, f"in_proj_{l}")
        cat, ca = ride("mixer_fwd", l, _mixer_fwd, proj, mp, bd, f"mixer_fwd_{l}")
        x1, xh1, rs1, q = _mm_res_ln(cat, big["w_out"], x, row(small["ln1_g"][l]), row(small["ln1_b"][l]),
                                     f"out_proj_ln1_q_{l}", then=big["w_q"])
        kv = _mm_nn(mem, big["w_kv"], BF16, f"kv_proj_{l}")
        o = _attn_fwd(q, kv, bd, f"attn_fwd_{l}")
        x2, xh2, rs2, h = ride("ff1", l, _mm_res_ln, o, big["w_o"], x1, row(small["ln2_g"][l]), row(small["ln2_b"][l]),
                               f"o_proj_ln2_ff1_{l}", then=big["w_ff1"])
        x3, xh3, rs3 = ride("ff2_ln3", l, _mm_res_ln, h, big["w_ff2"], x2, row(small["ln3_g"][l]),
                            row(small["ln3_b"][l]), f"ff2_ln3_{l}", relu2=True)
        saved.append(dict(mp=mp, x=x, proj=proj, cat=cat, ca=ca, x1=x1, xh1=xh1, rs1=rs1, q=q, kv=kv, o=o, x2=x2, xh2=xh2,
                          rs2=rs2, h=h, xh3=xh3, rs3=rs3))
        x = x3

    grads = {k: [None] * DEPTH for k in WEIGHTS}
    s = saved[-1]
    sq, dz3, dg, db = _loss_lnbwd(x, target, s["xh3"], s["rs3"], row(small["ln3_g"][DEPTH - 1]), "loss_ln3_bwd")
    grad_x = None
    causal = jnp.tril(jnp.ones((CHUNK, CHUNK), dtype=bool))
    for l in reversed(range(DEPTH)):
        s = saved[l]
        big = sched.weights(l)
        grads["ln3_g"][l], grads["ln3_b"][l] = jnp.sum(dg, axis=0), jnp.sum(db, axis=0)
        dh = ride("ff2_bwd", l, _mm_nt, dz3, big["w_ff2"], BF16, f"ff2_bwd_{l}", relu2_grad_of=s["h"], tn=D_FF)
        grads["w_ff2"][l] = _mm_tn(s["h"], dz3, f"ff2_wgrad_{l}", relu2=True, t1=2048)
        grads["w_ff1"][l] = _mm_tn_by_owner(s["x2"], dh, f"ff1_wgrad_{l}")
        sched.note_grads(l, {k: grads[k][l] for k in ("w_ff1", "w_ff2")})
        dz2, dg, db, do = _bwd_in(dz3, dh, big["w_ff1"], f"ff1_bwd_ln2_o_bwd_{l}",
                                  ln=(s["xh2"], s["rs2"], row(small["ln2_g"][l])), then=big["w_o"])
        grads["ln2_g"][l], grads["ln2_b"][l] = jnp.sum(dg, axis=0), jnp.sum(db, axis=0)
        grads["w_o"][l] = _mm_tn(s["o"], dz2, f"o_proj_wgrad_{l}")
        dq, dkv = ride("attn_bwd", l, _attn_bwd, s["q"], s["kv"], do, bd, f"attn_bwd_{l}")
        grads["w_q"][l] = _mm_tn(s["x1"], dq, f"q_wgrad_{l}")
        grads["w_kv"][l] = _mm_tn_by_owner(mem, dkv, f"kv_wgrad_{l}")
        sched.note_grads(l, {k: grads[k][l] for k in ("w_o", "w_q", "w_kv")})
        dz1, dg, db, dcat = ride("q_bwd_ln1", l, _bwd_in, dz2, dq, big["w_q"], f"q_bwd_ln1_out_bwd_{l}",
                                 ln=(s["xh1"], s["rs1"], row(small["ln1_g"][l])), then=big["w_out"])
        grads["ln1_g"][l], grads["ln1_b"][l] = jnp.sum(dg, axis=0), jnp.sum(db, axis=0)
        grads["w_out"][l] = _mm_tn(s["cat"], dz1, f"out_proj_wgrad_{l}")
        sched.note_grads(l, {"w_out": grads["w_out"][l]})
        (dproj, dcaw, dcab, dlag, dlab, dlvg, dlvb, dws, dbs, dccw) = ride(
            "mixer_bwd", l, _mixer_bwd, s["proj"], dcat, s["ca"], s["mp"], bd, f"mixer_bwd_{l}")
        grads["conv_a_w"][l] = jnp.sum(dcaw, axis=0)[:KA]
        grads["conv_a_b"][l] = jnp.sum(dcab, axis=(0, 1))
        grads["ln_a_g"][l] = jnp.sum(dlag, axis=(0, 1))
        grads["ln_a_b"][l] = jnp.sum(dlab, axis=(0, 1))
        grads["ln_v_g"][l] = jnp.sum(dlvg, axis=(0, 1))
        grads["ln_v_b"][l] = jnp.sum(dlvb, axis=(0, 1))
        grads["w_s"][l] = jnp.where(causal[None], jnp.sum(dws, axis=0), 0.0)
        grads["b_s"][l] = jnp.sum(dbs, axis=0)[:, :B_HEADS].T
        grads["conv_c_w"][l] = jnp.sum(dccw, axis=0)[:KC]
        grads["w_in"][l] = _mm_tn(s["x"], dproj, f"in_proj_wgrad_{l}")
        sched.note_grads(l, {k: v[l] for k, v in grads.items() if k not in ("w_ff1", "w_ff2")})
        if l > 0:
            p = saved[l - 1]
            dz3, dg, db = _bwd_in(dz1, dproj, big["w_in"], f"in_proj_bwd_ln3_{l}",
                                  ln=(p["xh3"], p["rs3"], row(small["ln3_g"][l - 1])))
        else:
            grad_x = ride("in_proj_bwd", 0, _bwd_in, dz1, dproj, big["w_in"], "in_proj_bwd_0")
    sched.finish()
    return sq, grad_x, grads


WEIGHTS = ("w_in", "conv_a_w", "conv_a_b", "ln_a_g", "ln_a_b", "ln_v_g", "ln_v_b", "w_s", "b_s", "conv_c_w", "w_out",
           "ln1_g", "ln1_b", "w_q", "w_kv", "w_o", "ln2_g", "ln2_b", "w_ff1", "w_ff2", "ln3_g", "ln3_b")
COL_SHARDED = ("w_in", "w_kv", "w_ff1")
ROW_SHARDED = ("w_out", "w_q", "w_o", "w_ff2")
BIG = COL_SHARDED + ROW_SHARDED
REPLICATED = tuple(k for k in WEIGHTS if k not in BIG and k not in ("conv_a_w", "conv_c_w"))
PACK_LANES = 128


CONV_ROWS = 32 + 8
GATHER_LAYER0 = {"first": ("w_in", "conv"), "in_proj": ("w_out", "w_q", "w_kv", "w_o"), "mixer_fwd": ("w_ff1", "w_ff2")}
GATHER_NEXT = {"ff1": ("w_ff2", "w_in", "w_out", "w_q", "w_o", "conv"), "ff2_ln3": ("w_ff1", "w_kv")}
GRADS_EARLY = ("w_ff1", "w_ff2")
GRADS_LATE = ("w_in", "w_kv", "w_out", "w_q", "w_o", "conv")
GRADS_MID0, GRADS_OUT0, GRADS_LAST0 = ("w_kv", "w_q", "w_o"), ("w_out",), ("w_in", "conv")


def _gathered_to_full(g, col_sharded):
    _, a, b = g.shape
    if col_sharded:
        return g.transpose(1, 0, 2).reshape(a, N_DEV * b)
    return g.reshape(N_DEV * a, b)


def _full_to_owner_major(g, col_sharded):
    if g.ndim == 4:
        return g
    a, b = g.shape
    if col_sharded:
        return g.reshape(a, 4, 2, b // N_DEV).transpose(1, 2, 0, 3)
    return g.reshape(4, 2, a // N_DEV, b)


def _conv_pack(conv_a, conv_c):
    pad = lambda a, rows: jnp.pad(a, [(0, 0)] * (a.ndim - 2) + [(0, rows - a.shape[-2]), (0, 0)])
    return jnp.concatenate([pad(conv_a, 32), pad(conv_c, 8)], axis=-2)


def _conv_unpack(packed):
    return packed[..., :KA, :], packed[..., 32:32 + KC, :]


class _Overlapped(_Schedule):
    def __init__(self, shards_bf16, conv_shards, core):
        self.shards, self.conv_shards, self.core = shards_bf16, conv_shards, core
        self.full = {l: {} for l in range(DEPTH)}
        self.grads = {l: {} for l in range(DEPTH)}
        self.owner_major = {}
        self.from_sibling = {}
        self.scattering = None
        self.own, self.remote = {}, {}
        self.replicated = None

    def _gather(self, l, names):
        return _gather_rider([self.conv_shards[l] if k == "conv" else self.shards[k][l] for k in names])

    def _store(self, l, names, gathered):
        for k, g in zip(names, gathered):
            if k == "conv":
                self.full[l]["conv_a_w"], self.full[l]["conv_c_w"] = _conv_unpack(_gathered_to_full(g, True))
            else:
                self.full[l][k] = _gathered_to_full(g, k in COL_SHARDED)

    def weights(self, l):
        if l == 0 and not self.full[0]:
            names = GATHER_LAYER0["first"]
            self._store(0, names, _ride_alone(self._gather(0, names), "weights_all_gather_first"))
        return self.full[l]

    def note_grads(self, l, grads):
        self.grads[l].update(grads)

    def _owner_major(self, l, k):
        if (l, k) not in self.owner_major:
            g = self.grads[l]
            if k == "conv":
                self.owner_major[(l, k)] = _full_to_owner_major(_conv_pack(g["conv_a_w"], g["conv_c_w"]), True)
            else:
                self.owner_major[(l, k)] = _full_to_owner_major(g[k], k in COL_SHARDED)
        return self.owner_major[(l, k)]

    def _exchange(self, l, names):
        return _exchange_rider([self._owner_major(l, k) for k in names])

    def _scatter(self, groups):
        sums, sums_bf16, self.scattering = [], [], []
        for l, names in groups:
            for k, r in zip(names, self.from_sibling.pop((l, names))):
                s, sb = _pair_add(self._owner_major(l, k), r, self.core, f"grad_pair_add_{l}_{k}")
                sums.append(s)
                sums_bf16.append(sb)
                self.scattering.append((l, k))
        return _scatter_rider(sums, sums_bf16)

    def _scattered(self, results):
        n = len(self.scattering)
        for i, key in enumerate(self.scattering):
            self.own[key], self.remote[key] = results[i], results[n + i]

    def rider(self, stage, l):
        if l == 0 and stage in ("in_proj", "mixer_fwd"):
            return self._gather(0, GATHER_LAYER0[stage])
        if stage in GATHER_NEXT and l + 1 < DEPTH:
            return self._gather(l + 1, GATHER_NEXT[stage])
        if stage == "ff2_bwd" and l + 1 < DEPTH:
            return self._exchange(l + 1, GRADS_LATE)
        if stage == "attn_bwd":
            return self._exchange(l, GRADS_EARLY)
        if stage == "q_bwd_ln1" and l == 0:
            return self._exchange(0, GRADS_MID0)
        if stage == "mixer_bwd" and l == 0:
            return _join_riders([self._scatter([(0, GRADS_EARLY), (1, GRADS_LATE), (0, GRADS_MID0)]),
                                 self._exchange(0, GRADS_OUT0)])
        if stage == "mixer_bwd":
            return self._scatter([(l, GRADS_EARLY)] + ([(l + 1, GRADS_LATE)] if l + 1 < DEPTH else []))
        if stage == "in_proj_bwd":
            packed = _pack_rows([jnp.stack([self.grads[i][k] for i in range(DEPTH)]) for k in REPLICATED])
            return _gather_rider([packed])
        return None

    def rode(self, stage, l, results):
        if l == 0 and stage in ("in_proj", "mixer_fwd"):
            self._store(0, GATHER_LAYER0[stage], results)
        elif stage in GATHER_NEXT:
            self._store(l + 1, GATHER_NEXT[stage], results)
        elif stage == "ff2_bwd":
            self.from_sibling[(l + 1, GRADS_LATE)] = results
        elif stage == "attn_bwd":
            self.from_sibling[(l, GRADS_EARLY)] = results
        elif stage == "q_bwd_ln1":
            self.from_sibling[(0, GRADS_MID0)] = results
        elif stage == "mixer_bwd" and l == 0:
            n = 2 * len(self.scattering)
            self._scattered(results[:n])
            self.from_sibling[(0, GRADS_OUT0)] = results[n:]
        elif stage == "mixer_bwd":
            self._scattered(results)
        elif stage == "in_proj_bwd":
            self.replicated = results[0]

    def finish(self):
        self.from_sibling[(0, GRADS_LAST0)] = _ride_alone(self._exchange(0, GRADS_LAST0), "grad_pair_exchange_last")
        self._scattered(_ride_alone(self._scatter([(0, GRADS_OUT0), (0, GRADS_LAST0)]), "grad_chip_scatter_last"))


def _pack_rows(parts):
    flat = jnp.concatenate([p.reshape(-1, PACK_LANES) for p in parts], axis=0)
    return jnp.pad(flat, ((0, -flat.shape[0] % 8), (0, 0)))


def _unpack_rows(packed, like):
    out, r = [], 0
    for p in like:
        n = p.size // PACK_LANES
        out.append(packed[r:r + n].reshape(p.shape))
        r += n
    return out


def kernel(x, mem, w_in, conv_a_w, conv_a_b, ln_a_g, ln_a_b, ln_v_g, ln_v_b, w_s, b_s, conv_c_w, w_out, ln1_g, ln1_b, w_q, w_kv, w_o, ln2_g, ln2_b, w_ff1, w_ff2, ln3_g, ln3_b, loss_target, m_w_in, m_conv_a_w, m_conv_a_b, m_ln_a_g, m_ln_a_b, m_ln_v_g, m_ln_v_b, m_w_s, m_b_s, m_conv_c_w, m_w_out, m_ln1_g, m_ln1_b, m_w_q, m_w_kv, m_w_o, m_ln2_g, m_ln2_b, m_w_ff1, m_w_ff2, m_ln3_g, m_ln3_b, v_w_in, v_conv_a_w, v_conv_a_b, v_ln_a_g, v_ln_a_b, v_ln_v_g, v_ln_v_b, v_w_s, v_b_s, v_conv_c_w, v_w_out, v_ln1_g, v_ln1_b, v_w_q, v_w_kv, v_w_o, v_ln2_g, v_ln2_b, v_w_ff1, v_w_ff2, v_ln3_g, v_ln3_b):
    given = dict(locals())
    w = {k: given[k] for k in WEIGHTS}
    mom = {k: given["m_" + k] for k in WEIGHTS}
    var = {k: given["v_" + k] for k in WEIGHTS}
    bd, s_len, _ = x.shape
    core = lax.axis_index("c").astype(jnp.int32).reshape(1)

    conv_pack = lambda d: _conv_pack(d["conv_a_w"], d["conv_c_w"])
    sched = _Overlapped({k: w[k].astype(BF16) for k in BIG}, conv_pack(w), core)
    sq, grad_x, grads = _local_step(x.reshape(bd * s_len, D), mem.reshape(-1, D), loss_target.reshape(bd * s_len, D),
                                    sched, {k: w[k] for k in REPLICATED}, bd)
    loss = lax.psum(0.5 * jnp.sum(sq) / D, ("x", "y", "c"))

    out = {}
    for k in BIG + ("conv",):
        own = [sched.own[(l, k)] for l in range(DEPTH)]
        remote = [sched.remote[(l, k)] for l in range(DEPTH)]
        if k == "conv":
            conv_out = _adamw_layers(conv_pack(w), conv_pack(mom), conv_pack(var), own, remote, "adamw_conv")
            unpacked = [_conv_unpack(o) for o in conv_out]
            out["conv_a_w"], out["conv_c_w"] = [u[0] for u in unpacked], [u[1] for u in unpacked]
        else:
            out[k] = _adamw_layers(w[k], mom[k], var[k], own, remote, f"adamw_{k}")

    rep_out = _adamw(_pack_rows([w[k] for k in REPLICATED]), _pack_rows([mom[k] for k in REPLICATED]),
                     _pack_rows([var[k] for k in REPLICATED]), [sched.replicated], "adamw_replicated")
    for i, o in enumerate(rep_out):
        for k, piece in zip(REPLICATED, _unpack_rows(o, [w[k] for k in REPLICATED])):
            out.setdefault(k, [None] * 4)[i] = piece

    res = [loss, grad_x.reshape(bd, s_len, D)]
    for i in range(4):
        res += [out[k][i] for k in WEIGHTS]
    return tuple(res)
```

```python
import functools
import math

import jax
import jax.numpy as jnp
from jax import lax
from jax.experimental import pallas as pl
from jax.experimental.pallas import tpu as pltpu

F32 = jnp.float32
BF16 = jnp.bfloat16

DEPTH = 4
D = 1024
D_A, D_B, D_C = 384, 256, 384
HEAD = 64
B_HEADS = 4
CHUNK = 128
KA, KC = 31, 3
HALO_A, HALO_C = 32, 8
IN_W = 2 * D_A + 2 * D_B + 3 * D_C
X_HEADS = 4
X_HD = D // X_HEADS
D_FF = 4 * D
EPS = 1e-5
ALPHA = (2.0 * DEPTH) ** 0.25
LR, B1, B2, ADAM_EPS, WD, STEP = 0.001, 0.9, 0.999, 1e-08, 0.01, 10
INV_SQRT2 = 0.7071067811865476
INV_SQRT_2PI = 0.3989422804014327
N_DEV = 8
VMEM_LIMIT = 56 * 1024 * 1024
MESH = pl.DeviceIdType.MESH
ANY = pl.BlockSpec(memory_space=pl.ANY)


def _cp(sem=None):
    return pltpu.CompilerParams(dimension_semantics=sem, vmem_limit_bytes=VMEM_LIMIT)


def _sds(shape, dtype):
    return jax.ShapeDtypeStruct(tuple(shape), dtype)


class _Rider:
    def __init__(self, arrays, out_shape, sems, start, finish):
        self.arrays, self.out_shape, self.sems, self.start, self.finish = arrays, out_shape, sems, start, finish


def _call(kern, name, grid, in_specs, out_specs, out_shape, args, sem, scratch=(), rider=None):
    single = not isinstance(out_shape, (list, tuple))
    out_specs_l = [out_specs] if single else list(out_specs)
    out_shape_l = [out_shape] if single else list(out_shape)
    if rider is None:
        res = pl.pallas_call(kern, name=name, grid=grid, in_specs=in_specs, out_specs=out_specs_l,
                             out_shape=out_shape_l, scratch_shapes=list(scratch), compiler_params=_cp(sem))(*args)
        return (res[0] if single else list(res)), None
    n_in, n_out, n_scr = len(args), len(out_shape_l), len(scratch)
    n_rin, n_rout = len(rider.arrays), len(rider.out_shape)

    def body(*refs):
        ins, refs = refs[:n_in], refs[n_in:]
        r_ins, refs = refs[:n_rin], refs[n_rin:]
        outs, refs = refs[:n_out], refs[n_out:]
        r_outs, refs = refs[:n_rout], refs[n_rout:]
        scr, r_sems = refs[:n_scr], refs[n_scr:]
        ids = [pl.program_id(d) for d in range(len(grid))]
        first = functools.reduce(jnp.logical_and, [i == 0 for i in ids])
        last = functools.reduce(jnp.logical_and, [i == g - 1 for i, g in zip(ids, grid)])

        @pl.when(first)
        def _():
            rider.start(r_ins, r_outs, r_sems)

        kern(*ins, *outs, *scr)

        @pl.when(last)
        def _():
            rider.finish(r_ins, r_outs, r_sems)

    res = pl.pallas_call(
        body, name=name, grid=grid, in_specs=list(in_specs) + [ANY] * n_rin,
        out_specs=out_specs_l + [ANY] * n_rout, out_shape=out_shape_l + list(rider.out_shape),
        scratch_shapes=list(scratch) + list(rider.sems),
        compiler_params=_cp(("arbitrary",) * len(grid)))(*args, *rider.arrays)
    mine, theirs = list(res[:n_out]), list(res[n_out:])
    return (mine[0] if single else mine), theirs


def _join_riders(riders):
    if len(riders) == 1:
        return riders[0]

    def parts(seq, attr):
        out, at = [], 0
        for r in riders:
            n = len(getattr(r, attr))
            out.append(seq[at:at + n])
            at += n
        return out

    def each(method):
        def run(ins, outs, sems):
            for r, i, o, s in zip(riders, parts(ins, "arrays"), parts(outs, "out_shape"), parts(sems, "sems")):
                getattr(r, method)(i, o, s)
        return run

    return _Rider(sum([r.arrays for r in riders], []), sum([r.out_shape for r in riders], []),
                  sum([r.sems for r in riders], []), each("start"), each("finish"))


def _ride_alone(rider, name):
    def body(*refs):
        n_rin, n_rout = len(rider.arrays), len(rider.out_shape)
        r_ins, r_outs, r_sems = refs[:n_rin], refs[n_rin:n_rin + n_rout], refs[n_rin + n_rout:]
        rider.start(r_ins, r_outs, r_sems)
        rider.finish(r_ins, r_outs, r_sems)

    return list(pl.pallas_call(
        body, name=name, in_specs=[ANY] * len(rider.arrays), out_specs=[ANY] * len(rider.out_shape),
        out_shape=list(rider.out_shape), scratch_shapes=list(rider.sems))(*rider.arrays))


def _ln(z):
    mu = jnp.mean(z, axis=-1, keepdims=True)
    zc = z - mu
    var = jnp.mean(zc * zc, axis=-1, keepdims=True)
    rstd = lax.rsqrt(var + EPS)
    return zc * rstd, rstd


def _ln_bwd(dxhat, xhat, rstd):
    m1 = jnp.mean(dxhat, axis=-1, keepdims=True)
    m2 = jnp.mean(dxhat * xhat, axis=-1, keepdims=True)
    return rstd * (dxhat - m1 - xhat * m2)


def _gelu(x):
    return 0.5 * x * (1.0 + lax.erf(x * INV_SQRT2))


def _gelu_and_grad(x):
    cdf = 0.5 * (1.0 + lax.erf(x * INV_SQRT2))
    return x * cdf, cdf + x * jnp.exp(-0.5 * x * x) * INV_SQRT_2PI


def _fold8(x):
    r, c = x.shape
    return jnp.sum(x.reshape(r // 8, 8, c), axis=0)


def _relu2(h):
    return jnp.square(jnp.maximum(h, 0.0))


def _weight_spec(block, index_map, resident):
    return pl.BlockSpec(block, index_map, pipeline_mode=pl.Buffered(1) if resident else None)


def _mm_nn(a, w, out_dtype, name, tm=512, tn=None, rider=None):
    n, k = a.shape
    m = w.shape[1]
    tm = min(tm, n)
    tn = m if tn is None else min(tn, m)

    def kern(a_ref, w_ref, o_ref):
        o_ref[...] = jnp.dot(a_ref[...].astype(BF16), w_ref[...], preferred_element_type=F32).astype(out_dtype)

    res, rode = _call(
        kern, name, (n // tm, m // tn),
        [pl.BlockSpec((tm, k), lambda i, j: (i, 0)), _weight_spec((k, tn), lambda i, j: (0, j), tn == m)],
        pl.BlockSpec((tm, tn), lambda i, j: (i, j)), _sds((n, m), out_dtype), (a, w), ("parallel", "parallel"),
        rider=rider)
    return res if rider is None else (res, rode)


def _mm_res_ln(a, w, res, g, b, name, relu2=False, tm=512, rider=None, then=None):
    n, k = a.shape
    tm = min(tm, n)

    m2 = None if then is None else then.shape[1]

    def kern(*refs):
        a_ref, w_ref, res_ref, g_ref, b_ref = refs[:5]
        x_ref, xhat_ref, rstd_ref = refs[-3:] if then is None else refs[-4:-1]
        av = a_ref[...]
        if relu2:
            av = _relu2(av.astype(F32))
        z = ALPHA * res_ref[...] + jnp.dot(av.astype(BF16), w_ref[...], preferred_element_type=F32)
        xhat, rstd = _ln(z)
        xhat_ref[...] = xhat
        rstd_ref[...] = rstd
        x = xhat * g_ref[...] + b_ref[...]
        x_ref[...] = x
        if then is not None:
            refs[-1][...] = jnp.dot(x.astype(BF16), refs[5][...], preferred_element_type=F32).astype(BF16)

    row = lambda i: (i, 0)
    fix = lambda i: (0, 0)
    in_specs = [pl.BlockSpec((tm, k), row), _weight_spec((k, D), fix, True), pl.BlockSpec((tm, D), row),
                pl.BlockSpec((1, D), fix), pl.BlockSpec((1, D), fix)]
    out_specs = [pl.BlockSpec((tm, D), row), pl.BlockSpec((tm, D), row), pl.BlockSpec((tm, 1), row)]
    out_shape = [_sds((n, D), F32), _sds((n, D), F32), _sds((n, 1), F32)]
    args = (a, w, res, g, b)
    if then is not None:
        in_specs.append(_weight_spec((D, m2), fix, True))
        out_specs.append(pl.BlockSpec((tm, m2), row))
        out_shape.append(_sds((n, m2), BF16))
        args += (then,)
    out, rode = _call(kern, name, (n // tm,), in_specs, out_specs, out_shape, args, ("parallel",), rider=rider)
    return out if rider is None else (out, rode)


def _mm_nt(a, w, out_dtype, name, relu2_grad_of=None, tm=512, tn=1024, rider=None):
    n, k = a.shape
    m = w.shape[0]
    tm = min(tm, n)
    tn = min(tn, m)
    with_h = relu2_grad_of is not None

    def kern(*refs):
        a_ref, w_ref = refs[0], refs[1]
        o_ref = refs[-1]
        r = lax.dot_general(a_ref[...].astype(BF16), w_ref[...], (((1,), (1,)), ((), ())), preferred_element_type=F32)
        if with_h:
            r = r * (2.0 * jnp.maximum(refs[2][...].astype(F32), 0.0))
        o_ref[...] = r.astype(out_dtype)

    in_specs = [pl.BlockSpec((tm, k), lambda i, j: (i, 0)), _weight_spec((tn, k), lambda i, j: (j, 0), tn == m)]
    args = [a, w]
    if with_h:
        in_specs.append(pl.BlockSpec((tm, tn), lambda i, j: (i, j)))
        args.append(relu2_grad_of)
    res, rode = _call(kern, name, (n // tm, m // tn), in_specs, pl.BlockSpec((tm, tn), lambda i, j: (i, j)),
                      _sds((n, m), out_dtype), args, ("parallel", "parallel"), rider=rider)
    return res if rider is None else (res, rode)


def _mm_tn_by_owner(a, b, name, tk=512):
    n, k1 = a.shape
    m = b.shape[1]
    s = m // N_DEV
    tk = min(tk, n)

    def kern(a_ref, b_ref, o_ref):
        @pl.when(pl.program_id(0) == 0)
        def _():
            o_ref[...] = jnp.zeros_like(o_ref)

        av = a_ref[...].astype(BF16)
        for j in range(N_DEV):
            o_ref[j // 2, j % 2] += lax.dot_general(av, b_ref[:, j * s:(j + 1) * s].astype(BF16),
                                                    (((0,), (0,)), ((), ())), preferred_element_type=F32)

    return pl.pallas_call(
        kern, name=name, grid=(n // tk,),
        in_specs=[pl.BlockSpec((tk, k1), lambda k: (k, 0)), pl.BlockSpec((tk, m), lambda k: (k, 0))],
        out_specs=pl.BlockSpec((4, 2, k1, s), lambda k: (0, 0, 0, 0), pipeline_mode=pl.Buffered(1)),
        out_shape=_sds((4, 2, k1, s), F32), compiler_params=_cp(("arbitrary",)))(a, b)


def _mm_tn(a, b, name, relu2=False, t1=1024, tn=2048, tk=512):
    n, k1 = a.shape
    m = b.shape[1]
    t1 = min(t1, k1)
    tn = m if m <= 2432 and m % tn else min(tn, m)
    tk = min(tk, n)

    def kern(a_ref, b_ref, o_ref):
        @pl.when(pl.program_id(2) == 0)
        def _():
            o_ref[...] = jnp.zeros_like(o_ref)

        av = a_ref[...]
        if relu2:
            av = _relu2(av.astype(F32))
        o_ref[...] += lax.dot_general(av.astype(BF16), b_ref[...].astype(BF16), (((0,), (0,)), ((), ())),
                                      preferred_element_type=F32)

    return pl.pallas_call(
        kern, name=name, grid=(k1 // t1, m // tn, n // tk),
        in_specs=[pl.BlockSpec((tk, t1), lambda i, j, k: (k, i)), pl.BlockSpec((tk, tn), lambda i, j, k: (k, j))],
        out_specs=pl.BlockSpec((t1, tn), lambda i, j, k: (i, j)),
        out_shape=_sds((k1, m), F32),
        compiler_params=_cp(("parallel", "parallel", "arbitrary")))(a, b)


def _bwd_in(dz_next, da, w, name, ln=None, tm=512, rider=None, then=None):
    n, k2 = da.shape
    tm = min(tm, n)
    row = lambda i: (i, 0)
    fix = lambda i: (0, 0)

    def dx_of(dzn_ref, da_ref, w_ref):
        return ALPHA * dzn_ref[...] + lax.dot_general(da_ref[...], w_ref[...], (((1,), (1,)), ((), ())),
                                                      preferred_element_type=F32)

    base_specs = [pl.BlockSpec((tm, D), row), pl.BlockSpec((tm, k2), row), _weight_spec((D, k2), fix, True)]
    if ln is None:
        def kern(dzn_ref, da_ref, w_ref, dx_ref):
            dx_ref[...] = dx_of(dzn_ref, da_ref, w_ref)

        out, rode = _call(kern, name, (n // tm,), base_specs, pl.BlockSpec((tm, D), row), _sds((n, D), F32),
                          (dz_next, da, w), ("parallel",), rider=rider)
        return out if rider is None else (out, rode)

    xhat, rstd, g = ln

    def kern(*refs):
        dzn_ref, da_ref, w_ref, xhat_ref, rstd_ref, g_ref = refs[:6]
        dz_ref, dg_ref, db_ref = refs[-3:] if then is None else refs[-4:-1]

        @pl.when(pl.program_id(0) == 0)
        def _():
            dg_ref[...] = jnp.zeros_like(dg_ref)
            db_ref[...] = jnp.zeros_like(db_ref)

        dx = dx_of(dzn_ref, da_ref, w_ref)
        xh = xhat_ref[...]
        dg_ref[...] += _fold8(dx * xh)
        db_ref[...] += _fold8(dx)
        dz = _ln_bwd(dx * g_ref[...], xh, rstd_ref[...])
        dz_ref[...] = dz
        if then is not None:
            refs[-1][...] = lax.dot_general(dz.astype(BF16), refs[6][...], (((1,), (1,)), ((), ())),
                                            preferred_element_type=F32).astype(BF16)

    in_specs = base_specs + [pl.BlockSpec((tm, D), row), pl.BlockSpec((tm, 1), row), pl.BlockSpec((1, D), fix)]
    out_specs = [pl.BlockSpec((tm, D), row), pl.BlockSpec((8, D), fix), pl.BlockSpec((8, D), fix)]
    out_shape = [_sds((n, D), F32), _sds((8, D), F32), _sds((8, D), F32)]
    args = (dz_next, da, w, xhat, rstd, g)
    if then is not None:
        m3 = then.shape[0]
        in_specs.append(_weight_spec((m3, D), fix, True))
        out_specs.append(pl.BlockSpec((tm, m3), row))
        out_shape.append(_sds((n, m3), BF16))
        args += (then,)
    out, rode = _call(kern, name, (n // tm,), in_specs, out_specs, out_shape, args, ("arbitrary",), rider=rider)
    return out if rider is None else (out, rode)


def _loss_lnbwd(x, target, xhat, rstd, g, name, tm=512):
    n = x.shape[0]
    tm = min(tm, n)
    row = lambda i: (i, 0)
    fix = lambda i: (0, 0)

    def kern(x_ref, t_ref, xhat_ref, rstd_ref, g_ref, sq_ref, dz_ref, dg_ref, db_ref):
        @pl.when(pl.program_id(0) == 0)
        def _():
            sq_ref[...] = jnp.zeros_like(sq_ref)
            dg_ref[...] = jnp.zeros_like(dg_ref)
            db_ref[...] = jnp.zeros_like(db_ref)

        err = x_ref[...] - t_ref[...]
        sq_ref[...] += _fold8(err * err)
        dx = err * (1.0 / D)
        xh = xhat_ref[...]
        dg_ref[...] += _fold8(dx * xh)
        db_ref[...] += _fold8(dx)
        dz_ref[...] = _ln_bwd(dx * g_ref[...], xh, rstd_ref[...])

    return pl.pallas_call(
        kern, name=name, grid=(n // tm,),
        in_specs=[pl.BlockSpec((tm, D), row), pl.BlockSpec((tm, D), row), pl.BlockSpec((tm, D), row),
                  pl.BlockSpec((tm, 1), row), pl.BlockSpec((1, D), fix)],
        out_specs=[pl.BlockSpec((8, D), fix), pl.BlockSpec((tm, D), row), pl.BlockSpec((8, D), fix),
                   pl.BlockSpec((8, D), fix)],
        out_shape=[_sds((8, D), F32), _sds((n, D), F32), _sds((8, D), F32), _sds((8, D), F32)],
        compiler_params=_cp(("arbitrary",)))(x, target, xhat, rstd, g)


def _softmax_rows(s):
    s = s - jnp.max(s, axis=-1, keepdims=True)
    e = jnp.exp(s)
    return e / jnp.sum(e, axis=-1, keepdims=True)


def _attn_fwd(q, kv, bd, name, tm=512):
    n = q.shape[0]
    s_len = n // bd
    m_len = kv.shape[0] // bd
    tm = min(tm, s_len)
    nt = s_len // tm
    scale = X_HD ** -0.5

    def kern(q_ref, k_ref, v_ref, o_ref):
        for h in range(X_HEADS):
            cs = slice(h * X_HD, (h + 1) * X_HD)
            s = lax.dot_general(q_ref[:, cs], k_ref[:, cs], (((1,), (1,)), ((), ())), preferred_element_type=F32)
            p = _softmax_rows(s * scale)
            o_ref[:, cs] = jnp.dot(p.astype(BF16), v_ref[:, cs], preferred_element_type=F32).astype(BF16)

    return pl.pallas_call(
        kern, name=name, grid=(bd, nt),
        in_specs=[pl.BlockSpec((tm, D), lambda b, i: (b * nt + i, 0)),
                  pl.BlockSpec((m_len, D), lambda b, i: (b, 0)), pl.BlockSpec((m_len, D), lambda b, i: (b, 1))],
        out_specs=pl.BlockSpec((tm, D), lambda b, i: (b * nt + i, 0)),
        out_shape=_sds((n, D), BF16),
        compiler_params=_cp(("parallel", "parallel")))(q, kv, kv)


def _attn_bwd(q, kv, do, bd, name, tm=512, rider=None):
    n = q.shape[0]
    s_len = n // bd
    m_len = kv.shape[0] // bd
    tm = min(tm, s_len)
    nt = s_len // tm
    scale = X_HD ** -0.5

    def kern(q_ref, k_ref, v_ref, do_ref, dq_ref, dkv_ref):
        @pl.when(pl.program_id(1) == 0)
        def _():
            dkv_ref[...] = jnp.zeros_like(dkv_ref)

        for h in range(X_HEADS):
            cs = slice(h * X_HD, (h + 1) * X_HD)
            vs = slice(D + h * X_HD, D + (h + 1) * X_HD)
            qh, kh, vh, doh = q_ref[:, cs], k_ref[:, cs], v_ref[:, cs], do_ref[:, cs]
            s = lax.dot_general(qh, kh, (((1,), (1,)), ((), ())), preferred_element_type=F32)
            p = _softmax_rows(s * scale)
            pb = p.astype(BF16)
            dp = lax.dot_general(doh, vh, (((1,), (1,)), ((), ())), preferred_element_type=F32)
            dkv_ref[:, vs] += lax.dot_general(pb, doh, (((0,), (0,)), ((), ())), preferred_element_type=F32)
            ds = (p * (dp - jnp.sum(dp * p, axis=-1, keepdims=True)) * scale).astype(BF16)
            dq_ref[:, cs] = jnp.dot(ds, kh, preferred_element_type=F32).astype(BF16)
            dkv_ref[:, cs] += lax.dot_general(ds, qh, (((0,), (0,)), ((), ())), preferred_element_type=F32)

    out, rode = _call(
        kern, name, (bd, nt),
        [pl.BlockSpec((tm, D), lambda b, i: (b * nt + i, 0)),
         pl.BlockSpec((m_len, D), lambda b, i: (b, 0)), pl.BlockSpec((m_len, D), lambda b, i: (b, 1)),
         pl.BlockSpec((tm, D), lambda b, i: (b * nt + i, 0))],
        [pl.BlockSpec((tm, D), lambda b, i: (b * nt + i, 0)), pl.BlockSpec((m_len, 2 * D), lambda b, i: (b, 0))],
        [_sds((n, D), BF16), _sds((bd * m_len, 2 * D), F32)], (q, kv, kv, do), ("parallel", "arbitrary"), rider=rider)
    return out if rider is None else (out, rode)


C_AV, C_AG, C_BU, C_BV, C_CB, C_CC, C_CX = 0, 384, 768, 1024, 1280, 1664, 2048


def _taps_by_phase(offsets):
    by_phase = {}
    for k, o in enumerate(offsets):
        by_phase.setdefault(o % 8, []).append((o // 8, k))
    return sorted(by_phase.items())


def _window_rows(win_ref, r0, a, rows, cs):
    return win_ref[pl.ds(pl.multiple_of(r0 + 8 * a, 8), rows), cs]


def _conv_taps(win_ref, r0, w_ref, offsets):
    parts = []
    for cb in range(3):
        cs = slice(cb * 128, (cb + 1) * 128)
        acc = jnp.zeros((CHUNK, 128), F32)
        for b, taps in _taps_by_phase(offsets):
            rows = CHUNK if b == 0 else CHUNK + 8
            part = jnp.zeros((rows, 128), F32)
            for a, k in taps:
                part = part + _window_rows(win_ref, r0, a, rows, cs) * w_ref[k:k + 1, cs]
            acc = acc + (part if b == 0 else part[b:b + CHUNK, :])
        parts.append(acc)
    return jnp.concatenate(parts, axis=1)


def _tap_grads(acc_ref, win_ref, r0, d, offsets, shifted_ref):
    for cb in range(3):
        cs = slice(cb * 128, (cb + 1) * 128)
        padded = jnp.concatenate([jnp.zeros((8, 128), F32), d[:, cs], jnp.zeros((8, 128), F32)], axis=0)
        for b, taps in _taps_by_phase(offsets):
            if b == 0:
                rows, db = CHUNK, d[:, cs]
            else:
                rows = CHUNK + 8
                shifted_ref[...] = padded[8 - b:8 - b + rows, :]
                db = shifted_ref[...]
            for a, k in taps:
                acc_ref[k * 8:(k + 1) * 8, cs] += _fold8(db * _window_rows(win_ref, r0, a, rows, cs))


def _causal_offsets(halo, n_taps):
    return [halo - (n_taps - 1) + k for k in range(n_taps)]


def _anticausal_offsets(n_taps):
    return [n_taps - 1 - k for k in range(n_taps)]


def _head_of_lane():
    return lax.broadcasted_iota(jnp.int32, (1, D_B), 1) // HEAD


def _spatial_mix(wm_ref, vb, head):
    mixed = jnp.zeros((CHUNK, D_B), F32)
    for h in range(B_HEADS):
        mh = jnp.dot(wm_ref[h], vb, preferred_element_type=F32)
        mixed = jnp.where(head == h, mh, mixed)
    return mixed


def _mixer_fwd(proj, p, bd, name, rider=None):
    n = proj.shape[0]
    s_len = n // bd
    n_chunks = s_len // CHUNK

    def kern(proj_ref, caw_ref, cab_ref, lag_ref, lab_ref, lvg_ref, lvb_ref, wm_ref, bsx_ref, ccw_ref, cat_ref,
             ca_ref, gs_ref, ccs_ref):
        gs_ref[0:HALO_A, :] = jnp.zeros((HALO_A, D_A), F32)
        ccs_ref[0:HALO_C, :] = jnp.zeros((HALO_C, D_C), F32)
        head = _head_of_lane()

        def chunk(i, carry):
            r0 = pl.multiple_of(i * CHUNK, CHUNK)
            rows = pl.ds(r0, CHUNK)
            ld = lambda c0, w: proj_ref[rows, c0:c0 + w].astype(F32)
            gs_ref[pl.ds(r0 + HALO_A, CHUNK), :] = ld(C_AV, D_A) * jax.nn.sigmoid(ld(C_AG, D_A))
            ca = _conv_taps(gs_ref, r0, caw_ref, _causal_offsets(HALO_A, KA)) + cab_ref[...]
            ca_ref[rows, :] = ca
            lna = _ln(ca)[0] * lag_ref[...] + lab_ref[...]
            cat_ref[rows, 0:D_A] = (lna * jax.nn.sigmoid(lna)).astype(BF16)
            u = _gelu(ld(C_BU, D_B))
            v = _ln(_gelu(ld(C_BV, D_B)))[0] * lvg_ref[...] + lvb_ref[...]
            mixed = _spatial_mix(wm_ref, v.astype(BF16), head) + bsx_ref[...]
            cat_ref[rows, D_A:D_A + D_B] = (u * mixed).astype(BF16)
            ccs_ref[pl.ds(r0 + HALO_C, CHUNK), :] = ld(C_CC, D_C) * ld(C_CX, D_C)
            conv = _conv_taps(ccs_ref, r0, ccw_ref, _causal_offsets(HALO_C, KC))
            cat_ref[rows, D_A + D_B:D] = (ld(C_CB, D_C) * conv).astype(BF16)
            return carry

        lax.fori_loop(0, n_chunks, chunk, 0)

    fix2 = lambda b: (0, 0)
    args = [proj, p["caw"], p["cab"], p["lag"], p["lab"], p["lvg"], p["lvb"], p["wm"], p["bsx"], p["ccw"]]
    in_specs = [pl.BlockSpec((s_len, IN_W), lambda b: (b, 0))]
    for a in args[1:]:
        in_specs.append(pl.BlockSpec(a.shape, (lambda b: (0, 0, 0)) if a.ndim == 3 else fix2))
    res, rode = _call(
        kern, name, (bd,), in_specs,
        [pl.BlockSpec((s_len, D), lambda b: (b, 0)), pl.BlockSpec((s_len, D_A), lambda b: (b, 0))],
        [_sds((n, D), BF16), _sds((n, D_A), F32)], args, ("parallel",),
        scratch=[pltpu.VMEM((s_len + HALO_A, D_A), F32), pltpu.VMEM((s_len + HALO_C, D_C), F32)], rider=rider)
    return res if rider is None else (res, rode)


def _mixer_bwd(proj, dcat, ca, p, bd, name, rider=None):
    n = proj.shape[0]
    s_len = n // bd
    n_chunks = s_len // CHUNK

    def kern(proj_ref, dcat_ref, ca_ref, caw_ref, cab_ref, lag_ref, lab_ref, lvg_ref, lvb_ref, wm_ref, bsx_ref, ccw_ref,
             dproj_ref, dcaw_ref, dcab_ref, dlag_ref, dlab_ref, dlvg_ref, dlvb_ref, dws_ref, dbs_ref, dccw_ref,
             gs_ref, dcas_ref, ccs_ref, dcs_ref, a_caw, a_cab, a_lag, a_lab, a_lvg, a_lvb, a_ccw, shifted_ref):
        gs_ref[0:HALO_A, :] = jnp.zeros((HALO_A, D_A), F32)
        ccs_ref[0:HALO_C, :] = jnp.zeros((HALO_C, D_C), F32)
        dcas_ref[s_len:s_len + HALO_A, :] = jnp.zeros((HALO_A, D_A), F32)
        dcs_ref[s_len:s_len + HALO_C, :] = jnp.zeros((HALO_C, D_C), F32)
        for acc in (a_caw, a_cab, a_lag, a_lab, a_lvg, a_lvb, a_ccw, dws_ref, dbs_ref):
            acc[...] = jnp.zeros_like(acc)
        head = _head_of_lane()
        lane128 = lax.broadcasted_iota(jnp.int32, (1, CHUNK), 1)

        def pass1(i, carry):
            r0 = pl.multiple_of(i * CHUNK, CHUNK)
            rows = pl.ds(r0, CHUNK)
            ld = lambda c0, w: proj_ref[rows, c0:c0 + w].astype(F32)
            dld = lambda c0, w: dcat_ref[rows, c0:c0 + w].astype(F32)
            gs_ref[pl.ds(r0 + HALO_A, CHUNK), :] = ld(C_AV, D_A) * jax.nn.sigmoid(ld(C_AG, D_A))
            xh, rstd = _ln(ca_ref[rows, :])
            lna = xh * lag_ref[...] + lab_ref[...]
            sg = jax.nn.sigmoid(lna)
            dlna = dld(0, D_A) * (sg * (1.0 + lna * (1.0 - sg)))
            a_lag[...] += _fold8(dlna * xh)
            a_lab[...] += _fold8(dlna)
            dca = _ln_bwd(dlna * lag_ref[...], xh, rstd)
            dcas_ref[rows, :] = dca
            a_cab[...] += _fold8(dca)
            _tap_grads(a_caw, gs_ref, r0, dca, _causal_offsets(HALO_A, KA), shifted_ref)
            pu, pv = ld(C_BU, D_B), ld(C_BV, D_B)
            u, du_dpu = _gelu_and_grad(pu)
            gv, dgv_dpv = _gelu_and_grad(pv)
            vxh, vrstd = _ln(gv)
            v = vxh * lvg_ref[...] + lvb_ref[...]
            vb = v.astype(BF16)
            mixed = _spatial_mix(wm_ref, vb, head) + bsx_ref[...]
            dbo = dld(D_A, D_B)
            dproj_ref[rows, C_BU:C_BU + D_B] = (dbo * mixed * du_dpu).astype(BF16)
            dmixed = dbo * u
            dv = jnp.zeros((CHUNK, D_B), F32)
            bsum = jnp.zeros((CHUNK, CHUNK), F32)
            for h in range(B_HEADS):
                dmh = jnp.where(head == h, dmixed, 0.0)
                dmb = dmh.astype(BF16)
                dvh = lax.dot_general(wm_ref[h], dmb, (((0,), (0,)), ((), ())), preferred_element_type=F32)
                dv = jnp.where(head == h, dvh, dv)
                dws_ref[h] += lax.dot_general(dmb, vb, (((1,), (1,)), ((), ())), preferred_element_type=F32)
                bsum = bsum + jnp.where(lane128 == h, jnp.sum(dmh, axis=-1, keepdims=True), 0.0)
            dbs_ref[...] += bsum
            a_lvg[...] += _fold8(dv * vxh)
            a_lvb[...] += _fold8(dv)
            dgv = _ln_bwd(dv * lvg_ref[...], vxh, vrstd)
            dproj_ref[rows, C_BV:C_BV + D_B] = (dgv * dgv_dpv).astype(BF16)
            ccs_ref[pl.ds(r0 + HALO_C, CHUNK), :] = ld(C_CC, D_C) * ld(C_CX, D_C)
            conv = _conv_taps(ccs_ref, r0, ccw_ref, _causal_offsets(HALO_C, KC))
            dco = dld(D_A + D_B, D_C)
            dproj_ref[rows, C_CB:C_CB + D_C] = (dco * conv).astype(BF16)
            dconv = dco * ld(C_CB, D_C)
            dcs_ref[rows, :] = dconv
            _tap_grads(a_ccw, ccs_ref, r0, dconv, _causal_offsets(HALO_C, KC), shifted_ref)
            return carry

        lax.fori_loop(0, n_chunks, pass1, 0)

        def pass2(i, carry):
            r0 = pl.multiple_of(i * CHUNK, CHUNK)
            rows = pl.ds(r0, CHUNK)
            ld = lambda c0, w: proj_ref[rows, c0:c0 + w].astype(F32)
            dg = _conv_taps(dcas_ref, r0, caw_ref, _anticausal_offsets(KA))
            pa = ld(C_AV, D_A)
            sg = jax.nn.sigmoid(ld(C_AG, D_A))
            dproj_ref[rows, C_AV:C_AV + D_A] = (dg * sg).astype(BF16)
            dproj_ref[rows, C_AG:C_AG + D_A] = (dg * pa * sg * (1.0 - sg)).astype(BF16)
            dcc = _conv_taps(dcs_ref, r0, ccw_ref, _anticausal_offsets(KC))
            dproj_ref[rows, C_CC:C_CC + D_C] = (dcc * ld(C_CX, D_C)).astype(BF16)
            dproj_ref[rows, C_CX:C_CX + D_C] = (dcc * ld(C_CC, D_C)).astype(BF16)
            return carry

        lax.fori_loop(0, n_chunks, pass2, 0)

        for k in range(KA):
            dcaw_ref[k:k + 1, :] = jnp.sum(a_caw[k * 8:(k + 1) * 8, :], axis=0, keepdims=True)
        dcaw_ref[KA:KA + 1, :] = jnp.zeros((1, D_A), F32)
        for k in range(8):
            if k < KC:
                dccw_ref[k:k + 1, :] = jnp.sum(a_ccw[k * 8:(k + 1) * 8, :], axis=0, keepdims=True)
            else:
                dccw_ref[k:k + 1, :] = jnp.zeros((1, D_C), F32)
        dcab_ref[...] = a_cab[...]
        dlag_ref[...] = a_lag[...]
        dlab_ref[...] = a_lab[...]
        dlvg_ref[...] = a_lvg[...]
        dlvb_ref[...] = a_lvb[...]

    fix2 = lambda b: (0, 0)
    args = [proj, dcat, ca, p["caw"], p["cab"], p["lag"], p["lab"], p["lvg"], p["lvb"], p["wm"], p["bsx"], p["ccw"]]
    once = pl.Buffered(1)
    in_specs = [pl.BlockSpec((s_len, IN_W), lambda b: (b, 0), pipeline_mode=once),
                pl.BlockSpec((s_len, D), lambda b: (b, 0), pipeline_mode=once),
                pl.BlockSpec((s_len, D_A), lambda b: (b, 0), pipeline_mode=once)]
    for a in args[3:]:
        in_specs.append(pl.BlockSpec(a.shape, (lambda b: (0, 0, 0)) if a.ndim == 3 else fix2))

    def per_seq(*shape):
        nd = len(shape)
        return (pl.BlockSpec((None,) + shape, lambda b: (b,) + (0,) * nd), _sds((bd,) + shape, F32))

    outs = [(pl.BlockSpec((s_len, IN_W), lambda b: (b, 0), pipeline_mode=once), _sds((n, IN_W), BF16)),
            per_seq(32, D_A), per_seq(8, D_A), per_seq(8, D_A), per_seq(8, D_A), per_seq(8, D_B), per_seq(8, D_B),
            per_seq(B_HEADS, CHUNK, CHUNK), per_seq(CHUNK, CHUNK), per_seq(8, D_C)]
    res, rode = _call(
        kern, name, (bd,), in_specs, [o[0] for o in outs], [o[1] for o in outs], args, ("parallel",),
        scratch=[pltpu.VMEM((s_len + HALO_A, D_A), F32), pltpu.VMEM((s_len + HALO_A, D_A), F32),
                 pltpu.VMEM((s_len + HALO_C, D_C), F32), pltpu.VMEM((s_len + HALO_C, D_C), F32),
                 pltpu.VMEM((KA * 8, D_A), F32), pltpu.VMEM((8, D_A), F32), pltpu.VMEM((8, D_A), F32),
                 pltpu.VMEM((8, D_A), F32), pltpu.VMEM((8, D_B), F32), pltpu.VMEM((8, D_B), F32),
                 pltpu.VMEM((KC * 8, D_C), F32), pltpu.VMEM((CHUNK + 8, 128), F32)],
        rider=rider)
    return res if rider is None else (res, rode)


def _adamw_update(w, m, v, g):
    mn = B1 * m + (1.0 - B1) * g
    vn = B2 * v + (1.0 - B2) * jnp.square(g)
    m_hat = mn / (1.0 - B1 ** STEP)
    v_hat = vn / (1.0 - B2 ** STEP)
    return -LR * (m_hat / (jnp.sqrt(v_hat) + ADAM_EPS) + WD * w), mn, vn


def _adamw_layers(w, m, v, own, remote, name):
    shape = w.shape
    depth, c = shape[0], shape[-1]
    r = math.prod(shape[1:-1])
    tr = 256 if r % 256 == 0 else r
    nt = r // tr

    def kern(*refs):
        w_ref, m_ref, v_ref = refs[:3]
        own_refs, rem_refs = refs[3:3 + depth], refs[3 + depth:3 + 4 * depth]
        go_ref, d_ref, mo_ref, vo_ref = refs[3 + 4 * depth:]
        for lp in range(depth):
            @pl.when(pl.program_id(0) == lp)
            def _(lp=lp):
                g = own_refs[lp][...]
                for j in range(3):
                    g = g + rem_refs[3 * lp + j][...].astype(F32)
                go_ref[...] = g
                d_ref[...], mo_ref[...], vo_ref[...] = _adamw_update(w_ref[...], m_ref[...], v_ref[...], g)

    def rows_of(lp):
        return lambda l, i: jnp.where(l == lp, i, jnp.where(l < lp, 0, nt - 1))

    spec = pl.BlockSpec((None, tr, c), lambda l, i: (l, i, 0))
    own_specs = [pl.BlockSpec((tr, c), functools.partial(lambda f, l, i: (f(l, i), 0), rows_of(lp)))
                 for lp in range(depth)]
    rem_specs = [pl.BlockSpec((None, tr, c), functools.partial(lambda f, j, l, i: (j, f(l, i), 0), rows_of(lp), j))
                 for lp in range(depth) for j in range(3)]
    as3d = lambda a: a.reshape(depth, r, c)
    outs = pl.pallas_call(
        kern, name=name, grid=(depth, nt), in_specs=[spec] * 3 + own_specs + rem_specs, out_specs=[spec] * 4,
        out_shape=[_sds((depth, r, c), F32)] * 4, compiler_params=_cp(("arbitrary", "arbitrary")))(
            as3d(w), as3d(m), as3d(v), *[o.reshape(r, c) for o in own],
            *[x.reshape(3, r, c) for x in remote for _ in range(3)])
    return [o.reshape(shape) for o in outs]


def _adamw(w, m, v, g_parts, name):
    shape = w.shape
    c = shape[-1]
    r = math.prod(shape[:-1])
    as2d = lambda a: a.reshape(r, c)
    tr = 512 if r % 512 == 0 else r
    slots = [(a.reshape(a.shape[0], r, c), p) for a in g_parts for p in range(a.shape[0])]
    n_g = len(slots)

    def kern(*refs):
        w_ref, m_ref, v_ref = refs[:3]
        g_refs = refs[3:3 + n_g]
        go_ref, d_ref, mo_ref, vo_ref = refs[3 + n_g:]
        g = g_refs[0][...].astype(F32)
        for gr in g_refs[1:]:
            g = g + gr[...].astype(F32)
        go_ref[...] = g
        d_ref[...], mo_ref[...], vo_ref[...] = _adamw_update(w_ref[...], m_ref[...], v_ref[...], g)

    spec = pl.BlockSpec((tr, c), lambda i: (i, 0))
    g_specs = [pl.BlockSpec((None, tr, c), functools.partial(lambda p, i: (p, i, 0), p)) for _, p in slots]
    outs = pl.pallas_call(
        kern, name=name, grid=(r // tr,), in_specs=[spec] * 3 + g_specs, out_specs=[spec] * 4,
        out_shape=[_sds((r, c), F32)] * 4, compiler_params=_cp(("parallel",)))(
            as2d(w), as2d(m), as2d(v), *[a for a, _ in slots])
    return [o.reshape(shape) for o in outs]


def _place():
    x, y, c = lax.axis_index("x"), lax.axis_index("y"), lax.axis_index("c")
    other_chips = [(1 - x, y), (x, 1 - y), (1 - x, 1 - y)]
    return x, y, c, other_chips


def _gather_rider(arrays):
    n_arr = len(arrays)

    def parts(ins, outs, sems):
        send_sems, recv_sems, local_sems = sems
        x, y, c, chips = _place()
        me, sibling = (x, y, c), (x, y, 1 - c)
        slot = lambda px, py, pc: 4 * px + 2 * py + pc

        def copy(a, k, block, to, from_input=False):
            dst = outs[a].at[slot(*block)]
            return pltpu.make_async_remote_copy(
                src_ref=ins[a] if from_input else dst, dst_ref=dst, send_sem=send_sems.at[k, a],
                recv_sem=recv_sems.at[k, a], device_id=to, device_id_type=MESH)

        mine = [pltpu.make_async_copy(ins[a], outs[a].at[slot(*me)], local_sems.at[a]) for a in range(n_arr)]
        first = []
        for a in range(n_arr):
            first.append(copy(a, 0, me, sibling, True))
            first += [copy(a, 1 + j, me, (*chip, c), True) for j, chip in enumerate(chips)]
        return copy, mine, first, me, sibling, chips, c

    def start(ins, outs, sems):
        _, mine, first, *_ = parts(ins, outs, sems)
        for cp in mine + first:
            cp.start()

    def finish(ins, outs, sems):
        copy, mine, first, me, sibling, chips, c = parts(ins, outs, sems)
        passed = []
        for j, chip in enumerate(chips):
            for a in range(n_arr):
                copy(a, 1 + j, (*chip, c), me).wait_recv()
                passed.append(copy(a, 4 + j, (*chip, c), sibling))
                passed[-1].start()
        for a in range(n_arr):
            copy(a, 0, sibling, me).wait_recv()
            for j, chip in enumerate(chips):
                copy(a, 4 + j, (*chip, 1 - c), me).wait_recv()
        for cp in first + passed:
            cp.wait_send()
        for cp in mine:
            cp.wait()

    return _Rider(list(arrays), [_sds((N_DEV,) + a.shape, a.dtype) for a in arrays],
                  [pltpu.SemaphoreType.DMA((7, n_arr)), pltpu.SemaphoreType.DMA((7, n_arr)),
                   pltpu.SemaphoreType.DMA((n_arr,))], start, finish)


def _exchange_rider(arrays):
    n_arr = len(arrays)

    def copies(ins, outs, sems):
        send_sems, recv_sems = sems
        x, y, c, _ = _place()
        return [pltpu.make_async_remote_copy(
            src_ref=ins[a].at[:, 1 - c], dst_ref=outs[a], send_sem=send_sems.at[a], recv_sem=recv_sems.at[a],
            device_id=(x, y, 1 - c), device_id_type=MESH) for a in range(n_arr)]

    def start(ins, outs, sems):
        for cp in copies(ins, outs, sems):
            cp.start()

    def finish(ins, outs, sems):
        for cp in copies(ins, outs, sems):
            cp.wait()

    return _Rider(list(arrays), [_sds(a.shape[:1] + a.shape[2:], a.dtype) for a in arrays],
                  [pltpu.SemaphoreType.DMA((n_arr,)), pltpu.SemaphoreType.DMA((n_arr,))], start, finish)


def _pair_add(mine, theirs, core, name):
    _, _, r, c = mine.shape
    tr = 512 if r % 512 == 0 else r

    def kern(core_ref, a_ref, b_ref, o_ref, ob_ref):
        s = a_ref[...] + b_ref[...]
        o_ref[...] = s
        ob_ref[...] = s.astype(BF16)

    row = lambda t, i, core_ref: (t, i, 0)
    return pl.pallas_call(
        kern, name=name,
        grid_spec=pltpu.PrefetchScalarGridSpec(
            num_scalar_prefetch=1, grid=(4, r // tr),
            in_specs=[pl.BlockSpec((None, None, tr, c), lambda t, i, core_ref: (t, core_ref[0], i, 0)),
                      pl.BlockSpec((None, tr, c), row)],
            out_specs=[pl.BlockSpec((None, tr, c), row), pl.BlockSpec((None, tr, c), row)]),
        out_shape=[_sds((4, r, c), F32), _sds((4, r, c), BF16)],
        compiler_params=_cp(("parallel", "parallel")))(core, mine, theirs)


def _scatter_rider(sums, sums_bf16):
    n_arr = len(sums)

    def copies(ins, outs, sems):
        send_sems, recv_sems, local_sems = sems
        x, y, c, chips = _place()
        own = [pltpu.make_async_copy(ins[a].at[2 * x + y], outs[a], local_sems.at[a]) for a in range(n_arr)]
        remote = []
        for a in range(n_arr):
            for j, (px, py) in enumerate(chips):
                remote.append(pltpu.make_async_remote_copy(
                    src_ref=ins[n_arr + a].at[2 * px + py], dst_ref=outs[n_arr + a].at[j], send_sem=send_sems.at[j, a],
                    recv_sem=recv_sems.at[j, a], device_id=(px, py, c), device_id_type=MESH))
        return own + remote

    def start(ins, outs, sems):
        for cp in copies(ins, outs, sems):
            cp.start()

    def finish(ins, outs, sems):
        for cp in copies(ins, outs, sems):
            cp.wait()

    return _Rider(list(sums) + list(sums_bf16),
                  [_sds(a.shape[1:], a.dtype) for a in sums] + [_sds((3,) + a.shape[1:], a.dtype) for a in sums_bf16],
                  [pltpu.SemaphoreType.DMA((3, n_arr)), pltpu.SemaphoreType.DMA((3, n_arr)),
                   pltpu.SemaphoreType.DMA((n_arr,))], start, finish)


def _mixer_params(conv_a_w, conv_a_b, ln_a_g, ln_a_b, ln_v_g, ln_v_b, w_s, b_s, conv_c_w):
    causal = jnp.tril(jnp.ones((CHUNK, CHUNK), dtype=bool))
    row = lambda a: a.reshape(1, -1)
    return dict(
        caw=jnp.pad(conv_a_w, ((0, 32 - KA), (0, 0))), cab=row(conv_a_b), lag=row(ln_a_g), lab=row(ln_a_b),
        lvg=row(ln_v_g), lvb=row(ln_v_b), wm=jnp.where(causal[None], w_s, 0.0).astype(BF16),
        bsx=jnp.repeat(b_s.T, HEAD, axis=1), ccw=jnp.pad(conv_c_w, ((0, 8 - KC), (0, 0))))


class _Schedule:
    def __init__(self, big=None):
        self.big = big

    def weights(self, l):
        return {k: v[l] for k, v in self.big.items()}

    def rider(self, stage, l):
        return None

    def rode(self, stage, l, results):
        pass

    def note_grads(self, l, grads):
        pass

    def finish(self):
        pass


def _local_step(x, mem, target, sched, small, bd):
    row = lambda a: a.reshape(1, -1)

    def ride(stage, l, fn, *args, **kw):
        rider = sched.rider(stage, l)
        res = fn(*args, rider=rider, **kw)
        if rider is not None:
            res, results = res
            sched.rode(stage, l, results)
        return res

    saved = []
    for l in range(DEPTH):
        big = sched.weights(l)
        mp = _mixer_params(big["conv_a_w"], *[small[k][l] for k in ("conv_a_b", "ln_a_g", "ln_a_b", "ln_v_g", "ln_v_b",
                                                                    "w_s", "b_s")], big["conv_c_w"])
        proj = ride("in_proj", l, _mm_nn, x, big["w_in"], BF16, f"in_proj_{l}")
        cat, ca = ride("mixer_fwd", l, _mixer_fwd, proj, mp, bd, f"mixer_fwd_{l}")
        x1, xh1, rs1, q = _mm_res_ln(cat, big["w_out"], x, row(small["ln1_g"][l]), row(small["ln1_b"][l]),
                                     f"out_proj_ln1_q_{l}", then=big["w_q"])
        kv = _mm_nn(mem, big["w_kv"], BF16, f"kv_proj_{l}")
        o = _attn_fwd(q, kv, bd, f"attn_fwd_{l}")
        x2, xh2, rs2, h = ride("ff1", l, _mm_res_ln, o, big["w_o"], x1, row(small["ln2_g"][l]), row(small["ln2_b"][l]),
                               f"o_proj_ln2_ff1_{l}", then=big["w_ff1"])
        x3, xh3, rs3 = ride("ff2_ln3", l, _mm_res_ln, h, big["w_ff2"], x2, row(small["ln3_g"][l]),
                            row(small["ln3_b"][l]), f"ff2_ln3_{l}", relu2=True)
        saved.append(dict(mp=mp, x=x, proj=proj, cat=cat, ca=ca, x1=x1, xh1=xh1, rs1=rs1, q=q, kv=kv, o=o, x2=x2, xh2=xh2,
                          rs2=rs2, h=h, xh3=xh3, rs3=rs3))
        x = x3

    grads = {k: [None] * DEPTH for k in WEIGHTS}
    s = saved[-1]
    sq, dz3, dg, db = _loss_lnbwd(x, target, s["xh3"], s["rs3"], row(small["ln3_g"][DEPTH - 1]), "loss_ln3_bwd")
    grad_x = None
    causal = jnp.tril(jnp.ones((CHUNK, CHUNK), dtype=bool))
    for l in reversed(range(DEPTH)):
        s = saved[l]
        big = sched.weights(l)
        grads["ln3_g"][l], grads["ln3_b"][l] = jnp.sum(dg, axis=0), jnp.sum(db, axis=0)
        dh = ride("ff2_bwd", l, _mm_nt, dz3, big["w_ff2"], BF16, f"ff2_bwd_{l}", relu2_grad_of=s["h"], tn=D_FF)
        grads["w_ff2"][l] = _mm_tn(s["h"], dz3, f"ff2_wgrad_{l}", relu2=True, t1=2048)
        grads["w_ff1"][l] = _mm_tn_by_owner(s["x2"], dh, f"ff1_wgrad_{l}")
        sched.note_grads(l, {k: grads[k][l] for k in ("w_ff1", "w_ff2")})
        dz2, dg, db, do = _bwd_in(dz3, dh, big["w_ff1"], f"ff1_bwd_ln2_o_bwd_{l}",
                                  ln=(s["xh2"], s["rs2"], row(small["ln2_g"][l])), then=big["w_o"])
        grads["ln2_g"][l], grads["ln2_b"][l] = jnp.sum(dg, axis=0), jnp.sum(db, axis=0)
        grads["w_o"][l] = _mm_tn(s["o"], dz2, f"o_proj_wgrad_{l}")
        dq, dkv = ride("attn_bwd", l, _attn_bwd, s["q"], s["kv"], do, bd, f"attn_bwd_{l}")
        grads["w_q"][l] = _mm_tn(s["x1"], dq, f"q_wgrad_{l}")
        grads["w_kv"][l] = _mm_tn_by_owner(mem, dkv, f"kv_wgrad_{l}")
        sched.note_grads(l, {k: grads[k][l] for k in ("w_o", "w_q", "w_kv")})
        dz1, dg, db, dcat = ride("q_bwd_ln1", l, _bwd_in, dz2, dq, big["w_q"], f"q_bwd_ln1_out_bwd_{l}",
                                 ln=(s["xh1"], s["rs1"], row(small["ln1_g"][l])), then=big["w_out"])
        grads["ln1_g"][l], grads["ln1_b"][l] = jnp.sum(dg, axis=0), jnp.sum(db, axis=0)
        grads["w_out"][l] = _mm_tn(s["cat"], dz1, f"out_proj_wgrad_{l}")
        sched.note_grads(l, {"w_out": grads["w_out"][l]})
        (dproj, dcaw, dcab, dlag, dlab, dlvg, dlvb, dws, dbs, dccw) = ride(
            "mixer_bwd", l, _mixer_bwd, s["proj"], dcat, s["ca"], s["mp"], bd, f"mixer_bwd_{l}")
        grads["conv_a_w"][l] = jnp.sum(dcaw, axis=0)[:KA]
        grads["conv_a_b"][l] = jnp.sum(dcab, axis=(0, 1))
        grads["ln_a_g"][l] = jnp.sum(dlag, axis=(0, 1))
        grads["ln_a_b"][l] = jnp.sum(dlab, axis=(0, 1))
        grads["ln_v_g"][l] = jnp.sum(dlvg, axis=(0, 1))
        grads["ln_v_b"][l] = jnp.sum(dlvb, axis=(0, 1))
        grads["w_s"][l] = jnp.where(causal[None], jnp.sum(dws, axis=0), 0.0)
        grads["b_s"][l] = jnp.sum(dbs, axis=0)[:, :B_HEADS].T
        grads["conv_c_w"][l] = jnp.sum(dccw, axis=0)[:KC]
        grads["w_in"][l] = _mm_tn(s["x"], dproj, f"in_proj_wgrad_{l}")
        sched.note_grads(l, {k: v[l] for k, v in grads.items() if k not in ("w_ff1", "w_ff2")})
        if l > 0:
            p = saved[l - 1]
            dz3, dg, db = _bwd_in(dz1, dproj, big["w_in"], f"in_proj_bwd_ln3_{l}",
                                  ln=(p["xh3"], p["rs3"], row(small["ln3_g"][l - 1])))
        else:
            grad_x = ride("in_proj_bwd", 0, _bwd_in, dz1, dproj, big["w_in"], "in_proj_bwd_0")
    sched.finish()
    return sq, grad_x, grads


WEIGHTS = ("w_in", "conv_a_w", "conv_a_b", "ln_a_g", "ln_a_b", "ln_v_g", "ln_v_b", "w_s", "b_s", "conv_c_w", "w_out",
           "ln1_g", "ln1_b", "w_q", "w_kv", "w_o", "ln2_g", "ln2_b", "w_ff1", "w_ff2", "ln3_g", "ln3_b")
COL_SHARDED = ("w_in", "w_kv", "w_ff1")
ROW_SHARDED = ("w_out", "w_q", "w_o", "w_ff2")
BIG = COL_SHARDED + ROW_SHARDED
REPLICATED = tuple(k for k in WEIGHTS if k not in BIG and k not in ("conv_a_w", "conv_c_w"))
PACK_LANES = 128


CONV_ROWS = 32 + 8
GATHER_LAYER0 = {"first": ("w_in", "conv"), "in_proj": ("w_out", "w_q", "w_kv", "w_o"), "mixer_fwd": ("w_ff1", "w_ff2")}
GATHER_NEXT = {"ff1": ("w_ff2", "w_in", "w_out", "w_q", "w_o", "conv"), "ff2_ln3": ("w_ff1", "w_kv")}
GRADS_EARLY = ("w_ff1", "w_ff2")
GRADS_LATE = ("w_in", "w_kv", "w_out", "w_q", "w_o", "conv")
GRADS_MID0, GRADS_OUT0, GRADS_LAST0 = ("w_kv", "w_q", "w_o"), ("w_out",), ("w_in", "conv")


def _gathered_to_full(g, col_sharded):
    _, a, b = g.shape
    if col_sharded:
        return g.transpose(1, 0, 2).reshape(a, N_DEV * b)
    return g.reshape(N_DEV * a, b)


def _full_to_owner_major(g, col_sharded):
    if g.ndim == 4:
        return g
    a, b = g.shape
    if col_sharded:
        return g.reshape(a, 4, 2, b // N_DEV).transpose(1, 2, 0, 3)
    return g.reshape(4, 2, a // N_DEV, b)


def _conv_pack(conv_a, conv_c):
    pad = lambda a, rows: jnp.pad(a, [(0, 0)] * (a.ndim - 2) + [(0, rows - a.shape[-2]), (0, 0)])
    return jnp.concatenate([pad(conv_a, 32), pad(conv_c, 8)], axis=-2)


def _conv_unpack(packed):
    return packed[..., :KA, :], packed[..., 32:32 + KC, :]


class _Overlapped(_Schedule):
    def __init__(self, shards_bf16, conv_shards, core):
        self.shards, self.conv_shards, self.core = shards_bf16, conv_shards, core
        self.full = {l: {} for l in range(DEPTH)}
        self.grads = {l: {} for l in range(DEPTH)}
        self.owner_major = {}
        self.from_sibling = {}
        self.scattering = None
        self.own, self.remote = {}, {}
        self.replicated = None

    def _gather(self, l, names):
        return _gather_rider([self.conv_shards[l] if k == "conv" else self.shards[k][l] for k in names])

    def _store(self, l, names, gathered):
        for k, g in zip(names, gathered):
            if k == "conv":
                self.full[l]["conv_a_w"], self.full[l]["conv_c_w"] = _conv_unpack(_gathered_to_full(g, True))
            else:
                self.full[l][k] = _gathered_to_full(g, k in COL_SHARDED)

    def weights(self, l):
        if l == 0 and not self.full[0]:
            names = GATHER_LAYER0["first"]
            self._store(0, names, _ride_alone(self._gather(0, names), "weights_all_gather_first"))
        return self.full[l]

    def note_grads(self, l, grads):
        self.grads[l].update(grads)

    def _owner_major(self, l, k):
        if (l, k) not in self.owner_major:
            g = self.grads[l]
            if k == "conv":
                self.owner_major[(l, k)] = _full_to_owner_major(_conv_pack(g["conv_a_w"], g["conv_c_w"]), True)
            else:
                self.owner_major[(l, k)] = _full_to_owner_major(g[k], k in COL_SHARDED)
        return self.owner_major[(l, k)]

    def _exchange(self, l, names):
        return _exchange_rider([self._owner_major(l, k) for k in names])

    def _scatter(self, groups):
        sums, sums_bf16, self.scattering = [], [], []
        for l, names in groups:
            for k, r in zip(names, self.from_sibling.pop((l, names))):
                s, sb = _pair_add(self._owner_major(l, k), r, self.core, f"grad_pair_add_{l}_{k}")
                sums.append(s)
                sums_bf16.append(sb)
                self.scattering.append((l, k))
        return _scatter_rider(sums, sums_bf16)

    def _scattered(self, results):
        n = len(self.scattering)
        for i, key in enumerate(self.scattering):
            self.own[key], self.remote[key] = results[i], results[n + i]

    def rider(self, stage, l):
        if l == 0 and stage in ("in_proj", "mixer_fwd"):
            return self._gather(0, GATHER_LAYER0[stage])
        if stage in GATHER_NEXT and l + 1 < DEPTH:
            return self._gather(l + 1, GATHER_NEXT[stage])
        if stage == "ff2_bwd" and l + 1 < DEPTH:
            return self._exchange(l + 1, GRADS_LATE)
        if stage == "attn_bwd":
            return self._exchange(l, GRADS_EARLY)
        if stage == "q_bwd_ln1" and l == 0:
            return self._exchange(0, GRADS_MID0)
        if stage == "mixer_bwd" and l == 0:
            return _join_riders([self._scatter([(0, GRADS_EARLY), (1, GRADS_LATE), (0, GRADS_MID0)]),
                                 self._exchange(0, GRADS_OUT0)])
        if stage == "mixer_bwd":
            return self._scatter([(l, GRADS_EARLY)] + ([(l + 1, GRADS_LATE)] if l + 1 < DEPTH else []))
        if stage == "in_proj_bwd":
            packed = _pack_rows([jnp.stack([self.grads[i][k] for i in range(DEPTH)]) for k in REPLICATED])
            return _gather_rider([packed])
        return None

    def rode(self, stage, l, results):
        if l == 0 and stage in ("in_proj", "mixer_fwd"):
            self._store(0, GATHER_LAYER0[stage], results)
        elif stage in GATHER_NEXT:
            self._store(l + 1, GATHER_NEXT[stage], results)
        elif stage == "ff2_bwd":
            self.from_sibling[(l + 1, GRADS_LATE)] = results
        elif stage == "attn_bwd":
            self.from_sibling[(l, GRADS_EARLY)] = results
        elif stage == "q_bwd_ln1":
            self.from_sibling[(0, GRADS_MID0)] = results
        elif stage == "mixer_bwd" and l == 0:
            n = 2 * len(self.scattering)
            self._scattered(results[:n])
            self.from_sibling[(0, GRADS_OUT0)] = results[n:]
        elif stage == "mixer_bwd":
            self._scattered(results)
        elif stage == "in_proj_bwd":
            self.replicated = results[0]

    def finish(self):
        self.from_sibling[(0, GRADS_LAST0)] = _ride_alone(self._exchange(0, GRADS_LAST0), "grad_pair_exchange_last")
        self._scattered(_ride_alone(self._scatter([(0, GRADS_OUT0), (0, GRADS_LAST0)]), "grad_chip_scatter_last"))


def _pack_rows(parts):
    flat = jnp.concatenate([p.reshape(-1, PACK_LANES) for p in parts], axis=0)
    return jnp.pad(flat, ((0, -flat.shape[0] % 8), (0, 0)))


def _unpack_rows(packed, like):
    out, r = [], 0
    for p in like:
        n = p.size // PACK_LANES
        out.append(packed[r:r + n].reshape(p.shape))
        r += n
    return out


def kernel(x, mem, w_in, conv_a_w, conv_a_b, ln_a_g, ln_a_b, ln_v_g, ln_v_b, w_s, b_s, conv_c_w, w_out, ln1_g, ln1_b, w_q, w_kv, w_o, ln2_g, ln2_b, w_ff1, w_ff2, ln3_g, ln3_b, loss_target, m_w_in, m_conv_a_w, m_conv_a_b, m_ln_a_g, m_ln_a_b, m_ln_v_g, m_ln_v_b, m_w_s, m_b_s, m_conv_c_w, m_w_out, m_ln1_g, m_ln1_b, m_w_q, m_w_kv, m_w_o, m_ln2_g, m_ln2_b, m_w_ff1, m_w_ff2, m_ln3_g, m_ln3_b, v_w_in, v_conv_a_w, v_conv_a_b, v_ln_a_g, v_ln_a_b, v_ln_v_g, v_ln_v_b, v_w_s, v_b_s, v_conv_c_w, v_w_out, v_ln1_g, v_ln1_b, v_w_q, v_w_kv, v_w_o, v_ln2_g, v_ln2_b, v_w_ff1, v_w_ff2, v_ln3_g, v_ln3_b):
    given = dict(locals())
    w = {k: given[k] for k in WEIGHTS}
    mom = {k: given["m_" + k] for k in WEIGHTS}
    var = {k: given["v_" + k] for k in WEIGHTS}
    bd, s_len, _ = x.shape
    core = lax.axis_index("c").astype(jnp.int32).reshape(1)

    conv_pack = lambda d: _conv_pack(d["conv_a_w"], d["conv_c_w"])
    sched = _Overlapped({k: w[k].astype(BF16) for k in BIG}, conv_pack(w), core)
    sq, grad_x, grads = _local_step(x.reshape(bd * s_len, D), mem.reshape(-1, D), loss_target.reshape(bd * s_len, D),
                                    sched, {k: w[k] for k in REPLICATED}, bd)
    loss = lax.psum(0.5 * jnp.sum(sq) / D, ("x", "y", "c"))

    out = {}
    for k in BIG + ("conv",):
        own = [sched.own[(l, k)] for l in range(DEPTH)]
        remote = [sched.remote[(l, k)] for l in range(DEPTH)]
        if k == "conv":
            conv_out = _adamw_layers(conv_pack(w), conv_pack(mom), conv_pack(var), own, remote, "adamw_conv")
            unpacked = [_conv_unpack(o) for o in conv_out]
            out["conv_a_w"], out["conv_c_w"] = [u[0] for u in unpacked], [u[1] for u in unpacked]
        else:
            out[k] = _adamw_layers(w[k], mom[k], var[k], own, remote, f"adamw_{k}")

    rep_out = _adamw(_pack_rows([w[k] for k in REPLICATED]), _pack_rows([mom[k] for k in REPLICATED]),
                     _pack_rows([var[k] for k in REPLICATED]), [sched.replicated], "adamw_replicated")
    for i, o in enumerate(rep_out):
        for k, piece in zip(REPLICATED, _unpack_rows(o, [w[k] for k in REPLICATED])):
            out.setdefault(k, [None] * 4)[i] = piece

    res = [loss, grad_x.reshape(bd, s_len, D)]
    for i in range(4):
        res += [out[k][i] for k in WEIGHTS]
    return tuple(res)
```

```python
import functools
import math

import jax
import jax.numpy as jnp
from jax import lax
from jax.experimental import pallas as pl
from jax.experimental.pallas import tpu as pltpu

F32 = jnp.float32
BF16 = jnp.bfloat16

DEPTH = 4
D = 1024
D_A, D_B, D_C = 384, 256, 384
HEAD = 64
B_HEADS = 4
CHUNK = 128
KA, KC = 31, 3
HALO_A, HALO_C = 32, 8
IN_W = 2 * D_A + 2 * D_B + 3 * D_C
X_HEADS = 4
X_HD = D // X_HEADS
D_FF = 4 * D
EPS = 1e-5
ALPHA = (2.0 * DEPTH) ** 0.25
LR, B1, B2, ADAM_EPS, WD, STEP = 0.001, 0.9, 0.999, 1e-08, 0.01, 10
INV_SQRT2 = 0.7071067811865476
INV_SQRT_2PI = 0.3989422804014327
N_DEV = 8
VMEM_LIMIT = 56 * 1024 * 1024
MESH = pl.DeviceIdType.MESH
ANY = pl.BlockSpec(memory_space=pl.ANY)


def _cp(sem=None):
    return pltpu.CompilerParams(dimension_semantics=sem, vmem_limit_bytes=VMEM_LIMIT)


def _sds(shape, dtype):
    return jax.ShapeDtypeStruct(tuple(shape), dtype)


class _Rider:
    def __init__(self, arrays, out_shape, sems, start, finish):
        self.arrays, self.out_shape, self.sems, self.start, self.finish = arrays, out_shape, sems, start, finish


def _call(kern, name, grid, in_specs, out_specs, out_shape, args, sem, scratch=(), rider=None):
    single = not isinstance(out_shape, (list, tuple))
    out_specs_l = [out_specs] if single else list(out_specs)
    out_shape_l = [out_shape] if single else list(out_shape)
    if rider is None:
        res = pl.pallas_call(kern, name=name, grid=grid, in_specs=in_specs, out_specs=out_specs_l,
                             out_shape=out_shape_l, scratch_shapes=list(scratch), compiler_params=_cp(sem))(*args)
        return (res[0] if single else list(res)), None
    n_in, n_out, n_scr = len(args), len(out_shape_l), len(scratch)
    n_rin, n_rout = len(rider.arrays), len(rider.out_shape)

    def body(*refs):
        ins, refs = refs[:n_in], refs[n_in:]
        r_ins, refs = refs[:n_rin], refs[n_rin:]
        outs, refs = refs[:n_out], refs[n_out:]
        r_outs, refs = refs[:n_rout], refs[n_rout:]
        scr, r_sems = refs[:n_scr], refs[n_scr:]
        ids = [pl.program_id(d) for d in range(len(grid))]
        first = functools.reduce(jnp.logical_and, [i == 0 for i in ids])
        last = functools.reduce(jnp.logical_and, [i == g - 1 for i, g in zip(ids, grid)])

        @pl.when(first)
        def _():
            rider.start(r_ins, r_outs, r_sems)

        kern(*ins, *outs, *scr)

        @pl.when(last)
        def _():
            rider.finish(r_ins, r_outs, r_sems)

    res = pl.pallas_call(
        body, name=name, grid=grid, in_specs=list(in_specs) + [ANY] * n_rin,
        out_specs=out_specs_l + [ANY] * n_rout, out_shape=out_shape_l + list(rider.out_shape),
        scratch_shapes=list(scratch) + list(rider.sems),
        compiler_params=_cp(("arbitrary",) * len(grid)))(*args, *rider.arrays)
    mine, theirs = list(res[:n_out]), list(res[n_out:])
    return (mine[0] if single else mine), theirs


def _join_riders(riders):
    if len(riders) == 1:
        return riders[0]

    def parts(seq, attr):
        out, at = [], 0
        for r in riders:
            n = len(getattr(r, attr))
            out.append(seq[at:at + n])
            at += n
        return out

    def each(method):
        def run(ins, outs, sems):
            for r, i, o, s in zip(riders, parts(ins, "arrays"), parts(outs, "out_shape"), parts(sems, "sems")):
                getattr(r, method)(i, o, s)
        return run

    return _Rider(sum([r.arrays for r in riders], []), sum([r.out_shape for r in riders], []),
                  sum([r.sems for r in riders], []), each("start"), each("finish"))


def _ride_alone(rider, name):
    def body(*refs):
        n_rin, n_rout = len(rider.arrays), len(rider.out_shape)
        r_ins, r_outs, r_sems = refs[:n_rin], refs[n_rin:n_rin + n_rout], refs[n_rin + n_rout:]
        rider.start(r_ins, r_outs, r_sems)
        rider.finish(r_ins, r_outs, r_sems)

    return list(pl.pallas_call(
        body, name=name, in_specs=[ANY] * len(rider.arrays), out_specs=[ANY] * len(rider.out_shape),
        out_shape=list(rider.out_shape), scratch_shapes=list(rider.sems))(*rider.arrays))


def _ln(z):
    mu = jnp.mean(z, axis=-1, keepdims=True)
    zc = z - mu
    var = jnp.mean(zc * zc, axis=-1, keepdims=True)
    rstd = lax.rsqrt(var + EPS)
    return zc * rstd, rstd


def _ln_bwd(dxhat, xhat, rstd):
    m1 = jnp.mean(dxhat, axis=-1, keepdims=True)
    m2 = jnp.mean(dxhat * xhat, axis=-1, keepdims=True)
    return rstd * (dxhat - m1 - xhat * m2)


def _gelu(x):
    return 0.5 * x * (1.0 + lax.erf(x * INV_SQRT2))


def _gelu_and_grad(x):
    cdf = 0.5 * (1.0 + lax.erf(x * INV_SQRT2))
    return x * cdf, cdf + x * jnp.exp(-0.5 * x * x) * INV_SQRT_2PI


def _fold8(x):
    r, c = x.shape
    return jnp.sum(x.reshape(r // 8, 8, c), axis=0)


def _relu2(h):
    return jnp.square(jnp.maximum(h, 0.0))


def _weight_spec(block, index_map, resident):
    return pl.BlockSpec(block, index_map, pipeline_mode=pl.Buffered(1) if resident else None)


def _mm_nn(a, w, out_dtype, name, tm=512, tn=None, rider=None):
    n, k = a.shape
    m = w.shape[1]
    tm = min(tm, n)
    tn = m if tn is None else min(tn, m)

    def kern(a_ref, w_ref, o_ref):
        o_ref[...] = jnp.dot(a_ref[...].astype(BF16), w_ref[...], preferred_element_type=F32).astype(out_dtype)

    res, rode = _call(
        kern, name, (n // tm, m // tn),
        [pl.BlockSpec((tm, k), lambda i, j: (i, 0)), _weight_spec((k, tn), lambda i, j: (0, j), tn == m)],
        pl.BlockSpec((tm, tn), lambda i, j: (i, j)), _sds((n, m), out_dtype), (a, w), ("parallel", "parallel"),
        rider=rider)
    return res if rider is None else (res, rode)


def _mm_res_ln(a, w, res, g, b, name, relu2=False, tm=512, rider=None, then=None):
    n, k = a.shape
    tm = min(tm, n)

    m2 = None if then is None else then.shape[1]

    def kern(*refs):
        a_ref, w_ref, res_ref, g_ref, b_ref = refs[:5]
        x_ref, xhat_ref, rstd_ref = refs[-3:] if then is None else refs[-4:-1]
        av = a_ref[...]
        if relu2:
            av = _relu2(av.astype(F32))
        z = ALPHA * res_ref[...] + jnp.dot(av.astype(BF16), w_ref[...], preferred_element_type=F32)
        xhat, rstd = _ln(z)
        xhat_ref[...] = xhat
        rstd_ref[...] = rstd
        x = xhat * g_ref[...] + b_ref[...]
        x_ref[...] = x
        if then is not None:
            refs[-1][...] = jnp.dot(x.astype(BF16), refs[5][...], preferred_element_type=F32).astype(BF16)

    row = lambda i: (i, 0)
    fix = lambda i: (0, 0)
    in_specs = [pl.BlockSpec((tm, k), row), _weight_spec((k, D), fix, True), pl.BlockSpec((tm, D), row),
                pl.BlockSpec((1, D), fix), pl.BlockSpec((1, D), fix)]
    out_specs = [pl.BlockSpec((tm, D), row), pl.BlockSpec((tm, D), row), pl.BlockSpec((tm, 1), row)]
    out_shape = [_sds((n, D), F32), _sds((n, D), F32), _sds((n, 1), F32)]
    args = (a, w, res, g, b)
    if then is not None:
        in_specs.append(_weight_spec((D, m2), fix, True))
        out_specs.append(pl.BlockSpec((tm, m2), row))
        out_shape.append(_sds((n, m2), BF16))
        args += (then,)
    out, rode = _call(kern, name, (n // tm,), in_specs, out_specs, out_shape, args, ("parallel",), rider=rider)
    return out if rider is None else (out, rode)


def _mm_nt(a, w, out_dtype, name, relu2_grad_of=None, tm=512, tn=1024, rider=None):
    n, k = a.shape
    m = w.shape[0]
    tm = min(tm, n)
    tn = min(tn, m)
    with_h = relu2_grad_of is not None

    def kern(*refs):
        a_ref, w_ref = refs[0], refs[1]
        o_ref = refs[-1]
        r = lax.dot_general(a_ref[...].astype(BF16), w_ref[...], (((1,), (1,)), ((), ())), preferred_element_type=F32)
        if with_h:
            r = r * (2.0 * jnp.maximum(refs[2][...].astype(F32), 0.0))
        o_ref[...] = r.astype(out_dtype)

    in_specs = [pl.BlockSpec((tm, k), lambda i, j: (i, 0)), _weight_spec((tn, k), lambda i, j: (j, 0), tn == m)]
    args = [a, w]
    if with_h:
        in_specs.append(pl.BlockSpec((tm, tn), lambda i, j: (i, j)))
        args.append(relu2_grad_of)
    res, rode = _call(kern, name, (n // tm, m // tn), in_specs, pl.BlockSpec((tm, tn), lambda i, j: (i, j)),
                      _sds((n, m), out_dtype), args, ("parallel", "parallel"), rider=rider)
    return res if rider is None else (res, rode)


def _mm_tn_by_owner(a, b, name, tk=512):
    n, k1 = a.shape
    m = b.shape[1]
    s = m // N_DEV
    tk = min(tk, n)

    def kern(a_ref, b_ref, o_ref):
        @pl.when(pl.program_id(0) == 0)
        def _():
            o_ref[...] = jnp.zeros_like(o_ref)

        av = a_ref[...].astype(BF16)
        for j in range(N_DEV):
            o_ref[j // 2, j % 2] += lax.dot_general(av, b_ref[:, j * s:(j + 1) * s].astype(BF16),
                                                    (((0,), (0,)), ((), ())), preferred_element_type=F32)

    return pl.pallas_call(
        kern, name=name, grid=(n // tk,),
        in_specs=[pl.BlockSpec((tk, k1), lambda k: (k, 0)), pl.BlockSpec((tk, m), lambda k: (k, 0))],
        out_specs=pl.BlockSpec((4, 2, k1, s), lambda k: (0, 0, 0, 0), pipeline_mode=pl.Buffered(1)),
        out_shape=_sds((4, 2, k1, s), F32), compiler_params=_cp(("arbitrary",)))(a, b)


def _mm_tn(a, b, name, relu2=False, t1=1024, tn=2048, tk=512):
    n, k1 = a.shape
    m = b.shape[1]
    t1 = min(t1, k1)
    tn = m if m <= 2432 and m % tn else min(tn, m)
    tk = min(tk, n)

    def kern(a_ref, b_ref, o_ref):
        @pl.when(pl.program_id(2) == 0)
        def _():
            o_ref[...] = jnp.zeros_like(o_ref)

        av = a_ref[...]
        if relu2:
            av = _relu2(av.astype(F32))
        o_ref[...] += lax.dot_general(av.astype(BF16), b_ref[...].astype(BF16), (((0,), (0,)), ((), ())),
                                      preferred_element_type=F32)

    return pl.pallas_call(
        kern, name=name, grid=(k1 // t1, m // tn, n // tk),
        in_specs=[pl.BlockSpec((tk, t1), lambda i, j, k: (k, i)), pl.BlockSpec((tk, tn), lambda i, j, k: (k, j))],
        out_specs=pl.BlockSpec((t1, tn), lambda i, j, k: (i, j)),
        out_shape=_sds((k1, m), F32),
        compiler_params=_cp(("parallel", "parallel", "arbitrary")))(a, b)


def _bwd_in(dz_next, da, w, name, ln=None, tm=512, rider=None, then=None):
    n, k2 = da.shape
    tm = min(tm, n)
    row = lambda i: (i, 0)
    fix = lambda i: (0, 0)

    def dx_of(dzn_ref, da_ref, w_ref):
        return ALPHA * dzn_ref[...] + lax.dot_general(da_ref[...], w_ref[...], (((1,), (1,)), ((), ())),
                                                      preferred_element_type=F32)

    base_specs = [pl.BlockSpec((tm, D), row), pl.BlockSpec((tm, k2), row), _weight_spec((D, k2), fix, True)]
    if ln is None:
        def kern(dzn_ref, da_ref, w_ref, dx_ref):
            dx_ref[...] = dx_of(dzn_ref, da_ref, w_ref)

        out, rode = _call(kern, name, (n // tm,), base_specs, pl.BlockSpec((tm, D), row), _sds((n, D), F32),
                          (dz_next, da, w), ("parallel",), rider=rider)
        return out if rider is None else (out, rode)

    xhat, rstd, g = ln

    def kern(*refs):
        dzn_ref, da_ref, w_ref, xhat_ref, rstd_ref, g_ref = refs[:6]
        dz_ref, dg_ref, db_ref = refs[-3:] if then is None else refs[-4:-1]

        @pl.when(pl.program_id(0) == 0)
        def _():
            dg_ref[...] = jnp.zeros_like(dg_ref)
            db_ref[...] = jnp.zeros_like(db_ref)

        dx = dx_of(dzn_ref, da_ref, w_ref)
        xh = xhat_ref[...]
        dg_ref[...] += _fold8(dx * xh)
        db_ref[...] += _fold8(dx)
        dz = _ln_bwd(dx * g_ref[...], xh, rstd_ref[...])
        dz_ref[...] = dz
        if then is not None:
            refs[-1][...] = lax.dot_general(dz.astype(BF16), refs[6][...], (((1,), (1,)), ((), ())),
                                            preferred_element_type=F32).astype(BF16)

    in_specs = base_specs + [pl.BlockSpec((tm, D), row), pl.BlockSpec((tm, 1), row), pl.BlockSpec((1, D), fix)]
    out_specs = [pl.BlockSpec((tm, D), row), pl.BlockSpec((8, D), fix), pl.BlockSpec((8, D), fix)]
    out_shape = [_sds((n, D), F32), _sds((8, D), F32), _sds((8, D), F32)]
    args = (dz_next, da, w, xhat, rstd, g)
    if then is not None:
        m3 = then.shape[0]
        in_specs.append(_weight_spec((m3, D), fix, True))
        out_specs.append(pl.BlockSpec((tm, m3), row))
        out_shape.append(_sds((n, m3), BF16))
        args += (then,)
    out, rode = _call(kern, name, (n // tm,), in_specs, out_specs, out_shape, args, ("arbitrary",), rider=rider)
    return out if rider is None else (out, rode)


def _loss_lnbwd(x, target, xhat, rstd, g, name, tm=512):
    n = x.shape[0]
    tm = min(tm, n)
    row = lambda i: (i, 0)
    fix = lambda i: (0, 0)

    def kern(x_ref, t_ref, xhat_ref, rstd_ref, g_ref, sq_ref, dz_ref, dg_ref, db_ref):
        @pl.when(pl.program_id(0) == 0)
        def _():
            sq_ref[...] = jnp.zeros_like(sq_ref)
            dg_ref[...] = jnp.zeros_like(dg_ref)
            db_ref[...] = jnp.zeros_like(db_ref)

        err = x_ref[...] - t_ref[...]
        sq_ref[...] += _fold8(err * err)
        dx = err * (1.0 / D)
        xh = xhat_ref[...]
        dg_ref[...] += _fold8(dx * xh)
        db_ref[...] += _fold8(dx)
        dz_ref[...] = _ln_bwd(dx * g_ref[...], xh, rstd_ref[...])

    return pl.pallas_call(
        kern, name=name, grid=(n // tm,),
        in_specs=[pl.BlockSpec((tm, D), row), pl.BlockSpec((tm, D), row), pl.BlockSpec((tm, D), row),
                  pl.BlockSpec((tm, 1), row), pl.BlockSpec((1, D), fix)],
        out_specs=[pl.BlockSpec((8, D), fix), pl.BlockSpec((tm, D), row), pl.BlockSpec((8, D), fix),
                   pl.BlockSpec((8, D), fix)],
        out_shape=[_sds((8, D), F32), _sds((n, D), F32), _sds((8, D), F32), _sds((8, D), F32)],
        compiler_params=_cp(("arbitrary",)))(x, target, xhat, rstd, g)


def _softmax_rows(s):
    s = s - jnp.max(s, axis=-1, keepdims=True)
    e = jnp.exp(s)
    return e / jnp.sum(e, axis=-1, keepdims=True)


def _attn_fwd(q, kv, bd, name, tm=512):
    n = q.shape[0]
    s_len = n // bd
    m_len = kv.shape[0] // bd
    tm = min(tm, s_len)
    nt = s_len // tm
    scale = X_HD ** -0.5

    def kern(q_ref, k_ref, v_ref, o_ref):
        for h in range(X_HEADS):
            cs = slice(h * X_HD, (h + 1) * X_HD)
            s = lax.dot_general(q_ref[:, cs], k_ref[:, cs], (((1,), (1,)), ((), ())), preferred_element_type=F32)
            p = _softmax_rows(s * scale)
            o_ref[:, cs] = jnp.dot(p.astype(BF16), v_ref[:, cs], preferred_element_type=F32).astype(BF16)

    return pl.pallas_call(
        kern, name=name, grid=(bd, nt),
        in_specs=[pl.BlockSpec((tm, D), lambda b, i: (b * nt + i, 0)),
                  pl.BlockSpec((m_len, D), lambda b, i: (b, 0)), pl.BlockSpec((m_len, D), lambda b, i: (b, 1))],
        out_specs=pl.BlockSpec((tm, D), lambda b, i: (b * nt + i, 0)),
        out_shape=_sds((n, D), BF16),
        compiler_params=_cp(("parallel", "parallel")))(q, kv, kv)


def _attn_bwd(q, kv, do, bd, name, tm=512, rider=None):
    n = q.shape[0]
    s_len = n // bd
    m_len = kv.shape[0] // bd
    tm = min(tm, s_len)
    nt = s_len // tm
    scale = X_HD ** -0.5

    def kern(q_ref, k_ref, v_ref, do_ref, dq_ref, dkv_ref):
        @pl.when(pl.program_id(1) == 0)
        def _():
            dkv_ref[...] = jnp.zeros_like(dkv_ref)

        for h in range(X_HEADS):
            cs = slice(h * X_HD, (h + 1) * X_HD)
            vs = slice(D + h * X_HD, D + (h + 1) * X_HD)
            qh, kh, vh, doh = q_ref[:, cs], k_ref[:, cs], v_ref[:, cs], do_ref[:, cs]
            s = lax.dot_general(qh, kh, (((1,), (1,)), ((), ())), preferred_element_type=F32)
            p = _softmax_rows(s * scale)
            pb = p.astype(BF16)
            dp = lax.dot_general(doh, vh, (((1,), (1,)), ((), ())), preferred_element_type=F32)
            dkv_ref[:, vs] += lax.dot_general(pb, doh, (((0,), (0,)), ((), ())), preferred_element_type=F32)
            ds = (p * (dp - jnp.sum(dp * p, axis=-1, keepdims=True)) * scale).astype(BF16)
            dq_ref[:, cs] = jnp.dot(ds, kh, preferred_element_type=F32).astype(BF16)
            dkv_ref[:, cs] += lax.dot_general(ds, qh, (((0,), (0,)), ((), ())), preferred_element_type=F32)

    out, rode = _call(
        kern, name, (bd, nt),
        [pl.BlockSpec((tm, D), lambda b, i: (b * nt + i, 0)),
         pl.BlockSpec((m_len, D), lambda b, i: (b, 0)), pl.BlockSpec((m_len, D), lambda b, i: (b, 1)),
         pl.BlockSpec((tm, D), lambda b, i: (b * nt + i, 0))],
        [pl.BlockSpec((tm, D), lambda b, i: (b * nt + i, 0)), pl.BlockSpec((m_len, 2 * D), lambda b, i: (b, 0))],
        [_sds((n, D), BF16), _sds((bd * m_len, 2 * D), F32)], (q, kv, kv, do), ("parallel", "arbitrary"), rider=rider)
    return out if rider is None else (out, rode)


C_AV, C_AG, C_BU, C_BV, C_CB, C_CC, C_CX = 0, 384, 768, 1024, 1280, 1664, 2048


def _taps_by_phase(offsets):
    by_phase = {}
    for k, o in enumerate(offsets):
        by_phase.setdefault(o % 8, []).append((o // 8, k))
    return sorted(by_phase.items())


def _window_rows(win_ref, r0, a, rows, cs):
    return win_ref[pl.ds(pl.multiple_of(r0 + 8 * a, 8), rows), cs]


def _conv_taps(win_ref, r0, w_ref, offsets):
    parts = []
    for cb in range(3):
        cs = slice(cb * 128, (cb + 1) * 128)
        acc = jnp.zeros((CHUNK, 128), F32)
        for b, taps in _taps_by_phase(offsets):
            rows = CHUNK if b == 0 else CHUNK + 8
            part = jnp.zeros((rows, 128), F32)
            for a, k in taps:
                part = part + _window_rows(win_ref, r0, a, rows, cs) * w_ref[k:k + 1, cs]
            acc = acc + (part if b == 0 else part[b:b + CHUNK, :])
        parts.append(acc)
    return jnp.concatenate(parts, axis=1)


def _tap_grads(acc_ref, win_ref, r0, d, offsets, shifted_ref):
    for cb in range(3):
        cs = slice(cb * 128, (cb + 1) * 128)
        padded = jnp.concatenate([jnp.zeros((8, 128), F32), d[:, cs], jnp.zeros((8, 128), F32)], axis=0)
        for b, taps in _taps_by_phase(offsets):
            if b == 0:
                rows, db = CHUNK, d[:, cs]
            else:
                rows = CHUNK + 8
                shifted_ref[...] = padded[8 - b:8 - b + rows, :]
                db = shifted_ref[...]
            for a, k in taps:
                acc_ref[k * 8:(k + 1) * 8, cs] += _fold8(db * _window_rows(win_ref, r0, a, rows, cs))


def _causal_offsets(halo, n_taps):
    return [halo - (n_taps - 1) + k for k in range(n_taps)]


def _anticausal_offsets(n_taps):
    return [n_taps - 1 - k for k in range(n_taps)]


def _head_of_lane():
    return lax.broadcasted_iota(jnp.int32, (1, D_B), 1) // HEAD


def _spatial_mix(wm_ref, vb, head):
    mixed = jnp.zeros((CHUNK, D_B), F32)
    for h in range(B_HEADS):
        mh = jnp.dot(wm_ref[h], vb, preferred_element_type=F32)
        mixed = jnp.where(head == h, mh, mixed)
    return mixed


def _mixer_fwd(proj, p, bd, name, rider=None):
    n = proj.shape[0]
    s_len = n // bd
    n_chunks = s_len // CHUNK

    def kern(proj_ref, caw_ref, cab_ref, lag_ref, lab_ref, lvg_ref, lvb_ref, wm_ref, bsx_ref, ccw_ref, cat_ref,
             ca_ref, gs_ref, ccs_ref):
        gs_ref[0:HALO_A, :] = jnp.zeros((HALO_A, D_A), F32)
        ccs_ref[0:HALO_C, :] = jnp.zeros((HALO_C, D_C), F32)
        head = _head_of_lane()

        def chunk(i, carry):
            r0 = pl.multiple_of(i * CHUNK, CHUNK)
            rows = pl.ds(r0, CHUNK)
            ld = lambda c0, w: proj_ref[rows, c0:c0 + w].astype(F32)
            gs_ref[pl.ds(r0 + HALO_A, CHUNK), :] = ld(C_AV, D_A) * jax.nn.sigmoid(ld(C_AG, D_A))
            ca = _conv_taps(gs_ref, r0, caw_ref, _causal_offsets(HALO_A, KA)) + cab_ref[...]
            ca_ref[rows, :] = ca
            lna = _ln(ca)[0] * lag_ref[...] + lab_ref[...]
            cat_ref[rows, 0:D_A] = (lna * jax.nn.sigmoid(lna)).astype(BF16)
            u = _gelu(ld(C_BU, D_B))
            v = _ln(_gelu(ld(C_BV, D_B)))[0] * lvg_ref[...] + lvb_ref[...]
            mixed = _spatial_mix(wm_ref, v.astype(BF16), head) + bsx_ref[...]
            cat_ref[rows, D_A:D_A + D_B] = (u * mixed).astype(BF16)
            ccs_ref[pl.ds(r0 + HALO_C, CHUNK), :] = ld(C_CC, D_C) * ld(C_CX, D_C)
            conv = _conv_taps(ccs_ref, r0, ccw_ref, _causal_offsets(HALO_C, KC))
            cat_ref[rows, D_A + D_B:D] = (ld(C_CB, D_C) * conv).astype(BF16)
            return carry

        lax.fori_loop(0, n_chunks, chunk, 0)

    fix2 = lambda b: (0, 0)
    args = [proj, p["caw"], p["cab"], p["lag"], p["lab"], p["lvg"], p["lvb"], p["wm"], p["bsx"], p["ccw"]]
    in_specs = [pl.BlockSpec((s_len, IN_W), lambda b: (b, 0))]
    for a in args[1:]:
        in_specs.append(pl.BlockSpec(a.shape, (lambda b: (0, 0, 0)) if a.ndim == 3 else fix2))
    res, rode = _call(
        kern, name, (bd,), in_specs,
        [pl.BlockSpec((s_len, D), lambda b: (b, 0)), pl.BlockSpec((s_len, D_A), lambda b: (b, 0))],
        [_sds((n, D), BF16), _sds((n, D_A), F32)], args, ("parallel",),
        scratch=[pltpu.VMEM((s_len + HALO_A, D_A), F32), pltpu.VMEM((s_len + HALO_C, D_C), F32)], rider=rider)
    return res if rider is None else (res, rode)


def _mixer_bwd(proj, dcat, ca, p, bd, name, rider=None):
    n = proj.shape[0]
    s_len = n // bd
    n_chunks = s_len // CHUNK

    def kern(proj_ref, dcat_ref, ca_ref, caw_ref, cab_ref, lag_ref, lab_ref, lvg_ref, lvb_ref, wm_ref, bsx_ref, ccw_ref,
             dproj_ref, dcaw_ref, dcab_ref, dlag_ref, dlab_ref, dlvg_ref, dlvb_ref, dws_ref, dbs_ref, dccw_ref,
             gs_ref, dcas_ref, ccs_ref, dcs_ref, a_caw, a_cab, a_lag, a_lab, a_lvg, a_lvb, a_ccw, shifted_ref):
        gs_ref[0:HALO_A, :] = jnp.zeros((HALO_A, D_A), F32)
        ccs_ref[0:HALO_C, :] = jnp.zeros((HALO_C, D_C), F32)
        dcas_ref[s_len:s_len + HALO_A, :] = jnp.zeros((HALO_A, D_A), F32)
        dcs_ref[s_len:s_len + HALO_C, :] = jnp.zeros((HALO_C, D_C), F32)
        for acc in (a_caw, a_cab, a_lag, a_lab, a_lvg, a_lvb, a_ccw, dws_ref, dbs_ref):
            acc[...] = jnp.zeros_like(acc)
        head = _head_of_lane()
        lane128 = lax.broadcasted_iota(jnp.int32, (1, CHUNK), 1)

        def pass1(i, carry):
            r0 = pl.multiple_of(i * CHUNK, CHUNK)
            rows = pl.ds(r0, CHUNK)
            ld = lambda c0, w: proj_ref[rows, c0:c0 + w].astype(F32)
            dld = lambda c0, w: dcat_ref[rows, c0:c0 + w].astype(F32)
            gs_ref[pl.ds(r0 + HALO_A, CHUNK), :] = ld(C_AV, D_A) * jax.nn.sigmoid(ld(C_AG, D_A))
            xh, rstd = _ln(ca_ref[rows, :])
            lna = xh * lag_ref[...] + lab_ref[...]
            sg = jax.nn.sigmoid(lna)
            dlna = dld(0, D_A) * (sg * (1.0 + lna * (1.0 - sg)))
            a_lag[...] += _fold8(dlna * xh)
            a_lab[...] += _fold8(dlna)
            dca = _ln_bwd(dlna * lag_ref[...], xh, rstd)
            dcas_ref[rows, :] = dca
            a_cab[...] += _fold8(dca)
            _tap_grads(a_caw, gs_ref, r0, dca, _causal_offsets(HALO_A, KA), shifted_ref)
            pu, pv = ld(C_BU, D_B), ld(C_BV, D_B)
            u, du_dpu = _gelu_and_grad(pu)
            gv, dgv_dpv = _gelu_and_grad(pv)
            vxh, vrstd = _ln(gv)
            v = vxh * lvg_ref[...] + lvb_ref[...]
            vb = v.astype(BF16)
            mixed = _spatial_mix(wm_ref, vb, head) + bsx_ref[...]
            dbo = dld(D_A, D_B)
            dproj_ref[rows, C_BU:C_BU + D_B] = (dbo * mixed * du_dpu).astype(BF16)
            dmixed = dbo * u
            dv = jnp.zeros((CHUNK, D_B), F32)
            bsum = jnp.zeros((CHUNK, CHUNK), F32)
            for h in range(B_HEADS):
                dmh = jnp.where(head == h, dmixed, 0.0)
                dmb = dmh.astype(BF16)
                dvh = lax.dot_general(wm_ref[h], dmb, (((0,), (0,)), ((), ())), preferred_element_type=F32)
                dv = jnp.where(head == h, dvh, dv)
                dws_ref[h] += lax.dot_general(dmb, vb, (((1,), (1,)), ((), ())), preferred_element_type=F32)
                bsum = bsum + jnp.where(lane128 == h, jnp.sum(dmh, axis=-1, keepdims=True), 0.0)
            dbs_ref[...] += bsum
            a_lvg[...] += _fold8(dv * vxh)
            a_lvb[...] += _fold8(dv)
            dgv = _ln_bwd(dv * lvg_ref[...], vxh, vrstd)
            dproj_ref[rows, C_BV:C_BV + D_B] = (dgv * dgv_dpv).astype(BF16)
            ccs_ref[pl.ds(r0 + HALO_C, CHUNK), :] = ld(C_CC, D_C) * ld(C_CX, D_C)
            conv = _conv_taps(ccs_ref, r0, ccw_ref, _causal_offsets(HALO_C, KC))
            dco = dld(D_A + D_B, D_C)
            dproj_ref[rows, C_CB:C_CB + D_C] = (dco * conv).astype(BF16)
            dconv = dco * ld(C_CB, D_C)
            dcs_ref[rows, :] = dconv
            _tap_grads(a_ccw, ccs_ref, r0, dconv, _causal_offsets(HALO_C, KC), shifted_ref)
            return carry

        lax.fori_loop(0, n_chunks, pass1, 0)

        def pass2(i, carry):
            r0 = pl.multiple_of(i * CHUNK, CHUNK)
            rows = pl.ds(r0, CHUNK)
            ld = lambda c0, w: proj_ref[rows, c0:c0 + w].astype(F32)
            dg = _conv_taps(dcas_ref, r0, caw_ref, _anticausal_offsets(KA))
            pa = ld(C_AV, D_A)
            sg = jax.nn.sigmoid(ld(C_AG, D_A))
            dproj_ref[rows, C_AV:C_AV + D_A] = (dg * sg).astype(BF16)
            dproj_ref[rows, C_AG:C_AG + D_A] = (dg * pa * sg * (1.0 - sg)).astype(BF16)
            dcc = _conv_taps(dcs_ref, r0, ccw_ref, _anticausal_offsets(KC))
            dproj_ref[rows, C_CC:C_CC + D_C] = (dcc * ld(C_CX, D_C)).astype(BF16)
            dproj_ref[rows, C_CX:C_CX + D_C] = (dcc * ld(C_CC, D_C)).astype(BF16)
            return carry

        lax.fori_loop(0, n_chunks, pass2, 0)

        for k in range(KA):
            dcaw_ref[k:k + 1, :] = jnp.sum(a_caw[k * 8:(k + 1) * 8, :], axis=0, keepdims=True)
        dcaw_ref[KA:KA + 1, :] = jnp.zeros((1, D_A), F32)
        for k in range(8):
            if k < KC:
                dccw_ref[k:k + 1, :] = jnp.sum(a_ccw[k * 8:(k + 1) * 8, :], axis=0, keepdims=True)
            else:
                dccw_ref[k:k + 1, :] = jnp.zeros((1, D_C), F32)
        dcab_ref[...] = a_cab[...]
        dlag_ref[...] = a_lag[...]
        dlab_ref[...] = a_lab[...]
        dlvg_ref[...] = a_lvg[...]
        dlvb_ref[...] = a_lvb[...]

    fix2 = lambda b: (0, 0)
    args = [proj, dcat, ca, p["caw"], p["cab"], p["lag"], p["lab"], p["lvg"], p["lvb"], p["wm"], p["bsx"], p["ccw"]]
    once = pl.Buffered(1)
    in_specs = [pl.BlockSpec((s_len, IN_W), lambda b: (b, 0), pipeline_mode=once),
                pl.BlockSpec((s_len, D), lambda b: (b, 0), pipeline_mode=once),
                pl.BlockSpec((s_len, D_A), lambda b: (b, 0), pipeline_mode=once)]
    for a in args[3:]:
        in_specs.append(pl.BlockSpec(a.shape, (lambda b: (0, 0, 0)) if a.ndim == 3 else fix2))

    def per_seq(*shape):
        nd = len(shape)
        return (pl.BlockSpec((None,) + shape, lambda b: (b,) + (0,) * nd), _sds((bd,) + shape, F32))

    outs = [(pl.BlockSpec((s_len, IN_W), lambda b: (b, 0), pipeline_mode=once), _sds((n, IN_W), BF16)),
            per_seq(32, D_A), per_seq(8, D_A), per_seq(8, D_A), per_seq(8, D_A), per_seq(8, D_B), per_seq(8, D_B),
            per_seq(B_HEADS, CHUNK, CHUNK), per_seq(CHUNK, CHUNK), per_seq(8, D_C)]
    res, rode = _call(
        kern, name, (bd,), in_specs, [o[0] for o in outs], [o[1] for o in outs], args, ("parallel",),
        scratch=[pltpu.VMEM((s_len + HALO_A, D_A), F32), pltpu.VMEM((s_len + HALO_A, D_A), F32),
                 pltpu.VMEM((s_len + HALO_C, D_C), F32), pltpu.VMEM((s_len + HALO_C, D_C), F32),
                 pltpu.VMEM((KA * 8, D_A), F32), pltpu.VMEM((8, D_A), F32), pltpu.VMEM((8, D_A), F32),
                 pltpu.VMEM((8, D_A), F32), pltpu.VMEM((8, D_B), F32), pltpu.VMEM((8, D_B), F32),
                 pltpu.VMEM((KC * 8, D_C), F32), pltpu.VMEM((CHUNK + 8, 128), F32)],
        rider=rider)
    return res if rider is None else (res, rode)


def _adamw_update(w, m, v, g):
    mn = B1 * m + (1.0 - B1) * g
    vn = B2 * v + (1.0 - B2) * jnp.square(g)
    m_hat = mn / (1.0 - B1 ** STEP)
    v_hat = vn / (1.0 - B2 ** STEP)
    return -LR * (m_hat / (jnp.sqrt(v_hat) + ADAM_EPS) + WD * w), mn, vn


def _adamw_layers(w, m, v, own, remote, name):
    shape = w.shape
    depth, c = shape[0], shape[-1]
    r = math.prod(shape[1:-1])
    tr = 256 if r % 256 == 0 else r
    nt = r // tr

    def kern(*refs):
        w_ref, m_ref, v_ref = refs[:3]
        own_refs, rem_refs = refs[3:3 + depth], refs[3 + depth:3 + 4 * depth]
        go_ref, d_ref, mo_ref, vo_ref = refs[3 + 4 * depth:]
        for lp in range(depth):
            @pl.when(pl.program_id(0) == lp)
            def _(lp=lp):
                g = own_refs[lp][...]
                for j in range(3):
                    g = g + rem_refs[3 * lp + j][...].astype(F32)
                go_ref[...] = g
                d_ref[...], mo_ref[...], vo_ref[...] = _adamw_update(w_ref[...], m_ref[...], v_ref[...], g)

    def rows_of(lp):
        return lambda l, i: jnp.where(l == lp, i, jnp.where(l < lp, 0, nt - 1))

    spec = pl.BlockSpec((None, tr, c), lambda l, i: (l, i, 0))
    own_specs = [pl.BlockSpec((tr, c), functools.partial(lambda f, l, i: (f(l, i), 0), rows_of(lp)))
                 for lp in range(depth)]
    rem_specs = [pl.BlockSpec((None, tr, c), functools.partial(lambda f, j, l, i: (j, f(l, i), 0), rows_of(lp), j))
                 for lp in range(depth) for j in range(3)]
    as3d = lambda a: a.reshape(depth, r, c)
    outs = pl.pallas_call(
        kern, name=name, grid=(depth, nt), in_specs=[spec] * 3 + own_specs + rem_specs, out_specs=[spec] * 4,
        out_shape=[_sds((depth, r, c), F32)] * 4, compiler_params=_cp(("arbitrary", "arbitrary")))(
            as3d(w), as3d(m), as3d(v), *[o.reshape(r, c) for o in own],
            *[x.reshape(3, r, c) for x in remote for _ in range(3)])
    return [o.reshape(shape) for o in outs]


def _adamw(w, m, v, g_parts, name):
    shape = w.shape
    c = shape[-1]
    r = math.prod(shape[:-1])
    as2d = lambda a: a.reshape(r, c)
    tr = 512 if r % 512 == 0 else r
    slots = [(a.reshape(a.shape[0], r, c), p) for a in g_parts for p in range(a.shape[0])]
    n_g = len(slots)

    def kern(*refs):
        w_ref, m_ref, v_ref = refs[:3]
        g_refs = refs[3:3 + n_g]
        go_ref, d_ref, mo_ref, vo_ref = refs[3 + n_g:]
        g = g_refs[0][...].astype(F32)
        for gr in g_refs[1:]:
            g = g + gr[...].astype(F32)
        go_ref[...] = g
        d_ref[...], mo_ref[...], vo_ref[...] = _adamw_update(w_ref[...], m_ref[...], v_ref[...], g)

    spec = pl.BlockSpec((tr, c), lambda i: (i, 0))
    g_specs = [pl.BlockSpec((None, tr, c), functools.partial(lambda p, i: (p, i, 0), p)) for _, p in slots]
    outs = pl.pallas_call(
        kern, name=name, grid=(r // tr,), in_specs=[spec] * 3 + g_specs, out_specs=[spec] * 4,
        out_shape=[_sds((r, c), F32)] * 4, compiler_params=_cp(("parallel",)))(
            as2d(w), as2d(m), as2d(v), *[a for a, _ in slots])
    return [o.reshape(shape) for o in outs]


def _place():
    x, y, c = lax.axis_index("x"), lax.axis_index("y"), lax.axis_index("c")
    other_chips = [(1 - x, y), (x, 1 - y), (1 - x, 1 - y)]
    return x, y, c, other_chips


def _gather_rider(arrays):
    n_arr = len(arrays)

    def parts(ins, outs, sems):
        send_sems, recv_sems, local_sems = sems
        x, y, c, chips = _place()
        me, sibling = (x, y, c), (x, y, 1 - c)
        slot = lambda px, py, pc: 4 * px + 2 * py + pc

        def copy(a, k, block, to, from_input=False):
            dst = outs[a].at[slot(*block)]
            return pltpu.make_async_remote_copy(
                src_ref=ins[a] if from_input else dst, dst_ref=dst, send_sem=send_sems.at[k, a],
                recv_sem=recv_sems.at[k, a], device_id=to, device_id_type=MESH)

        mine = [pltpu.make_async_copy(ins[a], outs[a].at[slot(*me)], local_sems.at[a]) for a in range(n_arr)]
        first = []
        for a in range(n_arr):
            first.append(copy(a, 0, me, sibling, True))
            first += [copy(a, 1 + j, me, (*chip, c), True) for j, chip in enumerate(chips)]
        return copy, mine, first, me, sibling, chips, c

    def start(ins, outs, sems):
        _, mine, first, *_ = parts(ins, outs, sems)
        for cp in mine + first:
            cp.start()

    def finish(ins, outs, sems):
        copy, mine, first, me, sibling, chips, c = parts(ins, outs, sems)
        passed = []
        for j, chip in enumerate(chips):
            for a in range(n_arr):
                copy(a, 1 + j, (*chip, c), me).wait_recv()
                passed.append(copy(a, 4 + j, (*chip, c), sibling))
                passed[-1].start()
        for a in range(n_arr):
            copy(a, 0, sibling, me).wait_recv()
            for j, chip in enumerate(chips):
                copy(a, 4 + j, (*chip, 1 - c), me).wait_recv()
        for cp in first + passed:
            cp.wait_send()
        for cp in mine:
            cp.wait()

    return _Rider(list(arrays), [_sds((N_DEV,) + a.shape, a.dtype) for a in arrays],
                  [pltpu.SemaphoreType.DMA((7, n_arr)), pltpu.SemaphoreType.DMA((7, n_arr)),
                   pltpu.SemaphoreType.DMA((n_arr,))], start, finish)


def _exchange_rider(arrays):
    n_arr = len(arrays)

    def copies(ins, outs, sems):
        send_sems, recv_sems = sems
        x, y, c, _ = _place()
        return [pltpu.make_async_remote_copy(
            src_ref=ins[a].at[:, 1 - c], dst_ref=outs[a], send_sem=send_sems.at[a], recv_sem=recv_sems.at[a],
            device_id=(x, y, 1 - c), device_id_type=MESH) for a in range(n_arr)]

    def start(ins, outs, sems):
        for cp in copies(ins, outs, sems):
            cp.start()

    def finish(ins, outs, sems):
        for cp in copies(ins, outs, sems):
            cp.wait()

    return _Rider(list(arrays), [_sds(a.shape[:1] + a.shape[2:], a.dtype) for a in arrays],
                  [pltpu.SemaphoreType.DMA((n_arr,)), pltpu.SemaphoreType.DMA((n_arr,))], start, finish)


def _pair_add(mine, theirs, core, name):
    _, _, r, c = mine.shape
    tr = 512 if r % 512 == 0 else r

    def kern(core_ref, a_ref, b_ref, o_ref, ob_ref):
        s = a_ref[...] + b_ref[...]
        o_ref[...] = s
        ob_ref[...] = s.astype(BF16)

    row = lambda t, i, core_ref: (t, i, 0)
    return pl.pallas_call(
        kern, name=name,
        grid_spec=pltpu.PrefetchScalarGridSpec(
            num_scalar_prefetch=1, grid=(4, r // tr),
            in_specs=[pl.BlockSpec((None, None, tr, c), lambda t, i, core_ref: (t, core_ref[0], i, 0)),
                      pl.BlockSpec((None, tr, c), row)],
            out_specs=[pl.BlockSpec((None, tr, c), row), pl.BlockSpec((None, tr, c), row)]),
        out_shape=[_sds((4, r, c), F32), _sds((4, r, c), BF16)],
        compiler_params=_cp(("parallel", "parallel")))(core, mine, theirs)


def _scatter_rider(sums, sums_bf16):
    n_arr = len(sums)

    def copies(ins, outs, sems):
        send_sems, recv_sems, local_sems = sems
        x, y, c, chips = _place()
        own = [pltpu.make_async_copy(ins[a].at[2 * x + y], outs[a], local_sems.at[a]) for a in range(n_arr)]
        remote = []
        for a in range(n_arr):
            for j, (px, py) in enumerate(chips):
                remote.append(pltpu.make_async_remote_copy(
                    src_ref=ins[n_arr + a].at[2 * px + py], dst_ref=outs[n_arr + a].at[j], send_sem=send_sems.at[j, a],
                    recv_sem=recv_sems.at[j, a], device_id=(px, py, c), device_id_type=MESH))
        return own + remote

    def start(ins, outs, sems):
        for cp in copies(ins, outs, sems):
            cp.start()

    def finish(ins, outs, sems):
        for cp in copies(ins, outs, sems):
            cp.wait()

    return _Rider(list(sums) + list(sums_bf16),
                  [_sds(a.shape[1:], a.dtype) for a in sums] + [_sds((3,) + a.shape[1:], a.dtype) for a in sums_bf16],
                  [pltpu.SemaphoreType.DMA((3, n_arr)), pltpu.SemaphoreType.DMA((3, n_arr)),
                   pltpu.SemaphoreType.DMA((n_arr,))], start, finish)


def _mixer_params(conv_a_w, conv_a_b, ln_a_g, ln_a_b, ln_v_g, ln_v_b, w_s, b_s, conv_c_w):
    causal = jnp.tril(jnp.ones((CHUNK, CHUNK), dtype=bool))
    row = lambda a: a.reshape(1, -1)
    return dict(
        caw=jnp.pad(conv_a_w, ((0, 32 - KA), (0, 0))), cab=row(conv_a_b), lag=row(ln_a_g), lab=row(ln_a_b),
        lvg=row(ln_v_g), lvb=row(ln_v_b), wm=jnp.where(causal[None], w_s, 0.0).astype(BF16),
        bsx=jnp.repeat(b_s.T, HEAD, axis=1), ccw=jnp.pad(conv_c_w, ((0, 8 - KC), (0, 0))))


class _Schedule:
    def __init__(self, big=None):
        self.big = big

    def weights(self, l):
        return {k: v[l] for k, v in self.big.items()}

    def rider(self, stage, l):
        return None

    def rode(self, stage, l, results):
        pass

    def note_grads(self, l, grads):
        pass

    def finish(self):
        pass


def _local_step(x, mem, target, sched, small, bd):
    row = lambda a: a.reshape(1, -1)

    def ride(stage, l, fn, *args, **kw):
        rider = sched.rider(stage, l)
        res = fn(*args, rider=rider, **kw)
        if rider is not None:
            res, results = res
            sched.rode(stage, l, results)
        return res

    saved = []
    for l in range(DEPTH):
        big = sched.weights(l)
        mp = _mixer_params(big["conv_a_w"], *[small[k][l] for k in ("conv_a_b", "ln_a_g", "ln_a_b", "ln_v_g", "ln_v_b",
                                                                    "w_s", "b_s")], big["conv_c_w"])
        proj = ride("in_proj", l, _mm_nn, x, big["w_in"], BF16, f"in_proj_{l}")
        cat, ca = ride("mixer_fwd", l, _mixer_fwd, proj, mp, bd, f"mixer_fwd_{l}")
        x1, xh1, rs1, q = ride("out_proj_ln1", l, _mm_res_ln, cat, big["w_out"], x, row(small["ln1_g"][l]),
                               row(small["ln1_b"][l]), f"out_proj_ln1_q_{l}", then=big["w_q"])
        kv = _mm_nn(mem, big["w_kv"], BF16, f"kv_proj_{l}")
        o = _attn_fwd(q, kv, bd, f"attn_fwd_{l}")
        x2, xh2, rs2, h = ride("ff1", l, _mm_res_ln, o, big["w_o"], x1, row(small["ln2_g"][l]), row(small["ln2_b"][l]),
                               f"o_proj_ln2_ff1_{l}", then=big["w_ff1"])
        x3, xh3, rs3 = ride("ff2_ln3", l, _mm_res_ln, h, big["w_ff2"], x2, row(small["ln3_g"][l]),
                            row(small["ln3_b"][l]), f"ff2_ln3_{l}", relu2=True)
        saved.append(dict(mp=mp, x=x, proj=proj, cat=cat, ca=ca, x1=x1, xh1=xh1, rs1=rs1, q=q, kv=kv, o=o, x2=x2, xh2=xh2,
                          rs2=rs2, h=h, xh3=xh3, rs3=rs3))
        x = x3

    grads = {k: [None] * DEPTH for k in WEIGHTS}
    s = saved[-1]
    sq, dz3, dg, db = _loss_lnbwd(x, target, s["xh3"], s["rs3"], row(small["ln3_g"][DEPTH - 1]), "loss_ln3_bwd")
    grad_x = None
    causal = jnp.tril(jnp.ones((CHUNK, CHUNK), dtype=bool))
    for l in reversed(range(DEPTH)):
        s = saved[l]
        big = sched.weights(l)
        grads["ln3_g"][l], grads["ln3_b"][l] = jnp.sum(dg, axis=0), jnp.sum(db, axis=0)
        dh = ride("ff2_bwd", l, _mm_nt, dz3, big["w_ff2"], BF16, f"ff2_bwd_{l}", relu2_grad_of=s["h"], tn=D_FF)
        grads["w_ff2"][l] = _mm_tn(s["h"], dz3, f"ff2_wgrad_{l}", relu2=True, t1=2048)
        grads["w_ff1"][l] = _mm_tn_by_owner(s["x2"], dh, f"ff1_wgrad_{l}")
        sched.note_grads(l, {k: grads[k][l] for k in ("w_ff1", "w_ff2")})
        dz2, dg, db, do = ride("ff1_bwd_ln2", l, _bwd_in, dz3, dh, big["w_ff1"], f"ff1_bwd_ln2_o_bwd_{l}",
                               ln=(s["xh2"], s["rs2"], row(small["ln2_g"][l])), then=big["w_o"])
        grads["ln2_g"][l], grads["ln2_b"][l] = jnp.sum(dg, axis=0), jnp.sum(db, axis=0)
        grads["w_o"][l] = _mm_tn(s["o"], dz2, f"o_proj_wgrad_{l}")
        dq, dkv = ride("attn_bwd", l, _attn_bwd, s["q"], s["kv"], do, bd, f"attn_bwd_{l}")
        grads["w_q"][l] = _mm_tn(s["x1"], dq, f"q_wgrad_{l}")
        grads["w_kv"][l] = _mm_tn_by_owner(mem, dkv, f"kv_wgrad_{l}")
        sched.note_grads(l, {k: grads[k][l] for k in ("w_o", "w_q", "w_kv")})
        dz1, dg, db, dcat = ride("q_bwd_ln1", l, _bwd_in, dz2, dq, big["w_q"], f"q_bwd_ln1_out_bwd_{l}",
                                 ln=(s["xh1"], s["rs1"], row(small["ln1_g"][l])), then=big["w_out"])
        grads["ln1_g"][l], grads["ln1_b"][l] = jnp.sum(dg, axis=0), jnp.sum(db, axis=0)
        grads["w_out"][l] = _mm_tn(s["cat"], dz1, f"out_proj_wgrad_{l}")
        sched.note_grads(l, {"w_out": grads["w_out"][l]})
        (dproj, dcaw, dcab, dlag, dlab, dlvg, dlvb, dws, dbs, dccw) = ride(
            "mixer_bwd", l, _mixer_bwd, s["proj"], dcat, s["ca"], s["mp"], bd, f"mixer_bwd_{l}")
        grads["conv_a_w"][l] = jnp.sum(dcaw, axis=0)[:KA]
        grads["conv_a_b"][l] = jnp.sum(dcab, axis=(0, 1))
        grads["ln_a_g"][l] = jnp.sum(dlag, axis=(0, 1))
        grads["ln_a_b"][l] = jnp.sum(dlab, axis=(0, 1))
        grads["ln_v_g"][l] = jnp.sum(dlvg, axis=(0, 1))
        grads["ln_v_b"][l] = jnp.sum(dlvb, axis=(0, 1))
        grads["w_s"][l] = jnp.where(causal[None], jnp.sum(dws, axis=0), 0.0)
        grads["b_s"][l] = jnp.sum(dbs, axis=0)[:, :B_HEADS].T
        grads["conv_c_w"][l] = jnp.sum(dccw, axis=0)[:KC]
        grads["w_in"][l] = _mm_tn(s["x"], dproj, f"in_proj_wgrad_{l}")
        sched.note_grads(l, {k: v[l] for k, v in grads.items() if k not in ("w_ff1", "w_ff2")})
        if l > 0:
            p = saved[l - 1]
            dz3, dg, db = _bwd_in(dz1, dproj, big["w_in"], f"in_proj_bwd_ln3_{l}",
                                  ln=(p["xh3"], p["rs3"], row(small["ln3_g"][l - 1])))
        else:
            grad_x = ride("in_proj_bwd", 0, _bwd_in, dz1, dproj, big["w_in"], "in_proj_bwd_0")
    sched.finish()
    return sq, grad_x, grads


WEIGHTS = ("w_in", "conv_a_w", "conv_a_b", "ln_a_g", "ln_a_b", "ln_v_g", "ln_v_b", "w_s", "b_s", "conv_c_w", "w_out",
           "ln1_g", "ln1_b", "w_q", "w_kv", "w_o", "ln2_g", "ln2_b", "w_ff1", "w_ff2", "ln3_g", "ln3_b")
COL_SHARDED = ("w_in", "w_kv", "w_ff1")
ROW_SHARDED = ("w_out", "w_q", "w_o", "w_ff2")
BIG = COL_SHARDED + ROW_SHARDED
REPLICATED = tuple(k for k in WEIGHTS if k not in BIG and k not in ("conv_a_w", "conv_c_w"))
PACK_LANES = 128


CONV_ROWS = 32 + 8
GATHER_LAYER0 = {"first": ("w_in", "conv"), "in_proj": ("w_out", "w_q", "w_kv", "w_o"), "mixer_fwd": ("w_ff1", "w_ff2")}
GATHER_NEXT0 = {"out_proj_ln1": ("w_in", "conv", "w_out"), "ff1": ("w_ff1", "w_kv"), "ff2_ln3": ("w_ff2", "w_q", "w_o")}
GATHER_NEXT = {"in_proj": ("w_in", "conv"), "mixer_fwd": ("w_out", "w_q", "w_o"), "ff1": ("w_ff1", "w_kv"),
               "ff2_ln3": ("w_ff2",)}
GRADS_EARLY = ("w_ff1", "w_ff2")
GRADS_LATE = ("w_in", "w_kv", "w_out", "w_q", "w_o", "conv")
GRADS_MID0, GRADS_OUT0, GRADS_LAST0 = ("w_kv", "w_q", "w_o"), ("w_out",), ("w_in", "conv")


def _gathered_to_full(g, col_sharded):
    _, a, b = g.shape
    if col_sharded:
        return g.transpose(1, 0, 2).reshape(a, N_DEV * b)
    return g.reshape(N_DEV * a, b)


def _full_to_owner_major(g, col_sharded):
    if g.ndim == 4:
        return g
    a, b = g.shape
    if col_sharded:
        return g.reshape(a, 4, 2, b // N_DEV).transpose(1, 2, 0, 3)
    return g.reshape(4, 2, a // N_DEV, b)


def _conv_pack(conv_a, conv_c):
    pad = lambda a, rows: jnp.pad(a, [(0, 0)] * (a.ndim - 2) + [(0, rows - a.shape[-2]), (0, 0)])
    return jnp.concatenate([pad(conv_a, 32), pad(conv_c, 8)], axis=-2)


def _conv_unpack(packed):
    return packed[..., :KA, :], packed[..., 32:32 + KC, :]


class _Overlapped(_Schedule):
    def __init__(self, shards_bf16, conv_shards, core):
        self.shards, self.conv_shards, self.core = shards_bf16, conv_shards, core
        self.full = {l: {} for l in range(DEPTH)}
        self.grads = {l: {} for l in range(DEPTH)}
        self.owner_major = {}
        self.from_sibling = {}
        self.scattering = None
        self.own, self.remote = {}, {}
        self.replicated = None

    def _gather(self, l, names):
        return _gather_rider([self.conv_shards[l] if k == "conv" else self.shards[k][l] for k in names])

    def _store(self, l, names, gathered):
        for k, g in zip(names, gathered):
            if k == "conv":
                self.full[l]["conv_a_w"], self.full[l]["conv_c_w"] = _conv_unpack(_gathered_to_full(g, True))
            else:
                self.full[l][k] = _gathered_to_full(g, k in COL_SHARDED)

    def weights(self, l):
        if l == 0 and not self.full[0]:
            names = GATHER_LAYER0["first"]
            self._store(0, names, _ride_alone(self._gather(0, names), "weights_all_gather_first"))
        return self.full[l]

    def note_grads(self, l, grads):
        self.grads[l].update(grads)

    def _owner_major(self, l, k):
        if (l, k) not in self.owner_major:
            g = self.grads[l]
            if k == "conv":
                self.owner_major[(l, k)] = _full_to_owner_major(_conv_pack(g["conv_a_w"], g["conv_c_w"]), True)
            else:
                self.owner_major[(l, k)] = _full_to_owner_major(g[k], k in COL_SHARDED)
        return self.owner_major[(l, k)]

    def _exchange(self, l, names):
        return _exchange_rider([self._owner_major(l, k) for k in names])

    def _scatter(self, groups):
        sums, sums_bf16, self.scattering = [], [], []
        for l, names in groups:
            for k, r in zip(names, self.from_sibling.pop((l, names))):
                s, sb = _pair_add(self._owner_major(l, k), r, self.core, f"grad_pair_add_{l}_{k}")
                sums.append(s)
                sums_bf16.append(sb)
                self.scattering.append((l, k))
        return _scatter_rider(sums, sums_bf16)

    def _scattered(self, results):
        n = len(self.scattering)
        for i, key in enumerate(self.scattering):
            self.own[key], self.remote[key] = results[i], results[n + i]

    def rider(self, stage, l):
        if l == 0 and stage in ("in_proj", "mixer_fwd"):
            return self._gather(0, GATHER_LAYER0[stage])
        gather_next = GATHER_NEXT0 if l == 0 else GATHER_NEXT
        if stage in gather_next and l + 1 < DEPTH:
            return self._gather(l + 1, gather_next[stage])
        if stage == "ff2_bwd" and l + 1 < DEPTH:
            return self._exchange(l + 1, GRADS_LATE)
        if stage == "ff1_bwd_ln2" and l + 1 < DEPTH:
            return self._scatter([(l + 1, GRADS_LATE)])
        if stage == "attn_bwd":
            return self._exchange(l, GRADS_EARLY)
        if stage == "q_bwd_ln1" and l == 0:
            return self._exchange(0, GRADS_MID0)
        if stage == "mixer_bwd" and l == 0:
            return _join_riders([self._scatter([(0, GRADS_EARLY), (0, GRADS_MID0)]), self._exchange(0, GRADS_OUT0)])
        if stage == "mixer_bwd":
            return self._scatter([(l, GRADS_EARLY)])
        if stage == "in_proj_bwd":
            packed = _pack_rows([jnp.stack([self.grads[i][k] for i in range(DEPTH)]) for k in REPLICATED])
            return _gather_rider([packed])
        return None

    def rode(self, stage, l, results):
        gather_next = GATHER_NEXT0 if l == 0 else GATHER_NEXT
        if l == 0 and stage in ("in_proj", "mixer_fwd"):
            self._store(0, GATHER_LAYER0[stage], results)
        elif stage in gather_next:
            self._store(l + 1, gather_next[stage], results)
        elif stage == "ff2_bwd":
            self.from_sibling[(l + 1, GRADS_LATE)] = results
        elif stage == "ff1_bwd_ln2":
            self._scattered(results)
        elif stage == "attn_bwd":
            self.from_sibling[(l, GRADS_EARLY)] = results
        elif stage == "q_bwd_ln1":
            self.from_sibling[(0, GRADS_MID0)] = results
        elif stage == "mixer_bwd" and l == 0:
            n = 2 * len(self.scattering)
            self._scattered(results[:n])
            self.from_sibling[(0, GRADS_OUT0)] = results[n:]
        elif stage == "mixer_bwd":
            self._scattered(results)
        elif stage == "in_proj_bwd":
            self.replicated = results[0]

    def finish(self):
        self.from_sibling[(0, GRADS_LAST0)] = _ride_alone(self._exchange(0, GRADS_LAST0), "grad_pair_exchange_last")
        self._scattered(_ride_alone(self._scatter([(0, GRADS_OUT0), (0, GRADS_LAST0)]), "grad_chip_scatter_last"))


def _pack_rows(parts):
    flat = jnp.concatenate([p.reshape(-1, PACK_LANES) for p in parts], axis=0)
    return jnp.pad(flat, ((0, -flat.shape[0] % 8), (0, 0)))


def _unpack_rows(packed, like):
    out, r = [], 0
    for p in like:
        n = p.size // PACK_LANES
        out.append(packed[r:r + n].reshape(p.shape))
        r += n
    return out


def kernel(x, mem, w_in, conv_a_w, conv_a_b, ln_a_g, ln_a_b, ln_v_g, ln_v_b, w_s, b_s, conv_c_w, w_out, ln1_g, ln1_b, w_q, w_kv, w_o, ln2_g, ln2_b, w_ff1, w_ff2, ln3_g, ln3_b, loss_target, m_w_in, m_conv_a_w, m_conv_a_b, m_ln_a_g, m_ln_a_b, m_ln_v_g, m_ln_v_b, m_w_s, m_b_s, m_conv_c_w, m_w_out, m_ln1_g, m_ln1_b, m_w_q, m_w_kv, m_w_o, m_ln2_g, m_ln2_b, m_w_ff1, m_w_ff2, m_ln3_g, m_ln3_b, v_w_in, v_conv_a_w, v_conv_a_b, v_ln_a_g, v_ln_a_b, v_ln_v_g, v_ln_v_b, v_w_s, v_b_s, v_conv_c_w, v_w_out, v_ln1_g, v_ln1_b, v_w_q, v_w_kv, v_w_o, v_ln2_g, v_ln2_b, v_w_ff1, v_w_ff2, v_ln3_g, v_ln3_b):
    given = dict(locals())
    w = {k: given[k] for k in WEIGHTS}
    mom = {k: given["m_" + k] for k in WEIGHTS}
    var = {k: given["v_" + k] for k in WEIGHTS}
    bd, s_len, _ = x.shape
    core = lax.axis_index("c").astype(jnp.int32).reshape(1)

    conv_pack = lambda d: _conv_pack(d["conv_a_w"], d["conv_c_w"])
    sched = _Overlapped({k: w[k].astype(BF16) for k in BIG}, conv_pack(w), core)
    sq, grad_x, grads = _local_step(x.reshape(bd * s_len, D), mem.reshape(-1, D), loss_target.reshape(bd * s_len, D),
                                    sched, {k: w[k] for k in REPLICATED}, bd)
    loss = lax.psum(0.5 * jnp.sum(sq) / D, ("x", "y", "c"))

    out = {}
    for k in BIG + ("conv",):
        own = [sched.own[(l, k)] for l in range(DEPTH)]
        remote = [sched.remote[(l, k)] for l in range(DEPTH)]
        if k == "conv":
            conv_out = _adamw_layers(conv_pack(w), conv_pack(mom), conv_pack(var), own, remote, "adamw_conv")
            unpacked = [_conv_unpack(o) for o in conv_out]
            out["conv_a_w"], out["conv_c_w"] = [u[0] for u in unpacked], [u[1] for u in unpacked]
        else:
            out[k] = _adamw_layers(w[k], mom[k], var[k], own, remote, f"adamw_{k}")

    rep_out = _adamw(_pack_rows([w[k] for k in REPLICATED]), _pack_rows([mom[k] for k in REPLICATED]),
                     _pack_rows([var[k] for k in REPLICATED]), [sched.replicated], "adamw_replicated")
    for i, o in enumerate(rep_out):
        for k, piece in zip(REPLICATED, _unpack_rows(o, [w[k] for k in REPLICATED])):
            out.setdefault(k, [None] * 4)[i] = piece

    res = [loss, grad_x.reshape(bd, s_len, D)]
    for i in range(4):
        res += [out[k][i] for k in WEIGHTS]
    return tuple(res)
```

```python
import functools
import math

import jax
import jax.numpy as jnp
from jax import lax
from jax.experimental import pallas as pl
from jax.experimental.pallas import tpu as pltpu

F32 = jnp.float32
BF16 = jnp.bfloat16

DEPTH = 4
D = 1024
D_A, D_B, D_C = 384, 256, 384
HEAD = 64
B_HEADS = 4
CHUNK = 128
KA, KC = 31, 3
HALO_A, HALO_C = 32, 8
IN_W = 2 * D_A + 2 * D_B + 3 * D_C
X_HEADS = 4
X_HD = D // X_HEADS
D_FF = 4 * D
EPS = 1e-5
ALPHA = (2.0 * DEPTH) ** 0.25
LR, B1, B2, ADAM_EPS, WD, STEP = 0.001, 0.9, 0.999, 1e-08, 0.01, 10
INV_SQRT2 = 0.7071067811865476
INV_SQRT_2PI = 0.3989422804014327
N_DEV = 8
VMEM_LIMIT = 56 * 1024 * 1024
MESH = pl.DeviceIdType.MESH
ANY = pl.BlockSpec(memory_space=pl.ANY)


def _cp(sem=None):
    return pltpu.CompilerParams(dimension_semantics=sem, vmem_limit_bytes=VMEM_LIMIT)


def _sds(shape, dtype):
    return jax.ShapeDtypeStruct(tuple(shape), dtype)


class _Rider:
    def __init__(self, arrays, out_shape, sems, start, finish):
        self.arrays, self.out_shape, self.sems, self.start, self.finish = arrays, out_shape, sems, start, finish


def _call(kern, name, grid, in_specs, out_specs, out_shape, args, sem, scratch=(), rider=None):
    single = not isinstance(out_shape, (list, tuple))
    out_specs_l = [out_specs] if single else list(out_specs)
    out_shape_l = [out_shape] if single else list(out_shape)
    if rider is None:
        res = pl.pallas_call(kern, name=name, grid=grid, in_specs=in_specs, out_specs=out_specs_l,
                             out_shape=out_shape_l, scratch_shapes=list(scratch), compiler_params=_cp(sem))(*args)
        return (res[0] if single else list(res)), None
    n_in, n_out, n_scr = len(args), len(out_shape_l), len(scratch)
    n_rin, n_rout = len(rider.arrays), len(rider.out_shape)

    def body(*refs):
        ins, refs = refs[:n_in], refs[n_in:]
        r_ins, refs = refs[:n_rin], refs[n_rin:]
        outs, refs = refs[:n_out], refs[n_out:]
        r_outs, refs = refs[:n_rout], refs[n_rout:]
        scr, r_sems = refs[:n_scr], refs[n_scr:]
        ids = [pl.program_id(d) for d in range(len(grid))]
        first = functools.reduce(jnp.logical_and, [i == 0 for i in ids])
        last = functools.reduce(jnp.logical_and, [i == g - 1 for i, g in zip(ids, grid)])

        @pl.when(first)
        def _():
            rider.start(r_ins, r_outs, r_sems)

        kern(*ins, *outs, *scr)

        @pl.when(last)
        def _():
            rider.finish(r_ins, r_outs, r_sems)

    res = pl.pallas_call(
        body, name=name, grid=grid, in_specs=list(in_specs) + [ANY] * n_rin,
        out_specs=out_specs_l + [ANY] * n_rout, out_shape=out_shape_l + list(rider.out_shape),
        scratch_shapes=list(scratch) + list(rider.sems),
        compiler_params=_cp(("arbitrary",) * len(grid)))(*args, *rider.arrays)
    mine, theirs = list(res[:n_out]), list(res[n_out:])
    return (mine[0] if single else mine), theirs


def _join_riders(riders):
    if len(riders) == 1:
        return riders[0]

    def parts(seq, attr):
        out, at = [], 0
        for r in riders:
            n = len(getattr(r, attr))
            out.append(seq[at:at + n])
            at += n
        return out

    def each(method):
        def run(ins, outs, sems):
            for r, i, o, s in zip(riders, parts(ins, "arrays"), parts(outs, "out_shape"), parts(sems, "sems")):
                getattr(r, method)(i, o, s)
        return run

    return _Rider(sum([r.arrays for r in riders], []), sum([r.out_shape for r in riders], []),
                  sum([r.sems for r in riders], []), each("start"), each("finish"))


def _ride_alone(rider, name):
    def body(*refs):
        n_rin, n_rout = len(rider.arrays), len(rider.out_shape)
        r_ins, r_outs, r_sems = refs[:n_rin], refs[n_rin:n_rin + n_rout], refs[n_rin + n_rout:]
        rider.start(r_ins, r_outs, r_sems)
        rider.finish(r_ins, r_outs, r_sems)

    return list(pl.pallas_call(
        body, name=name, in_specs=[ANY] * len(rider.arrays), out_specs=[ANY] * len(rider.out_shape),
        out_shape=list(rider.out_shape), scratch_shapes=list(rider.sems))(*rider.arrays))


def _ln(z):
    mu = jnp.mean(z, axis=-1, keepdims=True)
    zc = z - mu
    var = jnp.mean(zc * zc, axis=-1, keepdims=True)
    rstd = lax.rsqrt(var + EPS)
    return zc * rstd, rstd


def _ln_bwd(dxhat, xhat, rstd):
    m1 = jnp.mean(dxhat, axis=-1, keepdims=True)
    m2 = jnp.mean(dxhat * xhat, axis=-1, keepdims=True)
    return rstd * (dxhat - m1 - xhat * m2)


def _gelu(x):
    return 0.5 * x * (1.0 + lax.erf(x * INV_SQRT2))


def _gelu_and_grad(x):
    cdf = 0.5 * (1.0 + lax.erf(x * INV_SQRT2))
    return x * cdf, cdf + x * jnp.exp(-0.5 * x * x) * INV_SQRT_2PI


def _fold8(x):
    r, c = x.shape
    return jnp.sum(x.reshape(r // 8, 8, c), axis=0)


def _relu2(h):
    return jnp.square(jnp.maximum(h, 0.0))


def _weight_spec(block, index_map, resident):
    return pl.BlockSpec(block, index_map, pipeline_mode=pl.Buffered(1) if resident else None)


def _mm_nn(a, w, out_dtype, name, tm=512, tn=None, rider=None):
    n, k = a.shape
    m = w.shape[1]
    tm = min(tm, n)
    tn = m if tn is None else min(tn, m)

    def kern(a_ref, w_ref, o_ref):
        o_ref[...] = jnp.dot(a_ref[...].astype(BF16), w_ref[...], preferred_element_type=F32).astype(out_dtype)

    res, rode = _call(
        kern, name, (n // tm, m // tn),
        [pl.BlockSpec((tm, k), lambda i, j: (i, 0)), _weight_spec((k, tn), lambda i, j: (0, j), tn == m)],
        pl.BlockSpec((tm, tn), lambda i, j: (i, j)), _sds((n, m), out_dtype), (a, w), ("parallel", "parallel"),
        rider=rider)
    return res if rider is None else (res, rode)


def _mm_res_ln(a, w, res, g, b, name, relu2=False, tm=512, rider=None, then=None):
    n, k = a.shape
    tm = min(tm, n)

    m2 = None if then is None else then.shape[1]

    def kern(*refs):
        a_ref, w_ref, res_ref, g_ref, b_ref = refs[:5]
        x_ref, xhat_ref, rstd_ref = refs[-3:] if then is None else refs[-4:-1]
        av = a_ref[...]
        if relu2:
            av = _relu2(av.astype(F32))
        z = ALPHA * res_ref[...] + jnp.dot(av.astype(BF16), w_ref[...], preferred_element_type=F32)
        xhat, rstd = _ln(z)
        xhat_ref[...] = xhat
        rstd_ref[...] = rstd
        x = xhat * g_ref[...] + b_ref[...]
        x_ref[...] = x
        if then is not None:
            refs[-1][...] = jnp.dot(x.astype(BF16), refs[5][...], preferred_element_type=F32).astype(BF16)

    row = lambda i: (i, 0)
    fix = lambda i: (0, 0)
    in_specs = [pl.BlockSpec((tm, k), row), _weight_spec((k, D), fix, True), pl.BlockSpec((tm, D), row),
                pl.BlockSpec((1, D), fix), pl.BlockSpec((1, D), fix)]
    out_specs = [pl.BlockSpec((tm, D), row), pl.BlockSpec((tm, D), row), pl.BlockSpec((tm, 1), row)]
    out_shape = [_sds((n, D), F32), _sds((n, D), F32), _sds((n, 1), F32)]
    args = (a, w, res, g, b)
    if then is not None:
        in_specs.append(_weight_spec((D, m2), fix, True))
        out_specs.append(pl.BlockSpec((tm, m2), row))
        out_shape.append(_sds((n, m2), BF16))
        args += (then,)
    out, rode = _call(kern, name, (n // tm,), in_specs, out_specs, out_shape, args, ("parallel",), rider=rider)
    return out if rider is None else (out, rode)


def _mm_nt(a, w, out_dtype, name, relu2_grad_of=None, tm=512, tn=1024, rider=None):
    n, k = a.shape
    m = w.shape[0]
    tm = min(tm, n)
    tn = min(tn, m)
    with_h = relu2_grad_of is not None

    def kern(*refs):
        a_ref, w_ref = refs[0], refs[1]
        o_ref = refs[-1]
        r = lax.dot_general(a_ref[...].astype(BF16), w_ref[...], (((1,), (1,)), ((), ())), preferred_element_type=F32)
        if with_h:
            r = r * (2.0 * jnp.maximum(refs[2][...].astype(F32), 0.0))
        o_ref[...] = r.astype(out_dtype)

    in_specs = [pl.BlockSpec((tm, k), lambda i, j: (i, 0)), _weight_spec((tn, k), lambda i, j: (j, 0), tn == m)]
    args = [a, w]
    if with_h:
        in_specs.append(pl.BlockSpec((tm, tn), lambda i, j: (i, j)))
        args.append(relu2_grad_of)
    res, rode = _call(kern, name, (n // tm, m // tn), in_specs, pl.BlockSpec((tm, tn), lambda i, j: (i, j)),
                      _sds((n, m), out_dtype), args, ("parallel", "parallel"), rider=rider)
    return res if rider is None else (res, rode)


def _mm_tn_by_owner(a, b, name, tk=512):
    n, k1 = a.shape
    m = b.shape[1]
    s = m // N_DEV
    tk = min(tk, n)

    def kern(a_ref, b_ref, o_ref):
        @pl.when(pl.program_id(0) == 0)
        def _():
            o_ref[...] = jnp.zeros_like(o_ref)

        av = a_ref[...].astype(BF16)
        for j in range(N_DEV):
            o_ref[j // 2, j % 2] += lax.dot_general(av, b_ref[:, j * s:(j + 1) * s].astype(BF16),
                                                    (((0,), (0,)), ((), ())), preferred_element_type=F32)

    return pl.pallas_call(
        kern, name=name, grid=(n // tk,),
        in_specs=[pl.BlockSpec((tk, k1), lambda k: (k, 0)), pl.BlockSpec((tk, m), lambda k: (k, 0))],
        out_specs=pl.BlockSpec((4, 2, k1, s), lambda k: (0, 0, 0, 0), pipeline_mode=pl.Buffered(1)),
        out_shape=_sds((4, 2, k1, s), F32), compiler_params=_cp(("arbitrary",)))(a, b)


def _mm_tn(a, b, name, relu2=False, t1=1024, tn=2048, tk=512):
    n, k1 = a.shape
    m = b.shape[1]
    t1 = min(t1, k1)
    tn = m if m <= 2432 and m % tn else min(tn, m)
    tk = min(tk, n)

    def kern(a_ref, b_ref, o_ref):
        @pl.when(pl.program_id(2) == 0)
        def _():
            o_ref[...] = jnp.zeros_like(o_ref)

        av = a_ref[...]
        if relu2:
            av = _relu2(av.astype(F32))
        o_ref[...] += lax.dot_general(av.astype(BF16), b_ref[...].astype(BF16), (((0,), (0,)), ((), ())),
                                      preferred_element_type=F32)

    return pl.pallas_call(
        kern, name=name, grid=(k1 // t1, m // tn, n // tk),
        in_specs=[pl.BlockSpec((tk, t1), lambda i, j, k: (k, i)), pl.BlockSpec((tk, tn), lambda i, j, k: (k, j))],
        out_specs=pl.BlockSpec((t1, tn), lambda i, j, k: (i, j)),
        out_shape=_sds((k1, m), F32),
        compiler_params=_cp(("parallel", "parallel", "arbitrary")))(a, b)


def _bwd_in(dz_next, da, w, name, ln=None, tm=512, rider=None, then=None):
    n, k2 = da.shape
    tm = min(tm, n)
    row = lambda i: (i, 0)
    fix = lambda i: (0, 0)

    def dx_of(dzn_ref, da_ref, w_ref):
        return ALPHA * dzn_ref[...] + lax.dot_general(da_ref[...], w_ref[...], (((1,), (1,)), ((), ())),
                                                      preferred_element_type=F32)

    base_specs = [pl.BlockSpec((tm, D), row), pl.BlockSpec((tm, k2), row), _weight_spec((D, k2), fix, True)]
    if ln is None:
        def kern(dzn_ref, da_ref, w_ref, dx_ref):
            dx_ref[...] = dx_of(dzn_ref, da_ref, w_ref)

        out, rode = _call(kern, name, (n // tm,), base_specs, pl.BlockSpec((tm, D), row), _sds((n, D), F32),
                          (dz_next, da, w), ("parallel",), rider=rider)
        return out if rider is None else (out, rode)

    xhat, rstd, g = ln

    def kern(*refs):
        dzn_ref, da_ref, w_ref, xhat_ref, rstd_ref, g_ref = refs[:6]
        dz_ref, dg_ref, db_ref = refs[-3:] if then is None else refs[-4:-1]

        @pl.when(pl.program_id(0) == 0)
        def _():
            dg_ref[...] = jnp.zeros_like(dg_ref)
            db_ref[...] = jnp.zeros_like(db_ref)

        dx = dx_of(dzn_ref, da_ref, w_ref)
        xh = xhat_ref[...]
        dg_ref[...] += _fold8(dx * xh)
        db_ref[...] += _fold8(dx)
        dz = _ln_bwd(dx * g_ref[...], xh, rstd_ref[...])
        dz_ref[...] = dz
        if then is not None:
            refs[-1][...] = lax.dot_general(dz.astype(BF16), refs[6][...], (((1,), (1,)), ((), ())),
                                            preferred_element_type=F32).astype(BF16)

    in_specs = base_specs + [pl.BlockSpec((tm, D), row), pl.BlockSpec((tm, 1), row), pl.BlockSpec((1, D), fix)]
    out_specs = [pl.BlockSpec((tm, D), row), pl.BlockSpec((8, D), fix), pl.BlockSpec((8, D), fix)]
    out_shape = [_sds((n, D), F32), _sds((8, D), F32), _sds((8, D), F32)]
    args = (dz_next, da, w, xhat, rstd, g)
    if then is not None:
        m3 = then.shape[0]
        in_specs.append(_weight_spec((m3, D), fix, True))
        out_specs.append(pl.BlockSpec((tm, m3), row))
        out_shape.append(_sds((n, m3), BF16))
        args += (then,)
    out, rode = _call(kern, name, (n // tm,), in_specs, out_specs, out_shape, args, ("arbitrary",), rider=rider)
    return out if rider is None else (out, rode)


def _loss_lnbwd(x, target, xhat, rstd, g, name, tm=512):
    n = x.shape[0]
    tm = min(tm, n)
    row = lambda i: (i, 0)
    fix = lambda i: (0, 0)

    def kern(x_ref, t_ref, xhat_ref, rstd_ref, g_ref, sq_ref, dz_ref, dg_ref, db_ref):
        @pl.when(pl.program_id(0) == 0)
        def _():
            sq_ref[...] = jnp.zeros_like(sq_ref)
            dg_ref[...] = jnp.zeros_like(dg_ref)
            db_ref[...] = jnp.zeros_like(db_ref)

        err = x_ref[...] - t_ref[...]
        sq_ref[...] += _fold8(err * err)
        dx = err * (1.0 / D)
        xh = xhat_ref[...]
        dg_ref[...] += _fold8(dx * xh)
        db_ref[...] += _fold8(dx)
        dz_ref[...] = _ln_bwd(dx * g_ref[...], xh, rstd_ref[...])

    return pl.pallas_call(
        kern, name=name, grid=(n // tm,),
        in_specs=[pl.BlockSpec((tm, D), row), pl.BlockSpec((tm, D), row), pl.BlockSpec((tm, D), row),
                  pl.BlockSpec((tm, 1), row), pl.BlockSpec((1, D), fix)],
        out_specs=[pl.BlockSpec((8, D), fix), pl.BlockSpec((tm, D), row), pl.BlockSpec((8, D), fix),
                   pl.BlockSpec((8, D), fix)],
        out_shape=[_sds((8, D), F32), _sds((n, D), F32), _sds((8, D), F32), _sds((8, D), F32)],
        compiler_params=_cp(("arbitrary",)))(x, target, xhat, rstd, g)


def _softmax_rows(s):
    s = s - jnp.max(s, axis=-1, keepdims=True)
    e = jnp.exp(s)
    return e / jnp.sum(e, axis=-1, keepdims=True)


def _attn_fwd(q, kv, bd, name, tm=512):
    n = q.shape[0]
    s_len = n // bd
    m_len = kv.shape[0] // bd
    tm = min(tm, s_len)
    nt = s_len // tm
    scale = X_HD ** -0.5

    def kern(q_ref, k_ref, v_ref, o_ref):
        for h in range(X_HEADS):
            cs = slice(h * X_HD, (h + 1) * X_HD)
            s = lax.dot_general(q_ref[:, cs], k_ref[:, cs], (((1,), (1,)), ((), ())), preferred_element_type=F32)
            p = _softmax_rows(s * scale)
            o_ref[:, cs] = jnp.dot(p.astype(BF16), v_ref[:, cs], preferred_element_type=F32).astype(BF16)

    return pl.pallas_call(
        kern, name=name, grid=(bd, nt),
        in_specs=[pl.BlockSpec((tm, D), lambda b, i: (b * nt + i, 0)),
                  pl.BlockSpec((m_len, D), lambda b, i: (b, 0)), pl.BlockSpec((m_len, D), lambda b, i: (b, 1))],
        out_specs=pl.BlockSpec((tm, D), lambda b, i: (b * nt + i, 0)),
        out_shape=_sds((n, D), BF16),
        compiler_params=_cp(("parallel", "parallel")))(q, kv, kv)


def _attn_bwd(q, kv, do, bd, name, tm=512, rider=None):
    n = q.shape[0]
    s_len = n // bd
    m_len = kv.shape[0] // bd
    tm = min(tm, s_len)
    nt = s_len // tm
    scale = X_HD ** -0.5

    def kern(q_ref, k_ref, v_ref, do_ref, dq_ref, dkv_ref):
        @pl.when(pl.program_id(1) == 0)
        def _():
            dkv_ref[...] = jnp.zeros_like(dkv_ref)

        for h in range(X_HEADS):
            cs = slice(h * X_HD, (h + 1) * X_HD)
            vs = slice(D + h * X_HD, D + (h + 1) * X_HD)
            qh, kh, vh, doh = q_ref[:, cs], k_ref[:, cs], v_ref[:, cs], do_ref[:, cs]
            s = lax.dot_general(qh, kh, (((1,), (1,)), ((), ())), preferred_element_type=F32)
            p = _softmax_rows(s * scale)
            pb = p.astype(BF16)
            dp = lax.dot_general(doh, vh, (((1,), (1,)), ((), ())), preferred_element_type=F32)
            dkv_ref[:, vs] += lax.dot_general(pb, doh, (((0,), (0,)), ((), ())), preferred_element_type=F32)
            ds = (p * (dp - jnp.sum(dp * p, axis=-1, keepdims=True)) * scale).astype(BF16)
            dq_ref[:, cs] = jnp.dot(ds, kh, preferred_element_type=F32).astype(BF16)
            dkv_ref[:, cs] += lax.dot_general(ds, qh, (((0,), (0,)), ((), ())), preferred_element_type=F32)

    out, rode = _call(
        kern, name, (bd, nt),
        [pl.BlockSpec((tm, D), lambda b, i: (b * nt + i, 0)),
         pl.BlockSpec((m_len, D), lambda b, i: (b, 0)), pl.BlockSpec((m_len, D), lambda b, i: (b, 1)),
         pl.BlockSpec((tm, D), lambda b, i: (b * nt + i, 0))],
        [pl.BlockSpec((tm, D), lambda b, i: (b * nt + i, 0)), pl.BlockSpec((m_len, 2 * D), lambda b, i: (b, 0))],
        [_sds((n, D), BF16), _sds((bd * m_len, 2 * D), F32)], (q, kv, kv, do), ("parallel", "arbitrary"), rider=rider)
    return out if rider is None else (out, rode)


C_AV, C_AG, C_BU, C_BV, C_CB, C_CC, C_CX = 0, 384, 768, 1024, 1280, 1664, 2048


def _taps_by_phase(offsets):
    by_phase = {}
    for k, o in enumerate(offsets):
        by_phase.setdefault(o % 8, []).append((o // 8, k))
    return sorted(by_phase.items())


def _window_rows(win_ref, r0, a, rows, cs):
    return win_ref[pl.ds(pl.multiple_of(r0 + 8 * a, 8), rows), cs]


def _conv_taps(win_ref, r0, w_ref, offsets):
    parts = []
    for cb in range(3):
        cs = slice(cb * 128, (cb + 1) * 128)
        acc = jnp.zeros((CHUNK, 128), F32)
        for b, taps in _taps_by_phase(offsets):
            rows = CHUNK if b == 0 else CHUNK + 8
            part = jnp.zeros((rows, 128), F32)
            for a, k in taps:
                part = part + _window_rows(win_ref, r0, a, rows, cs) * w_ref[k:k + 1, cs]
            acc = acc + (part if b == 0 else part[b:b + CHUNK, :])
        parts.append(acc)
    return jnp.concatenate(parts, axis=1)


def _tap_grads(acc_ref, win_ref, r0, d, offsets, shifted_ref):
    for cb in range(3):
        cs = slice(cb * 128, (cb + 1) * 128)
        padded = jnp.concatenate([jnp.zeros((8, 128), F32), d[:, cs], jnp.zeros((8, 128), F32)], axis=0)
        for b, taps in _taps_by_phase(offsets):
            if b == 0:
                rows, db = CHUNK, d[:, cs]
            else:
                rows = CHUNK + 8
                shifted_ref[...] = padded[8 - b:8 - b + rows, :]
                db = shifted_ref[...]
            for a, k in taps:
                acc_ref[k * 8:(k + 1) * 8, cs] += _fold8(db * _window_rows(win_ref, r0, a, rows, cs))


def _causal_offsets(halo, n_taps):
    return [halo - (n_taps - 1) + k for k in range(n_taps)]


def _anticausal_offsets(n_taps):
    return [n_taps - 1 - k for k in range(n_taps)]


def _head_of_lane():
    return lax.broadcasted_iota(jnp.int32, (1, D_B), 1) // HEAD


def _spatial_mix(wm_ref, vb, head):
    mixed = jnp.zeros((CHUNK, D_B), F32)
    for h in range(B_HEADS):
        mh = jnp.dot(wm_ref[h], vb, preferred_element_type=F32)
        mixed = jnp.where(head == h, mh, mixed)
    return mixed


def _mixer_fwd(proj, p, bd, name, rider=None):
    n = proj.shape[0]
    s_len = n // bd
    n_chunks = s_len // CHUNK

    def kern(proj_ref, caw_ref, cab_ref, lag_ref, lab_ref, lvg_ref, lvb_ref, wm_ref, bsx_ref, ccw_ref, cat_ref,
             ca_ref, gs_ref, ccs_ref):
        gs_ref[0:HALO_A, :] = jnp.zeros((HALO_A, D_A), F32)
        ccs_ref[0:HALO_C, :] = jnp.zeros((HALO_C, D_C), F32)
        head = _head_of_lane()

        def chunk(i, carry):
            r0 = pl.multiple_of(i * CHUNK, CHUNK)
            rows = pl.ds(r0, CHUNK)
            ld = lambda c0, w: proj_ref[rows, c0:c0 + w].astype(F32)
            gs_ref[pl.ds(r0 + HALO_A, CHUNK), :] = ld(C_AV, D_A) * jax.nn.sigmoid(ld(C_AG, D_A))
            ca = _conv_taps(gs_ref, r0, caw_ref, _causal_offsets(HALO_A, KA)) + cab_ref[...]
            ca_ref[rows, :] = ca
            lna = _ln(ca)[0] * lag_ref[...] + lab_ref[...]
            cat_ref[rows, 0:D_A] = (lna * jax.nn.sigmoid(lna)).astype(BF16)
            u = _gelu(ld(C_BU, D_B))
            v = _ln(_gelu(ld(C_BV, D_B)))[0] * lvg_ref[...] + lvb_ref[...]
            mixed = _spatial_mix(wm_ref, v.astype(BF16), head) + bsx_ref[...]
            cat_ref[rows, D_A:D_A + D_B] = (u * mixed).astype(BF16)
            ccs_ref[pl.ds(r0 + HALO_C, CHUNK), :] = ld(C_CC, D_C) * ld(C_CX, D_C)
            conv = _conv_taps(ccs_ref, r0, ccw_ref, _causal_offsets(HALO_C, KC))
            cat_ref[rows, D_A + D_B:D] = (ld(C_CB, D_C) * conv).astype(BF16)
            return carry

        lax.fori_loop(0, n_chunks, chunk, 0)

    fix2 = lambda b: (0, 0)
    args = [proj, p["caw"], p["cab"], p["lag"], p["lab"], p["lvg"], p["lvb"], p["wm"], p["bsx"], p["ccw"]]
    in_specs = [pl.BlockSpec((s_len, IN_W), lambda b: (b, 0))]
    for a in args[1:]:
        in_specs.append(pl.BlockSpec(a.shape, (lambda b: (0, 0, 0)) if a.ndim == 3 else fix2))
    res, rode = _call(
        kern, name, (bd,), in_specs,
        [pl.BlockSpec((s_len, D), lambda b: (b, 0)), pl.BlockSpec((s_len, D_A), lambda b: (b, 0))],
        [_sds((n, D), BF16), _sds((n, D_A), F32)], args, ("parallel",),
        scratch=[pltpu.VMEM((s_len + HALO_A, D_A), F32), pltpu.VMEM((s_len + HALO_C, D_C), F32)], rider=rider)
    return res if rider is None else (res, rode)


def _mixer_bwd(proj, dcat, ca, p, bd, name, rider=None):
    n = proj.shape[0]
    s_len = n // bd
    n_chunks = s_len // CHUNK

    def kern(proj_ref, dcat_ref, ca_ref, caw_ref, cab_ref, lag_ref, lab_ref, lvg_ref, lvb_ref, wm_ref, bsx_ref, ccw_ref,
             dproj_ref, dcaw_ref, dcab_ref, dlag_ref, dlab_ref, dlvg_ref, dlvb_ref, dws_ref, dbs_ref, dccw_ref,
             gs_ref, dcas_ref, ccs_ref, dcs_ref, a_caw, a_cab, a_lag, a_lab, a_lvg, a_lvb, a_ccw, shifted_ref):
        gs_ref[0:HALO_A, :] = jnp.zeros((HALO_A, D_A), F32)
        ccs_ref[0:HALO_C, :] = jnp.zeros((HALO_C, D_C), F32)
        dcas_ref[s_len:s_len + HALO_A, :] = jnp.zeros((HALO_A, D_A), F32)
        dcs_ref[s_len:s_len + HALO_C, :] = jnp.zeros((HALO_C, D_C), F32)
        for acc in (a_caw, a_cab, a_lag, a_lab, a_lvg, a_lvb, a_ccw, dws_ref, dbs_ref):
            acc[...] = jnp.zeros_like(acc)
        head = _head_of_lane()
        lane128 = lax.broadcasted_iota(jnp.int32, (1, CHUNK), 1)

        def pass1(i, carry):
            r0 = pl.multiple_of(i * CHUNK, CHUNK)
            rows = pl.ds(r0, CHUNK)
            ld = lambda c0, w: proj_ref[rows, c0:c0 + w].astype(F32)
            dld = lambda c0, w: dcat_ref[rows, c0:c0 + w].astype(F32)
            gs_ref[pl.ds(r0 + HALO_A, CHUNK), :] = ld(C_AV, D_A) * jax.nn.sigmoid(ld(C_AG, D_A))
            xh, rstd = _ln(ca_ref[rows, :])
            lna = xh * lag_ref[...] + lab_ref[...]
            sg = jax.nn.sigmoid(lna)
            dlna = dld(0, D_A) * (sg * (1.0 + lna * (1.0 - sg)))
            a_lag[...] += _fold8(dlna * xh)
            a_lab[...] += _fold8(dlna)
            dca = _ln_bwd(dlna * lag_ref[...], xh, rstd)
            dcas_ref[rows, :] = dca
            a_cab[...] += _fold8(dca)
            _tap_grads(a_caw, gs_ref, r0, dca, _causal_offsets(HALO_A, KA), shifted_ref)
            pu, pv = ld(C_BU, D_B), ld(C_BV, D_B)
            u, du_dpu = _gelu_and_grad(pu)
            gv, dgv_dpv = _gelu_and_grad(pv)
            vxh, vrstd = _ln(gv)
            v = vxh * lvg_ref[...] + lvb_ref[...]
            vb = v.astype(BF16)
            mixed = _spatial_mix(wm_ref, vb, head) + bsx_ref[...]
            dbo = dld(D_A, D_B)
            dproj_ref[rows, C_BU:C_BU + D_B] = (dbo * mixed * du_dpu).astype(BF16)
            dmixed = dbo * u
            dv = jnp.zeros((CHUNK, D_B), F32)
            bsum = jnp.zeros((CHUNK, CHUNK), F32)
            for h in range(B_HEADS):
                dmh = jnp.where(head == h, dmixed, 0.0)
                dmb = dmh.astype(BF16)
                dvh = lax.dot_general(wm_ref[h], dmb, (((0,), (0,)), ((), ())), preferred_element_type=F32)
                dv = jnp.where(head == h, dvh, dv)
                dws_ref[h] += lax.dot_general(dmb, vb, (((1,), (1,)), ((), ())), preferred_element_type=F32)
                bsum = bsum + jnp.where(lane128 == h, jnp.sum(dmh, axis=-1, keepdims=True), 0.0)
            dbs_ref[...] += bsum
            a_lvg[...] += _fold8(dv * vxh)
            a_lvb[...] += _fold8(dv)
            dgv = _ln_bwd(dv * lvg_ref[...], vxh, vrstd)
            dproj_ref[rows, C_BV:C_BV + D_B] = (dgv * dgv_dpv).astype(BF16)
            ccs_ref[pl.ds(r0 + HALO_C, CHUNK), :] = ld(C_CC, D_C) * ld(C_CX, D_C)
            conv = _conv_taps(ccs_ref, r0, ccw_ref, _causal_offsets(HALO_C, KC))
            dco = dld(D_A + D_B, D_C)
            dproj_ref[rows, C_CB:C_CB + D_C] = (dco * conv).astype(BF16)
            dconv = dco * ld(C_CB, D_C)
            dcs_ref[rows, :] = dconv
            _tap_grads(a_ccw, ccs_ref, r0, dconv, _causal_offsets(HALO_C, KC), shifted_ref)
            return carry

        lax.fori_loop(0, n_chunks, pass1, 0)

        def pass2(i, carry):
            r0 = pl.multiple_of(i * CHUNK, CHUNK)
            rows = pl.ds(r0, CHUNK)
            ld = lambda c0, w: proj_ref[rows, c0:c0 + w].astype(F32)
            dg = _conv_taps(dcas_ref, r0, caw_ref, _anticausal_offsets(KA))
            pa = ld(C_AV, D_A)
            sg = jax.nn.sigmoid(ld(C_AG, D_A))
            dproj_ref[rows, C_AV:C_AV + D_A] = (dg * sg).astype(BF16)
            dproj_ref[rows, C_AG:C_AG + D_A] = (dg * pa * sg * (1.0 - sg)).astype(BF16)
            dcc = _conv_taps(dcs_ref, r0, ccw_ref, _anticausal_offsets(KC))
            dproj_ref[rows, C_CC:C_CC + D_C] = (dcc * ld(C_CX, D_C)).astype(BF16)
            dproj_ref[rows, C_CX:C_CX + D_C] = (dcc * ld(C_CC, D_C)).astype(BF16)
            return carry

        lax.fori_loop(0, n_chunks, pass2, 0)

        for k in range(KA):
            dcaw_ref[k:k + 1, :] = jnp.sum(a_caw[k * 8:(k + 1) * 8, :], axis=0, keepdims=True)
        dcaw_ref[KA:KA + 1, :] = jnp.zeros((1, D_A), F32)
        for k in range(8):
            if k < KC:
                dccw_ref[k:k + 1, :] = jnp.sum(a_ccw[k * 8:(k + 1) * 8, :], axis=0, keepdims=True)
            else:
                dccw_ref[k:k + 1, :] = jnp.zeros((1, D_C), F32)
        dcab_ref[...] = a_cab[...]
        dlag_ref[...] = a_lag[...]
        dlab_ref[...] = a_lab[...]
        dlvg_ref[...] = a_lvg[...]
        dlvb_ref[...] = a_lvb[...]

    fix2 = lambda b: (0, 0)
    args = [proj, dcat, ca, p["caw"], p["cab"], p["lag"], p["lab"], p["lvg"], p["lvb"], p["wm"], p["bsx"], p["ccw"]]
    once = pl.Buffered(1)
    in_specs = [pl.BlockSpec((s_len, IN_W), lambda b: (b, 0), pipeline_mode=once),
                pl.BlockSpec((s_len, D), lambda b: (b, 0), pipeline_mode=once),
                pl.BlockSpec((s_len, D_A), lambda b: (b, 0), pipeline_mode=once)]
    for a in args[3:]:
        in_specs.append(pl.BlockSpec(a.shape, (lambda b: (0, 0, 0)) if a.ndim == 3 else fix2))

    def per_seq(*shape):
        nd = len(shape)
        return (pl.BlockSpec((None,) + shape, lambda b: (b,) + (0,) * nd), _sds((bd,) + shape, F32))

    outs = [(pl.BlockSpec((s_len, IN_W), lambda b: (b, 0), pipeline_mode=once), _sds((n, IN_W), BF16)),
            per_seq(32, D_A), per_seq(8, D_A), per_seq(8, D_A), per_seq(8, D_A), per_seq(8, D_B), per_seq(8, D_B),
            per_seq(B_HEADS, CHUNK, CHUNK), per_seq(CHUNK, CHUNK), per_seq(8, D_C)]
    res, rode = _call(
        kern, name, (bd,), in_specs, [o[0] for o in outs], [o[1] for o in outs], args, ("parallel",),
        scratch=[pltpu.VMEM((s_len + HALO_A, D_A), F32), pltpu.VMEM((s_len + HALO_A, D_A), F32),
                 pltpu.VMEM((s_len + HALO_C, D_C), F32), pltpu.VMEM((s_len + HALO_C, D_C), F32),
                 pltpu.VMEM((KA * 8, D_A), F32), pltpu.VMEM((8, D_A), F32), pltpu.VMEM((8, D_A), F32),
                 pltpu.VMEM((8, D_A), F32), pltpu.VMEM((8, D_B), F32), pltpu.VMEM((8, D_B), F32),
                 pltpu.VMEM((KC * 8, D_C), F32), pltpu.VMEM((CHUNK + 8, 128), F32)],
        rider=rider)
    return res if rider is None else (res, rode)


def _adamw_update(w, m, v, g):
    mn = B1 * m + (1.0 - B1) * g
    vn = B2 * v + (1.0 - B2) * jnp.square(g)
    m_hat = mn / (1.0 - B1 ** STEP)
    v_hat = vn / (1.0 - B2 ** STEP)
    return -LR * (m_hat / (jnp.sqrt(v_hat) + ADAM_EPS) + WD * w), mn, vn


def _adamw_layers(w, m, v, own, remote, name):
    shape = w.shape
    depth, c = shape[0], shape[-1]
    r = math.prod(shape[1:-1])
    tr = 256 if r % 256 == 0 else r
    nt = r // tr

    def kern(*refs):
        w_ref, m_ref, v_ref = refs[:3]
        own_refs, rem_refs = refs[3:3 + depth], refs[3 + depth:3 + 4 * depth]
        go_ref, d_ref, mo_ref, vo_ref = refs[3 + 4 * depth:]
        for lp in range(depth):
            @pl.when(pl.program_id(0) == lp)
            def _(lp=lp):
                g = own_refs[lp][...]
                for j in range(3):
                    g = g + rem_refs[3 * lp + j][...].astype(F32)
                go_ref[...] = g
                d_ref[...], mo_ref[...], vo_ref[...] = _adamw_update(w_ref[...], m_ref[...], v_ref[...], g)

    def rows_of(lp):
        return lambda l, i: jnp.where(l == lp, i, jnp.where(l < lp, 0, nt - 1))

    spec = pl.BlockSpec((None, tr, c), lambda l, i: (l, i, 0))
    own_specs = [pl.BlockSpec((tr, c), functools.partial(lambda f, l, i: (f(l, i), 0), rows_of(lp)))
                 for lp in range(depth)]
    rem_specs = [pl.BlockSpec((None, tr, c), functools.partial(lambda f, j, l, i: (j, f(l, i), 0), rows_of(lp), j))
                 for lp in range(depth) for j in range(3)]
    as3d = lambda a: a.reshape(depth, r, c)
    outs = pl.pallas_call(
        kern, name=name, grid=(depth, nt), in_specs=[spec] * 3 + own_specs + rem_specs, out_specs=[spec] * 4,
        out_shape=[_sds((depth, r, c), F32)] * 4, compiler_params=_cp(("arbitrary", "arbitrary")))(
            as3d(w), as3d(m), as3d(v), *[o.reshape(r, c) for o in own],
            *[x.reshape(3, r, c) for x in remote for _ in range(3)])
    return [o.reshape(shape) for o in outs]


def _adamw(w, m, v, g_parts, name):
    shape = w.shape
    c = shape[-1]
    r = math.prod(shape[:-1])
    as2d = lambda a: a.reshape(r, c)
    tr = 512 if r % 512 == 0 else r
    slots = [(a.reshape(a.shape[0], r, c), p) for a in g_parts for p in range(a.shape[0])]
    n_g = len(slots)

    def kern(*refs):
        w_ref, m_ref, v_ref = refs[:3]
        g_refs = refs[3:3 + n_g]
        go_ref, d_ref, mo_ref, vo_ref = refs[3 + n_g:]
        g = g_refs[0][...].astype(F32)
        for gr in g_refs[1:]:
            g = g + gr[...].astype(F32)
        go_ref[...] = g
        d_ref[...], mo_ref[...], vo_ref[...] = _adamw_update(w_ref[...], m_ref[...], v_ref[...], g)

    spec = pl.BlockSpec((tr, c), lambda i: (i, 0))
    g_specs = [pl.BlockSpec((None, tr, c), functools.partial(lambda p, i: (p, i, 0), p)) for _, p in slots]
    outs = pl.pallas_call(
        kern, name=name, grid=(r // tr,), in_specs=[spec] * 3 + g_specs, out_specs=[spec] * 4,
        out_shape=[_sds((r, c), F32)] * 4, compiler_params=_cp(("parallel",)))(
            as2d(w), as2d(m), as2d(v), *[a for a, _ in slots])
    return [o.reshape(shape) for o in outs]


def _place():
    x, y, c = lax.axis_index("x"), lax.axis_index("y"), lax.axis_index("c")
    other_chips = [(1 - x, y), (x, 1 - y), (1 - x, 1 - y)]
    return x, y, c, other_chips


def _gather_rider(arrays):
    n_arr = len(arrays)

    def parts(ins, outs, sems):
        send_sems, recv_sems, local_sems = sems
        x, y, c, chips = _place()
        me, sibling = (x, y, c), (x, y, 1 - c)
        slot = lambda px, py, pc: 4 * px + 2 * py + pc

        def copy(a, k, block, to, from_input=False):
            dst = outs[a].at[slot(*block)]
            return pltpu.make_async_remote_copy(
                src_ref=ins[a] if from_input else dst, dst_ref=dst, send_sem=send_sems.at[k, a],
                recv_sem=recv_sems.at[k, a], device_id=to, device_id_type=MESH)

        mine = [pltpu.make_async_copy(ins[a], outs[a].at[slot(*me)], local_sems.at[a]) for a in range(n_arr)]
        first = []
        for a in range(n_arr):
            first.append(copy(a, 0, me, sibling, True))
            first += [copy(a, 1 + j, me, (*chip, c), True) for j, chip in enumerate(chips)]
        return copy, mine, first, me, sibling, chips, c

    def start(ins, outs, sems):
        _, mine, first, *_ = parts(ins, outs, sems)
        for cp in mine + first:
            cp.start()

    def finish(ins, outs, sems):
        copy, mine, first, me, sibling, chips, c = parts(ins, outs, sems)
        passed = []
        for j, chip in enumerate(chips):
            for a in range(n_arr):
                copy(a, 1 + j, (*chip, c), me).wait_recv()
                passed.append(copy(a, 4 + j, (*chip, c), sibling))
                passed[-1].start()
        for a in range(n_arr):
            copy(a, 0, sibling, me).wait_recv()
            for j, chip in enumerate(chips):
                copy(a, 4 + j, (*chip, 1 - c), me).wait_recv()
        for cp in first + passed:
            cp.wait_send()
        for cp in mine:
            cp.wait()

    return _Rider(list(arrays), [_sds((N_DEV,) + a.shape, a.dtype) for a in arrays],
                  [pltpu.SemaphoreType.DMA((7, n_arr)), pltpu.SemaphoreType.DMA((7, n_arr)),
                   pltpu.SemaphoreType.DMA((n_arr,))], start, finish)


def _exchange_rider(arrays):
    n_arr = len(arrays)

    def copies(ins, outs, sems):
        send_sems, recv_sems = sems
        x, y, c, _ = _place()
        return [pltpu.make_async_remote_copy(
            src_ref=ins[a].at[:, 1 - c], dst_ref=outs[a], send_sem=send_sems.at[a], recv_sem=recv_sems.at[a],
            device_id=(x, y, 1 - c), device_id_type=MESH) for a in range(n_arr)]

    def start(ins, outs, sems):
        for cp in copies(ins, outs, sems):
            cp.start()

    def finish(ins, outs, sems):
        for cp in copies(ins, outs, sems):
            cp.wait()

    return _Rider(list(arrays), [_sds(a.shape[:1] + a.shape[2:], a.dtype) for a in arrays],
                  [pltpu.SemaphoreType.DMA((n_arr,)), pltpu.SemaphoreType.DMA((n_arr,))], start, finish)


def _pair_add(mine, theirs, core, name):
    _, _, r, c = mine.shape
    tr = 512 if r % 512 == 0 else r

    def kern(core_ref, a_ref, b_ref, o_ref, ob_ref):
        s = a_ref[...] + b_ref[...]
        o_ref[...] = s
        ob_ref[...] = s.astype(BF16)

    row = lambda t, i, core_ref: (t, i, 0)
    return pl.pallas_call(
        kern, name=name,
        grid_spec=pltpu.PrefetchScalarGridSpec(
            num_scalar_prefetch=1, grid=(4, r // tr),
            in_specs=[pl.BlockSpec((None, None, tr, c), lambda t, i, core_ref: (t, core_ref[0], i, 0)),
                      pl.BlockSpec((None, tr, c), row)],
            out_specs=[pl.BlockSpec((None, tr, c), row), pl.BlockSpec((None, tr, c), row)]),
        out_shape=[_sds((4, r, c), F32), _sds((4, r, c), BF16)],
        compiler_params=_cp(("parallel", "parallel")))(core, mine, theirs)


def _scatter_rider(sums, sums_bf16):
    n_arr = len(sums)

    def copies(ins, outs, sems):
        send_sems, recv_sems, local_sems = sems
        x, y, c, chips = _place()
        own = [pltpu.make_async_copy(ins[a].at[2 * x + y], outs[a], local_sems.at[a]) for a in range(n_arr)]
        remote = []
        for a in range(n_arr):
            for j, (px, py) in enumerate(chips):
                remote.append(pltpu.make_async_remote_copy(
                    src_ref=ins[n_arr + a].at[2 * px + py], dst_ref=outs[n_arr + a].at[j], send_sem=send_sems.at[j, a],
                    recv_sem=recv_sems.at[j, a], device_id=(px, py, c), device_id_type=MESH))
        return own + remote

    def start(ins, outs, sems):
        for cp in copies(ins, outs, sems):
            cp.start()

    def finish(ins, outs, sems):
        for cp in copies(ins, outs, sems):
            cp.wait()

    return _Rider(list(sums) + list(sums_bf16),
                  [_sds(a.shape[1:], a.dtype) for a in sums] + [_sds((3,) + a.shape[1:], a.dtype) for a in sums_bf16],
                  [pltpu.SemaphoreType.DMA((3, n_arr)), pltpu.SemaphoreType.DMA((3, n_arr)),
                   pltpu.SemaphoreType.DMA((n_arr,))], start, finish)


def _mixer_params(conv_a_w, conv_a_b, ln_a_g, ln_a_b, ln_v_g, ln_v_b, w_s, b_s, conv_c_w):
    causal = jnp.tril(jnp.ones((CHUNK, CHUNK), dtype=bool))
    row = lambda a: a.reshape(1, -1)
    return dict(
        caw=jnp.pad(conv_a_w, ((0, 32 - KA), (0, 0))), cab=row(conv_a_b), lag=row(ln_a_g), lab=row(ln_a_b),
        lvg=row(ln_v_g), lvb=row(ln_v_b), wm=jnp.where(causal[None], w_s, 0.0).astype(BF16),
        bsx=jnp.repeat(b_s.T, HEAD, axis=1), ccw=jnp.pad(conv_c_w, ((0, 8 - KC), (0, 0))))


class _Schedule:
    def __init__(self, big=None):
        self.big = big

    def weights(self, l):
        return {k: v[l] for k, v in self.big.items()}

    def rider(self, stage, l):
        return None

    def rode(self, stage, l, results):
        pass

    def note_grads(self, l, grads):
        pass

    def finish(self):
        pass


def _local_step(x, mem, target, sched, small, bd):
    row = lambda a: a.reshape(1, -1)

    def ride(stage, l, fn, *args, **kw):
        rider = sched.rider(stage, l)
        res = fn(*args, rider=rider, **kw)
        if rider is not None:
            res, results = res
            sched.rode(stage, l, results)
        return res

    saved = []
    for l in range(DEPTH):
        big = sched.weights(l)
        mp = _mixer_params(big["conv_a_w"], *[small[k][l] for k in ("conv_a_b", "ln_a_g", "ln_a_b", "ln_v_g", "ln_v_b",
                                                                    "w_s", "b_s")], big["conv_c_w"])
        proj = ride("in_proj", l, _mm_nn, x, big["w_in"], BF16, f"in_proj_{l}")
        cat, ca = ride("mixer_fwd", l, _mixer_fwd, proj, mp, bd, f"mixer_fwd_{l}")
        x1, xh1, rs1, q = ride("out_proj_ln1", l, _mm_res_ln, cat, big["w_out"], x, row(small["ln1_g"][l]),
                               row(small["ln1_b"][l]), f"out_proj_ln1_q_{l}", then=big["w_q"])
        kv = _mm_nn(mem, big["w_kv"], BF16, f"kv_proj_{l}")
        o = _attn_fwd(q, kv, bd, f"attn_fwd_{l}")
        x2, xh2, rs2, h = ride("ff1", l, _mm_res_ln, o, big["w_o"], x1, row(small["ln2_g"][l]), row(small["ln2_b"][l]),
                               f"o_proj_ln2_ff1_{l}", then=big["w_ff1"])
        x3, xh3, rs3 = ride("ff2_ln3", l, _mm_res_ln, h, big["w_ff2"], x2, row(small["ln3_g"][l]),
                            row(small["ln3_b"][l]), f"ff2_ln3_{l}", relu2=True)
        saved.append(dict(mp=mp, x=x, proj=proj, cat=cat, ca=ca, x1=x1, xh1=xh1, rs1=rs1, q=q, kv=kv, o=o, x2=x2, xh2=xh2,
                          rs2=rs2, h=h, xh3=xh3, rs3=rs3))
        x = x3

    grads = {k: [None] * DEPTH for k in WEIGHTS}
    s = saved[-1]
    sq, dz3, dg, db = _loss_lnbwd(x, target, s["xh3"], s["rs3"], row(small["ln3_g"][DEPTH - 1]), "loss_ln3_bwd")
    grad_x = None
    causal = jnp.tril(jnp.ones((CHUNK, CHUNK), dtype=bool))
    for l in reversed(range(DEPTH)):
        s = saved[l]
        big = sched.weights(l)
        grads["ln3_g"][l], grads["ln3_b"][l] = jnp.sum(dg, axis=0), jnp.sum(db, axis=0)
        dh = ride("ff2_bwd", l, _mm_nt, dz3, big["w_ff2"], BF16, f"ff2_bwd_{l}", relu2_grad_of=s["h"], tn=D_FF)
        grads["w_ff2"][l] = _mm_tn(s["h"], dz3, f"ff2_wgrad_{l}", relu2=True, t1=2048)
        grads["w_ff1"][l] = _mm_tn_by_owner(s["x2"], dh, f"ff1_wgrad_{l}")
        sched.note_grads(l, {k: grads[k][l] for k in ("w_ff1", "w_ff2")})
        dz2, dg, db, do = ride("ff1_bwd_ln2", l, _bwd_in, dz3, dh, big["w_ff1"], f"ff1_bwd_ln2_o_bwd_{l}",
                               ln=(s["xh2"], s["rs2"], row(small["ln2_g"][l])), then=big["w_o"])
        grads["ln2_g"][l], grads["ln2_b"][l] = jnp.sum(dg, axis=0), jnp.sum(db, axis=0)
        grads["w_o"][l] = _mm_tn(s["o"], dz2, f"o_proj_wgrad_{l}")
        dq, dkv = ride("attn_bwd", l, _attn_bwd, s["q"], s["kv"], do, bd, f"attn_bwd_{l}")
        grads["w_q"][l] = _mm_tn(s["x1"], dq, f"q_wgrad_{l}")
        grads["w_kv"][l] = _mm_tn_by_owner(mem, dkv, f"kv_wgrad_{l}")
        sched.note_grads(l, {k: grads[k][l] for k in ("w_o", "w_q", "w_kv")})
        dz1, dg, db, dcat = ride("q_bwd_ln1", l, _bwd_in, dz2, dq, big["w_q"], f"q_bwd_ln1_out_bwd_{l}",
                                 ln=(s["xh1"], s["rs1"], row(small["ln1_g"][l])), then=big["w_out"])
        grads["ln1_g"][l], grads["ln1_b"][l] = jnp.sum(dg, axis=0), jnp.sum(db, axis=0)
        grads["w_out"][l] = _mm_tn(s["cat"], dz1, f"out_proj_wgrad_{l}")
        sched.note_grads(l, {"w_out": grads["w_out"][l]})
        (dproj, dcaw, dcab, dlag, dlab, dlvg, dlvb, dws, dbs, dccw) = ride(
            "mixer_bwd", l, _mixer_bwd, s["proj"], dcat, s["ca"], s["mp"], bd, f"mixer_bwd_{l}")
        grads["conv_a_w"][l] = jnp.sum(dcaw, axis=0)[:KA]
        grads["conv_a_b"][l] = jnp.sum(dcab, axis=(0, 1))
        grads["ln_a_g"][l] = jnp.sum(dlag, axis=(0, 1))
        grads["ln_a_b"][l] = jnp.sum(dlab, axis=(0, 1))
        grads["ln_v_g"][l] = jnp.sum(dlvg, axis=(0, 1))
        grads["ln_v_b"][l] = jnp.sum(dlvb, axis=(0, 1))
        grads["w_s"][l] = jnp.where(causal[None], jnp.sum(dws, axis=0), 0.0)
        grads["b_s"][l] = jnp.sum(dbs, axis=0)[:, :B_HEADS].T
        grads["conv_c_w"][l] = jnp.sum(dccw, axis=0)[:KC]
        grads["w_in"][l] = _mm_tn(s["x"], dproj, f"in_proj_wgrad_{l}")
        sched.note_grads(l, {k: v[l] for k, v in grads.items() if k not in ("w_ff1", "w_ff2")})
        if l > 0:
            p = saved[l - 1]
            dz3, dg, db = _bwd_in(dz1, dproj, big["w_in"], f"in_proj_bwd_ln3_{l}",
                                  ln=(p["xh3"], p["rs3"], row(small["ln3_g"][l - 1])))
        else:
            grad_x = ride("in_proj_bwd", 0, _bwd_in, dz1, dproj, big["w_in"], "in_proj_bwd_0")
    sched.finish()
    return sq, grad_x, grads


WEIGHTS = ("w_in", "conv_a_w", "conv_a_b", "ln_a_g", "ln_a_b", "ln_v_g", "ln_v_b", "w_s", "b_s", "conv_c_w", "w_out",
           "ln1_g", "ln1_b", "w_q", "w_kv", "w_o", "ln2_g", "ln2_b", "w_ff1", "w_ff2", "ln3_g", "ln3_b")
COL_SHARDED = ("w_in", "w_kv", "w_ff1")
ROW_SHARDED = ("w_out", "w_q", "w_o", "w_ff2")
BIG = COL_SHARDED + ROW_SHARDED
REPLICATED = tuple(k for k in WEIGHTS if k not in BIG and k not in ("conv_a_w", "conv_c_w"))
PACK_LANES = 128


CONV_ROWS = 32 + 8
GATHER_LAYER0 = {"first": ("w_in", "conv"), "in_proj": ("w_out", "w_q", "w_kv", "w_o"), "mixer_fwd": ("w_ff1", "w_ff2")}
GATHER_NEXT0 = {"out_proj_ln1": ("w_in", "conv", "w_out"), "ff1": ("w_ff1", "w_kv"), "ff2_ln3": ("w_ff2", "w_q", "w_o")}
GATHER_NEXT = {"in_proj": ("w_in", "conv"), "mixer_fwd": ("w_out", "w_q", "w_o"), "ff1": ("w_ff1", "w_kv"),
               "ff2_ln3": ("w_ff2",)}
GRADS_EARLY = ("w_ff1", "w_ff2")
GRADS_LATE = ("w_in", "w_kv", "w_out", "w_q", "w_o", "conv")
LATE_ON_FF1_BWD, LATE_ON_ATTN_BWD = ("w_kv", "w_out", "w_q", "w_o"), ("w_in", "conv")
GRADS_MID0, GRADS_OUT0, GRADS_LAST0 = ("w_kv", "w_q", "w_o"), ("w_out",), ("w_in", "conv")


def _gathered_to_full(g, col_sharded):
    _, a, b = g.shape
    if col_sharded:
        return g.transpose(1, 0, 2).reshape(a, N_DEV * b)
    return g.reshape(N_DEV * a, b)


def _full_to_owner_major(g, col_sharded):
    if g.ndim == 4:
        return g
    a, b = g.shape
    if col_sharded:
        return g.reshape(a, 4, 2, b // N_DEV).transpose(1, 2, 0, 3)
    return g.reshape(4, 2, a // N_DEV, b)


def _conv_pack(conv_a, conv_c):
    pad = lambda a, rows: jnp.pad(a, [(0, 0)] * (a.ndim - 2) + [(0, rows - a.shape[-2]), (0, 0)])
    return jnp.concatenate([pad(conv_a, 32), pad(conv_c, 8)], axis=-2)


def _conv_unpack(packed):
    return packed[..., :KA, :], packed[..., 32:32 + KC, :]


class _Overlapped(_Schedule):
    def __init__(self, shards_bf16, conv_shards, core):
        self.shards, self.conv_shards, self.core = shards_bf16, conv_shards, core
        self.full = {l: {} for l in range(DEPTH)}
        self.grads = {l: {} for l in range(DEPTH)}
        self.owner_major = {}
        self.from_sibling = {}
        self.scattering = None
        self.own, self.remote = {}, {}
        self.replicated = None

    def _gather(self, l, names):
        return _gather_rider([self.conv_shards[l] if k == "conv" else self.shards[k][l] for k in names])

    def _store(self, l, names, gathered):
        for k, g in zip(names, gathered):
            if k == "conv":
                self.full[l]["conv_a_w"], self.full[l]["conv_c_w"] = _conv_unpack(_gathered_to_full(g, True))
            else:
                self.full[l][k] = _gathered_to_full(g, k in COL_SHARDED)

    def weights(self, l):
        if l == 0 and not self.full[0]:
            names = GATHER_LAYER0["first"]
            self._store(0, names, _ride_alone(self._gather(0, names), "weights_all_gather_first"))
        return self.full[l]

    def note_grads(self, l, grads):
        self.grads[l].update(grads)

    def _owner_major(self, l, k):
        if (l, k) not in self.owner_major:
            g = self.grads[l]
            if k == "conv":
                self.owner_major[(l, k)] = _full_to_owner_major(_conv_pack(g["conv_a_w"], g["conv_c_w"]), True)
            else:
                self.owner_major[(l, k)] = _full_to_owner_major(g[k], k in COL_SHARDED)
        return self.owner_major[(l, k)]

    def _exchange(self, l, names):
        return _exchange_rider([self._owner_major(l, k) for k in names])

    def _scatter(self, groups):
        sums, sums_bf16, self.scattering = [], [], []
        for l, names in groups:
            for k in names:
                s, sb = _pair_add(self._owner_major(l, k), self.from_sibling.pop((l, k)), self.core,
                                  f"grad_pair_add_{l}_{k}")
                sums.append(s)
                sums_bf16.append(sb)
                self.scattering.append((l, k))
        return _scatter_rider(sums, sums_bf16)

    def _scattered(self, results):
        n = len(self.scattering)
        for i, key in enumerate(self.scattering):
            self.own[key], self.remote[key] = results[i], results[n + i]

    def _received(self, l, names, results):
        for k, r in zip(names, results):
            self.from_sibling[(l, k)] = r

    def rider(self, stage, l):
        if l == 0 and stage in ("in_proj", "mixer_fwd"):
            return self._gather(0, GATHER_LAYER0[stage])
        gather_next = GATHER_NEXT0 if l == 0 else GATHER_NEXT
        if stage in gather_next and l + 1 < DEPTH:
            return self._gather(l + 1, gather_next[stage])
        if stage == "ff2_bwd" and l + 1 < DEPTH:
            return self._exchange(l + 1, GRADS_LATE)
        if stage == "ff1_bwd_ln2" and l + 1 < DEPTH:
            return self._scatter([(l + 1, LATE_ON_FF1_BWD)])
        if stage == "attn_bwd" and l + 1 < DEPTH:
            return _join_riders([self._exchange(l, GRADS_EARLY), self._scatter([(l + 1, LATE_ON_ATTN_BWD)])])
        if stage == "attn_bwd":
            return self._exchange(l, GRADS_EARLY)
        if stage == "q_bwd_ln1" and l == 0:
            return self._exchange(0, GRADS_MID0)
        if stage == "mixer_bwd" and l == 0:
            return _join_riders([self._scatter([(0, GRADS_EARLY), (0, GRADS_MID0)]), self._exchange(0, GRADS_OUT0)])
        if stage == "mixer_bwd":
            return self._scatter([(l, GRADS_EARLY)])
        if stage == "in_proj_bwd":
            packed = _pack_rows([jnp.stack([self.grads[i][k] for i in range(DEPTH)]) for k in REPLICATED])
            return _gather_rider([packed])
        return None

    def rode(self, stage, l, results):
        gather_next = GATHER_NEXT0 if l == 0 else GATHER_NEXT
        if l == 0 and stage in ("in_proj", "mixer_fwd"):
            self._store(0, GATHER_LAYER0[stage], results)
        elif stage in gather_next:
            self._store(l + 1, gather_next[stage], results)
        elif stage == "ff2_bwd":
            self._received(l + 1, GRADS_LATE, results)
        elif stage == "ff1_bwd_ln2":
            self._scattered(results)
        elif stage == "attn_bwd":
            n = len(GRADS_EARLY)
            self._received(l, GRADS_EARLY, results[:n])
            if l + 1 < DEPTH:
                self._scattered(results[n:])
        elif stage == "q_bwd_ln1":
            self._received(0, GRADS_MID0, results)
        elif stage == "mixer_bwd" and l == 0:
            n = 2 * len(self.scattering)
            self._scattered(results[:n])
            self._received(0, GRADS_OUT0, results[n:])
        elif stage == "mixer_bwd":
            self._scattered(results)
        elif stage == "in_proj_bwd":
            self.replicated = results[0]

    def finish(self):
        self._received(0, GRADS_LAST0, _ride_alone(self._exchange(0, GRADS_LAST0), "grad_pair_exchange_last"))
        self._scattered(_ride_alone(self._scatter([(0, GRADS_OUT0), (0, GRADS_LAST0)]), "grad_chip_scatter_last"))


def _pack_rows(parts):
    flat = jnp.concatenate([p.reshape(-1, PACK_LANES) for p in parts], axis=0)
    return jnp.pad(flat, ((0, -flat.shape[0] % 8), (0, 0)))


def _unpack_rows(packed, like):
    out, r = [], 0
    for p in like:
        n = p.size // PACK_LANES
        out.append(packed[r:r + n].reshape(p.shape))
        r += n
    return out


def kernel(x, mem, w_in, conv_a_w, conv_a_b, ln_a_g, ln_a_b, ln_v_g, ln_v_b, w_s, b_s, conv_c_w, w_out, ln1_g, ln1_b, w_q, w_kv, w_o, ln2_g, ln2_b, w_ff1, w_ff2, ln3_g, ln3_b, loss_target, m_w_in, m_conv_a_w, m_conv_a_b, m_ln_a_g, m_ln_a_b, m_ln_v_g, m_ln_v_b, m_w_s, m_b_s, m_conv_c_w, m_w_out, m_ln1_g, m_ln1_b, m_w_q, m_w_kv, m_w_o, m_ln2_g, m_ln2_b, m_w_ff1, m_w_ff2, m_ln3_g, m_ln3_b, v_w_in, v_conv_a_w, v_conv_a_b, v_ln_a_g, v_ln_a_b, v_ln_v_g, v_ln_v_b, v_w_s, v_b_s, v_conv_c_w, v_w_out, v_ln1_g, v_ln1_b, v_w_q, v_w_kv, v_w_o, v_ln2_g, v_ln2_b, v_w_ff1, v_w_ff2, v_ln3_g, v_ln3_b):
    given = dict(locals())
    w = {k: given[k] for k in WEIGHTS}
    mom = {k: given["m_" + k] for k in WEIGHTS}
    var = {k: given["v_" + k] for k in WEIGHTS}
    bd, s_len, _ = x.shape
    core = lax.axis_index("c").astype(jnp.int32).reshape(1)

    conv_pack = lambda d: _conv_pack(d["conv_a_w"], d["conv_c_w"])
    sched = _Overlapped({k: w[k].astype(BF16) for k in BIG}, conv_pack(w), core)
    sq, grad_x, grads = _local_step(x.reshape(bd * s_len, D), mem.reshape(-1, D), loss_target.reshape(bd * s_len, D),
                                    sched, {k: w[k] for k in REPLICATED}, bd)
    loss = lax.psum(0.5 * jnp.sum(sq) / D, ("x", "y", "c"))

    out = {}
    for k in BIG + ("conv",):
        own = [sched.own[(l, k)] for l in range(DEPTH)]
        remote = [sched.remote[(l, k)] for l in range(DEPTH)]
        if k == "conv":
            conv_out = _adamw_layers(conv_pack(w), conv_pack(mom), conv_pack(var), own, remote, "adamw_conv")
            unpacked = [_conv_unpack(o) for o in conv_out]
            out["conv_a_w"], out["conv_c_w"] = [u[0] for u in unpacked], [u[1] for u in unpacked]
        else:
            out[k] = _adamw_layers(w[k], mom[k], var[k], own, remote, f"adamw_{k}")

    rep_out = _adamw(_pack_rows([w[k] for k in REPLICATED]), _pack_rows([mom[k] for k in REPLICATED]),
                     _pack_rows([var[k] for k in REPLICATED]), [sched.replicated], "adamw_replicated")
    for i, o in enumerate(rep_out):
        for k, piece in zip(REPLICATED, _unpack_rows(o, [w[k] for k in REPLICATED])):
            out.setdefault(k, [None] * 4)[i] = piece

    res = [loss, grad_x.reshape(bd, s_len, D)]
    for i in range(4):
        res += [out[k][i] for k in WEIGHTS]
    return tuple(res)
```

```python
import functools
import math

import jax
import jax.numpy as jnp
from jax import lax
from jax.experimental import pallas as pl
from jax.experimental.pallas import tpu as pltpu

F32 = jnp.float32
BF16 = jnp.bfloat16

DEPTH = 4
D = 1024
D_A, D_B, D_C = 384, 256, 384
HEAD = 64
B_HEADS = 4
CHUNK = 128
KA, KC = 31, 3
HALO_A, HALO_C = 32, 8
IN_W = 2 * D_A + 2 * D_B + 3 * D_C
X_HEADS = 4
X_HD = D // X_HEADS
D_FF = 4 * D
EPS = 1e-5
ALPHA = (2.0 * DEPTH) ** 0.25
LR, B1, B2, ADAM_EPS, WD, STEP = 0.001, 0.9, 0.999, 1e-08, 0.01, 10
INV_SQRT2 = 0.7071067811865476
INV_SQRT_2PI = 0.3989422804014327
N_DEV = 8
VMEM_LIMIT = 56 * 1024 * 1024
MESH = pl.DeviceIdType.MESH
ANY = pl.BlockSpec(memory_space=pl.ANY)


def _cp(sem=None):
    return pltpu.CompilerParams(dimension_semantics=sem, vmem_limit_bytes=VMEM_LIMIT)


def _sds(shape, dtype):
    return jax.ShapeDtypeStruct(tuple(shape), dtype)


class _Rider:
    def __init__(self, arrays, out_shape, sems, start, finish):
        self.arrays, self.out_shape, self.sems, self.start, self.finish = arrays, out_shape, sems, start, finish


def _call(kern, name, grid, in_specs, out_specs, out_shape, args, sem, scratch=(), rider=None):
    single = not isinstance(out_shape, (list, tuple))
    out_specs_l = [out_specs] if single else list(out_specs)
    out_shape_l = [out_shape] if single else list(out_shape)
    if rider is None:
        res = pl.pallas_call(kern, name=name, grid=grid, in_specs=in_specs, out_specs=out_specs_l,
                             out_shape=out_shape_l, scratch_shapes=list(scratch), compiler_params=_cp(sem))(*args)
        return (res[0] if single else list(res)), None
    n_in, n_out, n_scr = len(args), len(out_shape_l), len(scratch)
    n_rin, n_rout = len(rider.arrays), len(rider.out_shape)

    def body(*refs):
        ins, refs = refs[:n_in], refs[n_in:]
        r_ins, refs = refs[:n_rin], refs[n_rin:]
        outs, refs = refs[:n_out], refs[n_out:]
        r_outs, refs = refs[:n_rout], refs[n_rout:]
        scr, r_sems = refs[:n_scr], refs[n_scr:]
        ids = [pl.program_id(d) for d in range(len(grid))]
        first = functools.reduce(jnp.logical_and, [i == 0 for i in ids])
        last = functools.reduce(jnp.logical_and, [i == g - 1 for i, g in zip(ids, grid)])

        @pl.when(first)
        def _():
            rider.start(r_ins, r_outs, r_sems)

        kern(*ins, *outs, *scr)

        @pl.when(last)
        def _():
            rider.finish(r_ins, r_outs, r_sems)

    res = pl.pallas_call(
        body, name=name, grid=grid, in_specs=list(in_specs) + [ANY] * n_rin,
        out_specs=out_specs_l + [ANY] * n_rout, out_shape=out_shape_l + list(rider.out_shape),
        scratch_shapes=list(scratch) + list(rider.sems),
        compiler_params=_cp(("arbitrary",) * len(grid)))(*args, *rider.arrays)
    mine, theirs = list(res[:n_out]), list(res[n_out:])
    return (mine[0] if single else mine), theirs


def _join_riders(riders):
    if len(riders) == 1:
        return riders[0]

    def parts(seq, attr):
        out, at = [], 0
        for r in riders:
            n = len(getattr(r, attr))
            out.append(seq[at:at + n])
            at += n
        return out

    def each(method):
        def run(ins, outs, sems):
            for r, i, o, s in zip(riders, parts(ins, "arrays"), parts(outs, "out_shape"), parts(sems, "sems")):
                getattr(r, method)(i, o, s)
        return run

    return _Rider(sum([r.arrays for r in riders], []), sum([r.out_shape for r in riders], []),
                  sum([r.sems for r in riders], []), each("start"), each("finish"))


def _ride_alone(rider, name):
    def body(*refs):
        n_rin, n_rout = len(rider.arrays), len(rider.out_shape)
        r_ins, r_outs, r_sems = refs[:n_rin], refs[n_rin:n_rin + n_rout], refs[n_rin + n_rout:]
        rider.start(r_ins, r_outs, r_sems)
        rider.finish(r_ins, r_outs, r_sems)

    return list(pl.pallas_call(
        body, name=name, in_specs=[ANY] * len(rider.arrays), out_specs=[ANY] * len(rider.out_shape),
        out_shape=list(rider.out_shape), scratch_shapes=list(rider.sems))(*rider.arrays))


def _ln(z):
    mu = jnp.mean(z, axis=-1, keepdims=True)
    zc = z - mu
    var = jnp.mean(zc * zc, axis=-1, keepdims=True)
    rstd = lax.rsqrt(var + EPS)
    return zc * rstd, rstd


def _ln_bwd(dxhat, xhat, rstd):
    m1 = jnp.mean(dxhat, axis=-1, keepdims=True)
    m2 = jnp.mean(dxhat * xhat, axis=-1, keepdims=True)
    return rstd * (dxhat - m1 - xhat * m2)


def _gelu(x):
    return 0.5 * x * (1.0 + lax.erf(x * INV_SQRT2))


def _gelu_and_grad(x):
    cdf = 0.5 * (1.0 + lax.erf(x * INV_SQRT2))
    return x * cdf, cdf + x * jnp.exp(-0.5 * x * x) * INV_SQRT_2PI


def _fold8(x):
    r, c = x.shape
    return jnp.sum(x.reshape(r // 8, 8, c), axis=0)


def _relu2(h):
    return jnp.square(jnp.maximum(h, 0.0))


def _weight_spec(block, index_map, resident):
    return pl.BlockSpec(block, index_map, pipeline_mode=pl.Buffered(1) if resident else None)


def _mm_nn(a, w, out_dtype, name, tm=512, tn=None, rider=None):
    n, k = a.shape
    m = w.shape[1]
    tm = min(tm, n)
    tn = m if tn is None else min(tn, m)

    def kern(a_ref, w_ref, o_ref):
        o_ref[...] = jnp.dot(a_ref[...].astype(BF16), w_ref[...], preferred_element_type=F32).astype(out_dtype)

    res, rode = _call(
        kern, name, (n // tm, m // tn),
        [pl.BlockSpec((tm, k), lambda i, j: (i, 0)), _weight_spec((k, tn), lambda i, j: (0, j), tn == m)],
        pl.BlockSpec((tm, tn), lambda i, j: (i, j)), _sds((n, m), out_dtype), (a, w), ("parallel", "parallel"),
        rider=rider)
    return res if rider is None else (res, rode)


def _mm_res_ln(a, w, res, g, b, name, relu2=False, tm=512, rider=None, then=None):
    n, k = a.shape
    tm = min(tm, n)

    m2 = None if then is None else then.shape[1]

    def kern(*refs):
        a_ref, w_ref, res_ref, g_ref, b_ref = refs[:5]
        x_ref, xhat_ref, rstd_ref = refs[-3:] if then is None else refs[-4:-1]
        av = a_ref[...]
        if relu2:
            av = _relu2(av.astype(F32))
        z = ALPHA * res_ref[...] + jnp.dot(av.astype(BF16), w_ref[...], preferred_element_type=F32)
        xhat, rstd = _ln(z)
        xhat_ref[...] = xhat
        rstd_ref[...] = rstd
        x = xhat * g_ref[...] + b_ref[...]
        x_ref[...] = x
        if then is not None:
            refs[-1][...] = jnp.dot(x.astype(BF16), refs[5][...], preferred_element_type=F32).astype(BF16)

    row = lambda i: (i, 0)
    fix = lambda i: (0, 0)
    in_specs = [pl.BlockSpec((tm, k), row), _weight_spec((k, D), fix, True), pl.BlockSpec((tm, D), row),
                pl.BlockSpec((1, D), fix), pl.BlockSpec((1, D), fix)]
    out_specs = [pl.BlockSpec((tm, D), row), pl.BlockSpec((tm, D), row), pl.BlockSpec((tm, 1), row)]
    out_shape = [_sds((n, D), F32), _sds((n, D), F32), _sds((n, 1), F32)]
    args = (a, w, res, g, b)
    if then is not None:
        in_specs.append(_weight_spec((D, m2), fix, True))
        out_specs.append(pl.BlockSpec((tm, m2), row))
        out_shape.append(_sds((n, m2), BF16))
        args += (then,)
    out, rode = _call(kern, name, (n // tm,), in_specs, out_specs, out_shape, args, ("parallel",), rider=rider)
    return out if rider is None else (out, rode)


def _mm_nt(a, w, out_dtype, name, relu2_grad_of=None, tm=512, tn=1024, rider=None):
    n, k = a.shape
    m = w.shape[0]
    tm = min(tm, n)
    tn = min(tn, m)
    with_h = relu2_grad_of is not None

    def kern(*refs):
        a_ref, w_ref = refs[0], refs[1]
        o_ref = refs[-1]
        r = lax.dot_general(a_ref[...].astype(BF16), w_ref[...], (((1,), (1,)), ((), ())), preferred_element_type=F32)
        if with_h:
            r = r * (2.0 * jnp.maximum(refs[2][...].astype(F32), 0.0))
        o_ref[...] = r.astype(out_dtype)

    in_specs = [pl.BlockSpec((tm, k), lambda i, j: (i, 0)), _weight_spec((tn, k), lambda i, j: (j, 0), tn == m)]
    args = [a, w]
    if with_h:
        in_specs.append(pl.BlockSpec((tm, tn), lambda i, j: (i, j)))
        args.append(relu2_grad_of)
    res, rode = _call(kern, name, (n // tm, m // tn), in_specs, pl.BlockSpec((tm, tn), lambda i, j: (i, j)),
                      _sds((n, m), out_dtype), args, ("parallel", "parallel"), rider=rider)
    return res if rider is None else (res, rode)


def _mm_tn_by_owner(a, b, name, tk=1024):
    n, k1 = a.shape
    m = b.shape[1]
    s = m // N_DEV
    tk = min(tk, n)

    def kern(a_ref, b_ref, o_ref):
        @pl.when(pl.program_id(0) == 0)
        def _():
            o_ref[...] = jnp.zeros_like(o_ref)

        av = a_ref[...].astype(BF16)
        for j in range(N_DEV):
            o_ref[j // 2, j % 2] += lax.dot_general(av, b_ref[:, j * s:(j + 1) * s].astype(BF16),
                                                    (((0,), (0,)), ((), ())), preferred_element_type=F32)

    return pl.pallas_call(
        kern, name=name, grid=(n // tk,),
        in_specs=[pl.BlockSpec((tk, k1), lambda k: (k, 0)), pl.BlockSpec((tk, m), lambda k: (k, 0))],
        out_specs=pl.BlockSpec((4, 2, k1, s), lambda k: (0, 0, 0, 0), pipeline_mode=pl.Buffered(1)),
        out_shape=_sds((4, 2, k1, s), F32), compiler_params=_cp(("arbitrary",)))(a, b)


def _mm_tn(a, b, name, relu2=False, t1=1024, tn=2048, tk=1024):
    n, k1 = a.shape
    m = b.shape[1]
    t1 = min(t1, k1)
    tn = m if m <= 2432 and m % tn else min(tn, m)
    tk = min(tk, n)

    def kern(a_ref, b_ref, o_ref):
        @pl.when(pl.program_id(2) == 0)
        def _():
            o_ref[...] = jnp.zeros_like(o_ref)

        av = a_ref[...]
        if relu2:
            av = _relu2(av.astype(F32))
        o_ref[...] += lax.dot_general(av.astype(BF16), b_ref[...].astype(BF16), (((0,), (0,)), ((), ())),
                                      preferred_element_type=F32)

    return pl.pallas_call(
        kern, name=name, grid=(k1 // t1, m // tn, n // tk),
        in_specs=[pl.BlockSpec((tk, t1), lambda i, j, k: (k, i)), pl.BlockSpec((tk, tn), lambda i, j, k: (k, j))],
        out_specs=pl.BlockSpec((t1, tn), lambda i, j, k: (i, j)),
        out_shape=_sds((k1, m), F32),
        compiler_params=_cp(("parallel", "parallel", "arbitrary")))(a, b)


def _bwd_in(dz_next, da, w, name, ln=None, tm=512, rider=None, then=None):
    n, k2 = da.shape
    tm = min(tm, n)
    row = lambda i: (i, 0)
    fix = lambda i: (0, 0)

    def dx_of(dzn_ref, da_ref, w_ref):
        return ALPHA * dzn_ref[...] + lax.dot_general(da_ref[...], w_ref[...], (((1,), (1,)), ((), ())),
                                                      preferred_element_type=F32)

    base_specs = [pl.BlockSpec((tm, D), row), pl.BlockSpec((tm, k2), row), _weight_spec((D, k2), fix, True)]
    if ln is None:
        def kern(dzn_ref, da_ref, w_ref, dx_ref):
            dx_ref[...] = dx_of(dzn_ref, da_ref, w_ref)

        out, rode = _call(kern, name, (n // tm,), base_specs, pl.BlockSpec((tm, D), row), _sds((n, D), F32),
                          (dz_next, da, w), ("parallel",), rider=rider)
        return out if rider is None else (out, rode)

    xhat, rstd, g = ln

    def kern(*refs):
        dzn_ref, da_ref, w_ref, xhat_ref, rstd_ref, g_ref = refs[:6]
        dz_ref, dg_ref, db_ref = refs[-3:] if then is None else refs[-4:-1]

        @pl.when(pl.program_id(0) == 0)
        def _():
            dg_ref[...] = jnp.zeros_like(dg_ref)
            db_ref[...] = jnp.zeros_like(db_ref)

        dx = dx_of(dzn_ref, da_ref, w_ref)
        xh = xhat_ref[...]
        dg_ref[...] += _fold8(dx * xh)
        db_ref[...] += _fold8(dx)
        dz = _ln_bwd(dx * g_ref[...], xh, rstd_ref[...])
        dz_ref[...] = dz
        if then is not None:
            refs[-1][...] = lax.dot_general(dz.astype(BF16), refs[6][...], (((1,), (1,)), ((), ())),
                                            preferred_element_type=F32).astype(BF16)

    in_specs = base_specs + [pl.BlockSpec((tm, D), row), pl.BlockSpec((tm, 1), row), pl.BlockSpec((1, D), fix)]
    out_specs = [pl.BlockSpec((tm, D), row), pl.BlockSpec((8, D), fix), pl.BlockSpec((8, D), fix)]
    out_shape = [_sds((n, D), F32), _sds((8, D), F32), _sds((8, D), F32)]
    args = (dz_next, da, w, xhat, rstd, g)
    if then is not None:
        m3 = then.shape[0]
        in_specs.append(_weight_spec((m3, D), fix, True))
        out_specs.append(pl.BlockSpec((tm, m3), row))
        out_shape.append(_sds((n, m3), BF16))
        args += (then,)
    out, rode = _call(kern, name, (n // tm,), in_specs, out_specs, out_shape, args, ("arbitrary",), rider=rider)
    return out if rider is None else (out, rode)


def _loss_lnbwd(x, target, xhat, rstd, g, name, tm=512):
    n = x.shape[0]
    tm = min(tm, n)
    row = lambda i: (i, 0)
    fix = lambda i: (0, 0)

    def kern(x_ref, t_ref, xhat_ref, rstd_ref, g_ref, sq_ref, dz_ref, dg_ref, db_ref):
        @pl.when(pl.program_id(0) == 0)
        def _():
            sq_ref[...] = jnp.zeros_like(sq_ref)
            dg_ref[...] = jnp.zeros_like(dg_ref)
            db_ref[...] = jnp.zeros_like(db_ref)

        err = x_ref[...] - t_ref[...]
        sq_ref[...] += _fold8(err * err)
        dx = err * (1.0 / D)
        xh = xhat_ref[...]
        dg_ref[...] += _fold8(dx * xh)
        db_ref[...] += _fold8(dx)
        dz_ref[...] = _ln_bwd(dx * g_ref[...], xh, rstd_ref[...])

    return pl.pallas_call(
        kern, name=name, grid=(n // tm,),
        in_specs=[pl.BlockSpec((tm, D), row), pl.BlockSpec((tm, D), row), pl.BlockSpec((tm, D), row),
                  pl.BlockSpec((tm, 1), row), pl.BlockSpec((1, D), fix)],
        out_specs=[pl.BlockSpec((8, D), fix), pl.BlockSpec((tm, D), row), pl.BlockSpec((8, D), fix),
                   pl.BlockSpec((8, D), fix)],
        out_shape=[_sds((8, D), F32), _sds((n, D), F32), _sds((8, D), F32), _sds((8, D), F32)],
        compiler_params=_cp(("arbitrary",)))(x, target, xhat, rstd, g)


def _softmax_rows(s):
    s = s - jnp.max(s, axis=-1, keepdims=True)
    e = jnp.exp(s)
    return e / jnp.sum(e, axis=-1, keepdims=True)


def _attn_fwd(q, kv, bd, name, tm=512):
    n = q.shape[0]
    s_len = n // bd
    m_len = kv.shape[0] // bd
    tm = min(tm, s_len)
    nt = s_len // tm
    scale = X_HD ** -0.5

    def kern(q_ref, k_ref, v_ref, o_ref):
        for h in range(X_HEADS):
            cs = slice(h * X_HD, (h + 1) * X_HD)
            s = lax.dot_general(q_ref[:, cs], k_ref[:, cs], (((1,), (1,)), ((), ())), preferred_element_type=F32)
            p = _softmax_rows(s * scale)
            o_ref[:, cs] = jnp.dot(p.astype(BF16), v_ref[:, cs], preferred_element_type=F32).astype(BF16)

    return pl.pallas_call(
        kern, name=name, grid=(bd, nt),
        in_specs=[pl.BlockSpec((tm, D), lambda b, i: (b * nt + i, 0)),
                  pl.BlockSpec((m_len, D), lambda b, i: (b, 0)), pl.BlockSpec((m_len, D), lambda b, i: (b, 1))],
        out_specs=pl.BlockSpec((tm, D), lambda b, i: (b * nt + i, 0)),
        out_shape=_sds((n, D), BF16),
        compiler_params=_cp(("parallel", "parallel")))(q, kv, kv)


def _attn_bwd(q, kv, do, bd, name, tm=512, rider=None):
    n = q.shape[0]
    s_len = n // bd
    m_len = kv.shape[0] // bd
    tm = min(tm, s_len)
    nt = s_len // tm
    scale = X_HD ** -0.5

    def kern(q_ref, k_ref, v_ref, do_ref, dq_ref, dkv_ref):
        @pl.when(pl.program_id(1) == 0)
        def _():
            dkv_ref[...] = jnp.zeros_like(dkv_ref)

        for h in range(X_HEADS):
            cs = slice(h * X_HD, (h + 1) * X_HD)
            vs = slice(D + h * X_HD, D + (h + 1) * X_HD)
            qh, kh, vh, doh = q_ref[:, cs], k_ref[:, cs], v_ref[:, cs], do_ref[:, cs]
            s = lax.dot_general(qh, kh, (((1,), (1,)), ((), ())), preferred_element_type=F32)
            p = _softmax_rows(s * scale)
            pb = p.astype(BF16)
            dp = lax.dot_general(doh, vh, (((1,), (1,)), ((), ())), preferred_element_type=F32)
            dkv_ref[:, vs] += lax.dot_general(pb, doh, (((0,), (0,)), ((), ())), preferred_element_type=F32)
            ds = (p * (dp - jnp.sum(dp * p, axis=-1, keepdims=True)) * scale).astype(BF16)
            dq_ref[:, cs] = jnp.dot(ds, kh, preferred_element_type=F32).astype(BF16)
            dkv_ref[:, cs] += lax.dot_general(ds, qh, (((0,), (0,)), ((), ())), preferred_element_type=F32)

    out, rode = _call(
        kern, name, (bd, nt),
        [pl.BlockSpec((tm, D), lambda b, i: (b * nt + i, 0)),
         pl.BlockSpec((m_len, D), lambda b, i: (b, 0)), pl.BlockSpec((m_len, D), lambda b, i: (b, 1)),
         pl.BlockSpec((tm, D), lambda b, i: (b * nt + i, 0))],
        [pl.BlockSpec((tm, D), lambda b, i: (b * nt + i, 0)), pl.BlockSpec((m_len, 2 * D), lambda b, i: (b, 0))],
        [_sds((n, D), BF16), _sds((bd * m_len, 2 * D), F32)], (q, kv, kv, do), ("parallel", "arbitrary"), rider=rider)
    return out if rider is None else (out, rode)


C_AV, C_AG, C_BU, C_BV, C_CB, C_CC, C_CX = 0, 384, 768, 1024, 1280, 1664, 2048


def _taps_by_phase(offsets):
    by_phase = {}
    for k, o in enumerate(offsets):
        by_phase.setdefault(o % 8, []).append((o // 8, k))
    return sorted(by_phase.items())


def _window_rows(win_ref, r0, a, rows, cs):
    return win_ref[pl.ds(pl.multiple_of(r0 + 8 * a, 8), rows), cs]


def _conv_taps(win_ref, r0, w_ref, offsets):
    parts = []
    for cb in range(3):
        cs = slice(cb * 128, (cb + 1) * 128)
        acc = jnp.zeros((CHUNK, 128), F32)
        for b, taps in _taps_by_phase(offsets):
            rows = CHUNK if b == 0 else CHUNK + 8
            part = jnp.zeros((rows, 128), F32)
            for a, k in taps:
                part = part + _window_rows(win_ref, r0, a, rows, cs) * w_ref[k:k + 1, cs]
            acc = acc + (part if b == 0 else part[b:b + CHUNK, :])
        parts.append(acc)
    return jnp.concatenate(parts, axis=1)


def _tap_grads(acc_ref, win_ref, r0, d, offsets, shifted_ref):
    for cb in range(3):
        cs = slice(cb * 128, (cb + 1) * 128)
        padded = jnp.concatenate([jnp.zeros((8, 128), F32), d[:, cs], jnp.zeros((8, 128), F32)], axis=0)
        for b, taps in _taps_by_phase(offsets):
            if b == 0:
                rows, db = CHUNK, d[:, cs]
            else:
                rows = CHUNK + 8
                shifted_ref[...] = padded[8 - b:8 - b + rows, :]
                db = shifted_ref[...]
            for a, k in taps:
                acc_ref[k * 8:(k + 1) * 8, cs] += _fold8(db * _window_rows(win_ref, r0, a, rows, cs))


def _causal_offsets(halo, n_taps):
    return [halo - (n_taps - 1) + k for k in range(n_taps)]


def _anticausal_offsets(n_taps):
    return [n_taps - 1 - k for k in range(n_taps)]


def _head_of_lane():
    return lax.broadcasted_iota(jnp.int32, (1, D_B), 1) // HEAD


def _spatial_mix(wm_ref, vb, head):
    mixed = jnp.zeros((CHUNK, D_B), F32)
    for h in range(B_HEADS):
        mh = jnp.dot(wm_ref[h], vb, preferred_element_type=F32)
        mixed = jnp.where(head == h, mh, mixed)
    return mixed


def _mixer_fwd(proj, p, bd, name, rider=None):
    n = proj.shape[0]
    s_len = n // bd
    n_chunks = s_len // CHUNK

    def kern(proj_ref, caw_ref, cab_ref, lag_ref, lab_ref, lvg_ref, lvb_ref, wm_ref, bsx_ref, ccw_ref, cat_ref,
             ca_ref, gs_ref, ccs_ref):
        gs_ref[0:HALO_A, :] = jnp.zeros((HALO_A, D_A), F32)
        ccs_ref[0:HALO_C, :] = jnp.zeros((HALO_C, D_C), F32)
        head = _head_of_lane()

        def chunk(i, carry):
            r0 = pl.multiple_of(i * CHUNK, CHUNK)
            rows = pl.ds(r0, CHUNK)
            ld = lambda c0, w: proj_ref[rows, c0:c0 + w].astype(F32)
            gs_ref[pl.ds(r0 + HALO_A, CHUNK), :] = ld(C_AV, D_A) * jax.nn.sigmoid(ld(C_AG, D_A))
            ca = _conv_taps(gs_ref, r0, caw_ref, _causal_offsets(HALO_A, KA)) + cab_ref[...]
            ca_ref[rows, :] = ca
            lna = _ln(ca)[0] * lag_ref[...] + lab_ref[...]
            cat_ref[rows, 0:D_A] = (lna * jax.nn.sigmoid(lna)).astype(BF16)
            u = _gelu(ld(C_BU, D_B))
            v = _ln(_gelu(ld(C_BV, D_B)))[0] * lvg_ref[...] + lvb_ref[...]
            mixed = _spatial_mix(wm_ref, v.astype(BF16), head) + bsx_ref[...]
            cat_ref[rows, D_A:D_A + D_B] = (u * mixed).astype(BF16)
            ccs_ref[pl.ds(r0 + HALO_C, CHUNK), :] = ld(C_CC, D_C) * ld(C_CX, D_C)
            conv = _conv_taps(ccs_ref, r0, ccw_ref, _causal_offsets(HALO_C, KC))
            cat_ref[rows, D_A + D_B:D] = (ld(C_CB, D_C) * conv).astype(BF16)
            return carry

        lax.fori_loop(0, n_chunks, chunk, 0)

    fix2 = lambda b: (0, 0)
    args = [proj, p["caw"], p["cab"], p["lag"], p["lab"], p["lvg"], p["lvb"], p["wm"], p["bsx"], p["ccw"]]
    in_specs = [pl.BlockSpec((s_len, IN_W), lambda b: (b, 0))]
    for a in args[1:]:
        in_specs.append(pl.BlockSpec(a.shape, (lambda b: (0, 0, 0)) if a.ndim == 3 else fix2))
    res, rode = _call(
        kern, name, (bd,), in_specs,
        [pl.BlockSpec((s_len, D), lambda b: (b, 0)), pl.BlockSpec((s_len, D_A), lambda b: (b, 0))],
        [_sds((n, D), BF16), _sds((n, D_A), F32)], args, ("parallel",),
        scratch=[pltpu.VMEM((s_len + HALO_A, D_A), F32), pltpu.VMEM((s_len + HALO_C, D_C), F32)], rider=rider)
    return res if rider is None else (res, rode)


def _mixer_bwd(proj, dcat, ca, p, bd, name, rider=None):
    n = proj.shape[0]
    s_len = n // bd
    n_chunks = s_len // CHUNK

    def kern(proj_ref, dcat_ref, ca_ref, caw_ref, cab_ref, lag_ref, lab_ref, lvg_ref, lvb_ref, wm_ref, bsx_ref, ccw_ref,
             dproj_ref, dcaw_ref, dcab_ref, dlag_ref, dlab_ref, dlvg_ref, dlvb_ref, dws_ref, dbs_ref, dccw_ref,
             gs_ref, dcas_ref, ccs_ref, dcs_ref, a_caw, a_cab, a_lag, a_lab, a_lvg, a_lvb, a_ccw, shifted_ref):
        gs_ref[0:HALO_A, :] = jnp.zeros((HALO_A, D_A), F32)
        ccs_ref[0:HALO_C, :] = jnp.zeros((HALO_C, D_C), F32)
        dcas_ref[s_len:s_len + HALO_A, :] = jnp.zeros((HALO_A, D_A), F32)
        dcs_ref[s_len:s_len + HALO_C, :] = jnp.zeros((HALO_C, D_C), F32)
        for acc in (a_caw, a_cab, a_lag, a_lab, a_lvg, a_lvb, a_ccw, dws_ref, dbs_ref):
            acc[...] = jnp.zeros_like(acc)
        head = _head_of_lane()
        lane128 = lax.broadcasted_iota(jnp.int32, (1, CHUNK), 1)

        def pass1(i, carry):
            r0 = pl.multiple_of(i * CHUNK, CHUNK)
            rows = pl.ds(r0, CHUNK)
            ld = lambda c0, w: proj_ref[rows, c0:c0 + w].astype(F32)
            dld = lambda c0, w: dcat_ref[rows, c0:c0 + w].astype(F32)
            gs_ref[pl.ds(r0 + HALO_A, CHUNK), :] = ld(C_AV, D_A) * jax.nn.sigmoid(ld(C_AG, D_A))
            xh, rstd = _ln(ca_ref[rows, :])
            lna = xh * lag_ref[...] + lab_ref[...]
            sg = jax.nn.sigmoid(lna)
            dlna = dld(0, D_A) * (sg * (1.0 + lna * (1.0 - sg)))
            a_lag[...] += _fold8(dlna * xh)
            a_lab[...] += _fold8(dlna)
            dca = _ln_bwd(dlna * lag_ref[...], xh, rstd)
            dcas_ref[rows, :] = dca
            a_cab[...] += _fold8(dca)
            _tap_grads(a_caw, gs_ref, r0, dca, _causal_offsets(HALO_A, KA), shifted_ref)
            pu, pv = ld(C_BU, D_B), ld(C_BV, D_B)
            u, du_dpu = _gelu_and_grad(pu)
            gv, dgv_dpv = _gelu_and_grad(pv)
            vxh, vrstd = _ln(gv)
            v = vxh * lvg_ref[...] + lvb_ref[...]
            vb = v.astype(BF16)
            mixed = _spatial_mix(wm_ref, vb, head) + bsx_ref[...]
            dbo = dld(D_A, D_B)
            dproj_ref[rows, C_BU:C_BU + D_B] = (dbo * mixed * du_dpu).astype(BF16)
            dmixed = dbo * u
            dv = jnp.zeros((CHUNK, D_B), F32)
            bsum = jnp.zeros((CHUNK, CHUNK), F32)
            for h in range(B_HEADS):
                dmh = jnp.where(head == h, dmixed, 0.0)
                dmb = dmh.astype(BF16)
                dvh = lax.dot_general(wm_ref[h], dmb, (((0,), (0,)), ((), ())), preferred_element_type=F32)
                dv = jnp.where(head == h, dvh, dv)
                dws_ref[h] += lax.dot_general(dmb, vb, (((1,), (1,)), ((), ())), preferred_element_type=F32)
                bsum = bsum + jnp.where(lane128 == h, jnp.sum(dmh, axis=-1, keepdims=True), 0.0)
            dbs_ref[...] += bsum
            a_lvg[...] += _fold8(dv * vxh)
            a_lvb[...] += _fold8(dv)
            dgv = _ln_bwd(dv * lvg_ref[...], vxh, vrstd)
            dproj_ref[rows, C_BV:C_BV + D_B] = (dgv * dgv_dpv).astype(BF16)
            ccs_ref[pl.ds(r0 + HALO_C, CHUNK), :] = ld(C_CC, D_C) * ld(C_CX, D_C)
            conv = _conv_taps(ccs_ref, r0, ccw_ref, _causal_offsets(HALO_C, KC))
            dco = dld(D_A + D_B, D_C)
            dproj_ref[rows, C_CB:C_CB + D_C] = (dco * conv).astype(BF16)
            dconv = dco * ld(C_CB, D_C)
            dcs_ref[rows, :] = dconv
            _tap_grads(a_ccw, ccs_ref, r0, dconv, _causal_offsets(HALO_C, KC), shifted_ref)
            return carry

        lax.fori_loop(0, n_chunks, pass1, 0)

        def pass2(i, carry):
            r0 = pl.multiple_of(i * CHUNK, CHUNK)
            rows = pl.ds(r0, CHUNK)
            ld = lambda c0, w: proj_ref[rows, c0:c0 + w].astype(F32)
            dg = _conv_taps(dcas_ref, r0, caw_ref, _anticausal_offsets(KA))
            pa = ld(C_AV, D_A)
            sg = jax.nn.sigmoid(ld(C_AG, D_A))
            dproj_ref[rows, C_AV:C_AV + D_A] = (dg * sg).astype(BF16)
            dproj_ref[rows, C_AG:C_AG + D_A] = (dg * pa * sg * (1.0 - sg)).astype(BF16)
            dcc = _conv_taps(dcs_ref, r0, ccw_ref, _anticausal_offsets(KC))
            dproj_ref[rows, C_CC:C_CC + D_C] = (dcc * ld(C_CX, D_C)).astype(BF16)
            dproj_ref[rows, C_CX:C_CX + D_C] = (dcc * ld(C_CC, D_C)).astype(BF16)
            return carry

        lax.fori_loop(0, n_chunks, pass2, 0)

        for k in range(KA):
            dcaw_ref[k:k + 1, :] = jnp.sum(a_caw[k * 8:(k + 1) * 8, :], axis=0, keepdims=True)
        dcaw_ref[KA:KA + 1, :] = jnp.zeros((1, D_A), F32)
        for k in range(8):
            if k < KC:
                dccw_ref[k:k + 1, :] = jnp.sum(a_ccw[k * 8:(k + 1) * 8, :], axis=0, keepdims=True)
            else:
                dccw_ref[k:k + 1, :] = jnp.zeros((1, D_C), F32)
        dcab_ref[...] = a_cab[...]
        dlag_ref[...] = a_lag[...]
        dlab_ref[...] = a_lab[...]
        dlvg_ref[...] = a_lvg[...]
        dlvb_ref[...] = a_lvb[...]

    fix2 = lambda b: (0, 0)
    args = [proj, dcat, ca, p["caw"], p["cab"], p["lag"], p["lab"], p["lvg"], p["lvb"], p["wm"], p["bsx"], p["ccw"]]
    once = pl.Buffered(1)
    in_specs = [pl.BlockSpec((s_len, IN_W), lambda b: (b, 0), pipeline_mode=once),
                pl.BlockSpec((s_len, D), lambda b: (b, 0), pipeline_mode=once),
                pl.BlockSpec((s_len, D_A), lambda b: (b, 0), pipeline_mode=once)]
    for a in args[3:]:
        in_specs.append(pl.BlockSpec(a.shape, (lambda b: (0, 0, 0)) if a.ndim == 3 else fix2))

    def per_seq(*shape):
        nd = len(shape)
        return (pl.BlockSpec((None,) + shape, lambda b: (b,) + (0,) * nd), _sds((bd,) + shape, F32))

    outs = [(pl.BlockSpec((s_len, IN_W), lambda b: (b, 0), pipeline_mode=once), _sds((n, IN_W), BF16)),
            per_seq(32, D_A), per_seq(8, D_A), per_seq(8, D_A), per_seq(8, D_A), per_seq(8, D_B), per_seq(8, D_B),
            per_seq(B_HEADS, CHUNK, CHUNK), per_seq(CHUNK, CHUNK), per_seq(8, D_C)]
    res, rode = _call(
        kern, name, (bd,), in_specs, [o[0] for o in outs], [o[1] for o in outs], args, ("parallel",),
        scratch=[pltpu.VMEM((s_len + HALO_A, D_A), F32), pltpu.VMEM((s_len + HALO_A, D_A), F32),
                 pltpu.VMEM((s_len + HALO_C, D_C), F32), pltpu.VMEM((s_len + HALO_C, D_C), F32),
                 pltpu.VMEM((KA * 8, D_A), F32), pltpu.VMEM((8, D_A), F32), pltpu.VMEM((8, D_A), F32),
                 pltpu.VMEM((8, D_A), F32), pltpu.VMEM((8, D_B), F32), pltpu.VMEM((8, D_B), F32),
                 pltpu.VMEM((KC * 8, D_C), F32), pltpu.VMEM((CHUNK + 8, 128), F32)],
        rider=rider)
    return res if rider is None else (res, rode)


def _adamw_update(w, m, v, g):
    mn = B1 * m + (1.0 - B1) * g
    vn = B2 * v + (1.0 - B2) * jnp.square(g)
    m_hat = mn / (1.0 - B1 ** STEP)
    v_hat = vn / (1.0 - B2 ** STEP)
    return -LR * (m_hat / (jnp.sqrt(v_hat) + ADAM_EPS) + WD * w), mn, vn


def _adamw_layers(w, m, v, own, remote, name):
    shape = w.shape
    depth, c = shape[0], shape[-1]
    r = math.prod(shape[1:-1])
    tr = 256 if r % 256 == 0 else r
    nt = r // tr

    def kern(*refs):
        w_ref, m_ref, v_ref = refs[:3]
        own_refs, rem_refs = refs[3:3 + depth], refs[3 + depth:3 + 4 * depth]
        go_ref, d_ref, mo_ref, vo_ref = refs[3 + 4 * depth:]
        for lp in range(depth):
            @pl.when(pl.program_id(0) == lp)
            def _(lp=lp):
                g = own_refs[lp][...]
                for j in range(3):
                    g = g + rem_refs[3 * lp + j][...].astype(F32)
                go_ref[...] = g
                d_ref[...], mo_ref[...], vo_ref[...] = _adamw_update(w_ref[...], m_ref[...], v_ref[...], g)

    def rows_of(lp):
        return lambda l, i: jnp.where(l == lp, i, jnp.where(l < lp, 0, nt - 1))

    spec = pl.BlockSpec((None, tr, c), lambda l, i: (l, i, 0))
    own_specs = [pl.BlockSpec((tr, c), functools.partial(lambda f, l, i: (f(l, i), 0), rows_of(lp)))
                 for lp in range(depth)]
    rem_specs = [pl.BlockSpec((None, tr, c), functools.partial(lambda f, j, l, i: (j, f(l, i), 0), rows_of(lp), j))
                 for lp in range(depth) for j in range(3)]
    as3d = lambda a: a.reshape(depth, r, c)
    outs = pl.pallas_call(
        kern, name=name, grid=(depth, nt), in_specs=[spec] * 3 + own_specs + rem_specs, out_specs=[spec] * 4,
        out_shape=[_sds((depth, r, c), F32)] * 4, compiler_params=_cp(("arbitrary", "arbitrary")))(
            as3d(w), as3d(m), as3d(v), *[o.reshape(r, c) for o in own],
            *[x.reshape(3, r, c) for x in remote for _ in range(3)])
    return [o.reshape(shape) for o in outs]


def _adamw(w, m, v, g_parts, name):
    shape = w.shape
    c = shape[-1]
    r = math.prod(shape[:-1])
    as2d = lambda a: a.reshape(r, c)
    tr = 512 if r % 512 == 0 else r
    slots = [(a.reshape(a.shape[0], r, c), p) for a in g_parts for p in range(a.shape[0])]
    n_g = len(slots)

    def kern(*refs):
        w_ref, m_ref, v_ref = refs[:3]
        g_refs = refs[3:3 + n_g]
        go_ref, d_ref, mo_ref, vo_ref = refs[3 + n_g:]
        g = g_refs[0][...].astype(F32)
        for gr in g_refs[1:]:
            g = g + gr[...].astype(F32)
        go_ref[...] = g
        d_ref[...], mo_ref[...], vo_ref[...] = _adamw_update(w_ref[...], m_ref[...], v_ref[...], g)

    spec = pl.BlockSpec((tr, c), lambda i: (i, 0))
    g_specs = [pl.BlockSpec((None, tr, c), functools.partial(lambda p, i: (p, i, 0), p)) for _, p in slots]
    outs = pl.pallas_call(
        kern, name=name, grid=(r // tr,), in_specs=[spec] * 3 + g_specs, out_specs=[spec] * 4,
        out_shape=[_sds((r, c), F32)] * 4, compiler_params=_cp(("parallel",)))(
            as2d(w), as2d(m), as2d(v), *[a for a, _ in slots])
    return [o.reshape(shape) for o in outs]


def _place():
    x, y, c = lax.axis_index("x"), lax.axis_index("y"), lax.axis_index("c")
    other_chips = [(1 - x, y), (x, 1 - y), (1 - x, 1 - y)]
    return x, y, c, other_chips


def _gather_rider(arrays):
    n_arr = len(arrays)

    def parts(ins, outs, sems):
        send_sems, recv_sems, local_sems = sems
        x, y, c, chips = _place()
        me, sibling = (x, y, c), (x, y, 1 - c)
        slot = lambda px, py, pc: 4 * px + 2 * py + pc

        def copy(a, k, block, to, from_input=False):
            dst = outs[a].at[slot(*block)]
            return pltpu.make_async_remote_copy(
                src_ref=ins[a] if from_input else dst, dst_ref=dst, send_sem=send_sems.at[k, a],
                recv_sem=recv_sems.at[k, a], device_id=to, device_id_type=MESH)

        mine = [pltpu.make_async_copy(ins[a], outs[a].at[slot(*me)], local_sems.at[a]) for a in range(n_arr)]
        first = []
        for a in range(n_arr):
            first.append(copy(a, 0, me, sibling, True))
            first += [copy(a, 1 + j, me, (*chip, c), True) for j, chip in enumerate(chips)]
        return copy, mine, first, me, sibling, chips, c

    def start(ins, outs, sems):
        _, mine, first, *_ = parts(ins, outs, sems)
        for cp in mine + first:
            cp.start()

    def finish(ins, outs, sems):
        copy, mine, first, me, sibling, chips, c = parts(ins, outs, sems)
        passed = []
        for j, chip in enumerate(chips):
            for a in range(n_arr):
                copy(a, 1 + j, (*chip, c), me).wait_recv()
                passed.append(copy(a, 4 + j, (*chip, c), sibling))
                passed[-1].start()
        for a in range(n_arr):
            copy(a, 0, sibling, me).wait_recv()
            for j, chip in enumerate(chips):
                copy(a, 4 + j, (*chip, 1 - c), me).wait_recv()
        for cp in first + passed:
            cp.wait_send()
        for cp in mine:
            cp.wait()

    return _Rider(list(arrays), [_sds((N_DEV,) + a.shape, a.dtype) for a in arrays],
                  [pltpu.SemaphoreType.DMA((7, n_arr)), pltpu.SemaphoreType.DMA((7, n_arr)),
                   pltpu.SemaphoreType.DMA((n_arr,))], start, finish)


def _exchange_rider(arrays):
    n_arr = len(arrays)

    def copies(ins, outs, sems):
        send_sems, recv_sems = sems
        x, y, c, _ = _place()
        return [pltpu.make_async_remote_copy(
            src_ref=ins[a].at[:, 1 - c], dst_ref=outs[a], send_sem=send_sems.at[a], recv_sem=recv_sems.at[a],
            device_id=(x, y, 1 - c), device_id_type=MESH) for a in range(n_arr)]

    def start(ins, outs, sems):
        for cp in copies(ins, outs, sems):
            cp.start()

    def finish(ins, outs, sems):
        for cp in copies(ins, outs, sems):
            cp.wait()

    return _Rider(list(arrays), [_sds(a.shape[:1] + a.shape[2:], a.dtype) for a in arrays],
                  [pltpu.SemaphoreType.DMA((n_arr,)), pltpu.SemaphoreType.DMA((n_arr,))], start, finish)


def _pair_add(mine, theirs, core, name):
    _, _, r, c = mine.shape
    tr = 512 if r % 512 == 0 else r

    def kern(core_ref, a_ref, b_ref, o_ref, ob_ref):
        s = a_ref[...] + b_ref[...]
        o_ref[...] = s
        ob_ref[...] = s.astype(BF16)

    row = lambda t, i, core_ref: (t, i, 0)
    return pl.pallas_call(
        kern, name=name,
        grid_spec=pltpu.PrefetchScalarGridSpec(
            num_scalar_prefetch=1, grid=(4, r // tr),
            in_specs=[pl.BlockSpec((None, None, tr, c), lambda t, i, core_ref: (t, core_ref[0], i, 0)),
                      pl.BlockSpec((None, tr, c), row)],
            out_specs=[pl.BlockSpec((None, tr, c), row), pl.BlockSpec((None, tr, c), row)]),
        out_shape=[_sds((4, r, c), F32), _sds((4, r, c), BF16)],
        compiler_params=_cp(("parallel", "parallel")))(core, mine, theirs)


def _scatter_rider(sums, sums_bf16):
    n_arr = len(sums)

    def copies(ins, outs, sems):
        send_sems, recv_sems, local_sems = sems
        x, y, c, chips = _place()
        own = [pltpu.make_async_copy(ins[a].at[2 * x + y], outs[a], local_sems.at[a]) for a in range(n_arr)]
        remote = []
        for a in range(n_arr):
            for j, (px, py) in enumerate(chips):
                remote.append(pltpu.make_async_remote_copy(
                    src_ref=ins[n_arr + a].at[2 * px + py], dst_ref=outs[n_arr + a].at[j], send_sem=send_sems.at[j, a],
                    recv_sem=recv_sems.at[j, a], device_id=(px, py, c), device_id_type=MESH))
        return own + remote

    def start(ins, outs, sems):
        for cp in copies(ins, outs, sems):
            cp.start()

    def finish(ins, outs, sems):
        for cp in copies(ins, outs, sems):
            cp.wait()

    return _Rider(list(sums) + list(sums_bf16),
                  [_sds(a.shape[1:], a.dtype) for a in sums] + [_sds((3,) + a.shape[1:], a.dtype) for a in sums_bf16],
                  [pltpu.SemaphoreType.DMA((3, n_arr)), pltpu.SemaphoreType.DMA((3, n_arr)),
                   pltpu.SemaphoreType.DMA((n_arr,))], start, finish)


def _mixer_params(conv_a_w, conv_a_b, ln_a_g, ln_a_b, ln_v_g, ln_v_b, w_s, b_s, conv_c_w):
    causal = jnp.tril(jnp.ones((CHUNK, CHUNK), dtype=bool))
    row = lambda a: a.reshape(1, -1)
    return dict(
        caw=jnp.pad(conv_a_w, ((0, 32 - KA), (0, 0))), cab=row(conv_a_b), lag=row(ln_a_g), lab=row(ln_a_b),
        lvg=row(ln_v_g), lvb=row(ln_v_b), wm=jnp.where(causal[None], w_s, 0.0).astype(BF16),
        bsx=jnp.repeat(b_s.T, HEAD, axis=1), ccw=jnp.pad(conv_c_w, ((0, 8 - KC), (0, 0))))


class _Schedule:
    def __init__(self, big=None):
        self.big = big

    def weights(self, l):
        return {k: v[l] for k, v in self.big.items()}

    def rider(self, stage, l):
        return None

    def rode(self, stage, l, results):
        pass

    def note_grads(self, l, grads):
        pass

    def finish(self):
        pass


def _local_step(x, mem, target, sched, small, bd):
    row = lambda a: a.reshape(1, -1)

    def ride(stage, l, fn, *args, **kw):
        rider = sched.rider(stage, l)
        res = fn(*args, rider=rider, **kw)
        if rider is not None:
            res, results = res
            sched.rode(stage, l, results)
        return res

    saved = []
    for l in range(DEPTH):
        big = sched.weights(l)
        mp = _mixer_params(big["conv_a_w"], *[small[k][l] for k in ("conv_a_b", "ln_a_g", "ln_a_b", "ln_v_g", "ln_v_b",
                                                                    "w_s", "b_s")], big["conv_c_w"])
        proj = ride("in_proj", l, _mm_nn, x, big["w_in"], BF16, f"in_proj_{l}")
        cat, ca = ride("mixer_fwd", l, _mixer_fwd, proj, mp, bd, f"mixer_fwd_{l}")
        x1, xh1, rs1, q = ride("out_proj_ln1", l, _mm_res_ln, cat, big["w_out"], x, row(small["ln1_g"][l]),
                               row(small["ln1_b"][l]), f"out_proj_ln1_q_{l}", then=big["w_q"])
        kv = _mm_nn(mem, big["w_kv"], BF16, f"kv_proj_{l}")
        o = _attn_fwd(q, kv, bd, f"attn_fwd_{l}")
        x2, xh2, rs2, h = ride("ff1", l, _mm_res_ln, o, big["w_o"], x1, row(small["ln2_g"][l]), row(small["ln2_b"][l]),
                               f"o_proj_ln2_ff1_{l}", then=big["w_ff1"])
        x3, xh3, rs3 = ride("ff2_ln3", l, _mm_res_ln, h, big["w_ff2"], x2, row(small["ln3_g"][l]),
                            row(small["ln3_b"][l]), f"ff2_ln3_{l}", relu2=True)
        saved.append(dict(mp=mp, x=x, proj=proj, cat=cat, ca=ca, x1=x1, xh1=xh1, rs1=rs1, q=q, kv=kv, o=o, x2=x2, xh2=xh2,
                          rs2=rs2, h=h, xh3=xh3, rs3=rs3))
        x = x3

    grads = {k: [None] * DEPTH for k in WEIGHTS}
    s = saved[-1]
    sq, dz3, dg, db = _loss_lnbwd(x, target, s["xh3"], s["rs3"], row(small["ln3_g"][DEPTH - 1]), "loss_ln3_bwd")
    grad_x = None
    causal = jnp.tril(jnp.ones((CHUNK, CHUNK), dtype=bool))
    for l in reversed(range(DEPTH)):
        s = saved[l]
        big = sched.weights(l)
        grads["ln3_g"][l], grads["ln3_b"][l] = jnp.sum(dg, axis=0), jnp.sum(db, axis=0)
        dh = ride("ff2_bwd", l, _mm_nt, dz3, big["w_ff2"], BF16, f"ff2_bwd_{l}", relu2_grad_of=s["h"], tn=D_FF)
        grads["w_ff2"][l] = _mm_tn(s["h"], dz3, f"ff2_wgrad_{l}", relu2=True, t1=2048)
        grads["w_ff1"][l] = _mm_tn_by_owner(s["x2"], dh, f"ff1_wgrad_{l}")
        sched.note_grads(l, {k: grads[k][l] for k in ("w_ff1", "w_ff2")})
        dz2, dg, db, do = ride("ff1_bwd_ln2", l, _bwd_in, dz3, dh, big["w_ff1"], f"ff1_bwd_ln2_o_bwd_{l}",
                               ln=(s["xh2"], s["rs2"], row(small["ln2_g"][l])), then=big["w_o"])
        grads["ln2_g"][l], grads["ln2_b"][l] = jnp.sum(dg, axis=0), jnp.sum(db, axis=0)
        grads["w_o"][l] = _mm_tn(s["o"], dz2, f"o_proj_wgrad_{l}")
        dq, dkv = ride("attn_bwd", l, _attn_bwd, s["q"], s["kv"], do, bd, f"attn_bwd_{l}")
        grads["w_q"][l] = _mm_tn(s["x1"], dq, f"q_wgrad_{l}")
        grads["w_kv"][l] = _mm_tn_by_owner(mem, dkv, f"kv_wgrad_{l}")
        sched.note_grads(l, {k: grads[k][l] for k in ("w_o", "w_q", "w_kv")})
        dz1, dg, db, dcat = ride("q_bwd_ln1", l, _bwd_in, dz2, dq, big["w_q"], f"q_bwd_ln1_out_bwd_{l}",
                                 ln=(s["xh1"], s["rs1"], row(small["ln1_g"][l])), then=big["w_out"])
        grads["ln1_g"][l], grads["ln1_b"][l] = jnp.sum(dg, axis=0), jnp.sum(db, axis=0)
        grads["w_out"][l] = _mm_tn(s["cat"], dz1, f"out_proj_wgrad_{l}")
        sched.note_grads(l, {"w_out": grads["w_out"][l]})
        (dproj, dcaw, dcab, dlag, dlab, dlvg, dlvb, dws, dbs, dccw) = ride(
            "mixer_bwd", l, _mixer_bwd, s["proj"], dcat, s["ca"], s["mp"], bd, f"mixer_bwd_{l}")
        grads["conv_a_w"][l] = jnp.sum(dcaw, axis=0)[:KA]
        grads["conv_a_b"][l] = jnp.sum(dcab, axis=(0, 1))
        grads["ln_a_g"][l] = jnp.sum(dlag, axis=(0, 1))
        grads["ln_a_b"][l] = jnp.sum(dlab, axis=(0, 1))
        grads["ln_v_g"][l] = jnp.sum(dlvg, axis=(0, 1))
        grads["ln_v_b"][l] = jnp.sum(dlvb, axis=(0, 1))
        grads["w_s"][l] = jnp.where(causal[None], jnp.sum(dws, axis=0), 0.0)
        grads["b_s"][l] = jnp.sum(dbs, axis=0)[:, :B_HEADS].T
        grads["conv_c_w"][l] = jnp.sum(dccw, axis=0)[:KC]
        grads["w_in"][l] = _mm_tn(s["x"], dproj, f"in_proj_wgrad_{l}")
        sched.note_grads(l, {k: v[l] for k, v in grads.items() if k not in ("w_ff1", "w_ff2")})
        if l > 0:
            p = saved[l - 1]
            dz3, dg, db = _bwd_in(dz1, dproj, big["w_in"], f"in_proj_bwd_ln3_{l}",
                                  ln=(p["xh3"], p["rs3"], row(small["ln3_g"][l - 1])))
        else:
            grad_x = ride("in_proj_bwd", 0, _bwd_in, dz1, dproj, big["w_in"], "in_proj_bwd_0")
    sched.finish()
    return sq, grad_x, grads


WEIGHTS = ("w_in", "conv_a_w", "conv_a_b", "ln_a_g", "ln_a_b", "ln_v_g", "ln_v_b", "w_s", "b_s", "conv_c_w", "w_out",
           "ln1_g", "ln1_b", "w_q", "w_kv", "w_o", "ln2_g", "ln2_b", "w_ff1", "w_ff2", "ln3_g", "ln3_b")
COL_SHARDED = ("w_in", "w_kv", "w_ff1")
ROW_SHARDED = ("w_out", "w_q", "w_o", "w_ff2")
BIG = COL_SHARDED + ROW_SHARDED
REPLICATED = tuple(k for k in WEIGHTS if k not in BIG and k not in ("conv_a_w", "conv_c_w"))
PACK_LANES = 128


CONV_ROWS = 32 + 8
GATHER_LAYER0 = {"first": ("w_in", "conv"), "in_proj": ("w_out", "w_q", "w_kv", "w_o"), "mixer_fwd": ("w_ff1", "w_ff2")}
GATHER_NEXT0 = {"out_proj_ln1": ("w_in", "conv", "w_out"), "ff1": ("w_ff1", "w_kv"), "ff2_ln3": ("w_ff2", "w_q", "w_o")}
GATHER_NEXT = {"in_proj": ("w_in", "conv"), "mixer_fwd": ("w_out", "w_q", "w_o"), "ff1": ("w_ff1", "w_kv"),
               "ff2_ln3": ("w_ff2",)}
GRADS_EARLY = ("w_ff1", "w_ff2")
GRADS_LATE = ("w_in", "w_kv", "w_out", "w_q", "w_o", "conv")
LATE_ON_FF1_BWD, LATE_ON_ATTN_BWD = ("w_kv", "w_out", "w_q", "w_o"), ("w_in", "conv")
GRADS_MID0, GRADS_OUT0, GRADS_LAST0 = ("w_kv", "w_q", "w_o"), ("w_out",), ("w_in", "conv")


def _gathered_to_full(g, col_sharded):
    _, a, b = g.shape
    if col_sharded:
        return g.transpose(1, 0, 2).reshape(a, N_DEV * b)
    return g.reshape(N_DEV * a, b)


def _full_to_owner_major(g, col_sharded):
    if g.ndim == 4:
        return g
    a, b = g.shape
    if col_sharded:
        return g.reshape(a, 4, 2, b // N_DEV).transpose(1, 2, 0, 3)
    return g.reshape(4, 2, a // N_DEV, b)


def _conv_pack(conv_a, conv_c):
    pad = lambda a, rows: jnp.pad(a, [(0, 0)] * (a.ndim - 2) + [(0, rows - a.shape[-2]), (0, 0)])
    return jnp.concatenate([pad(conv_a, 32), pad(conv_c, 8)], axis=-2)


def _conv_unpack(packed):
    return packed[..., :KA, :], packed[..., 32:32 + KC, :]


class _Overlapped(_Schedule):
    def __init__(self, shards_bf16, conv_shards, core):
        self.shards, self.conv_shards, self.core = shards_bf16, conv_shards, core
        self.full = {l: {} for l in range(DEPTH)}
        self.grads = {l: {} for l in range(DEPTH)}
        self.owner_major = {}
        self.from_sibling = {}
        self.scattering = None
        self.own, self.remote = {}, {}
        self.replicated = None

    def _gather(self, l, names):
        return _gather_rider([self.conv_shards[l] if k == "conv" else self.shards[k][l] for k in names])

    def _store(self, l, names, gathered):
        for k, g in zip(names, gathered):
            if k == "conv":
                self.full[l]["conv_a_w"], self.full[l]["conv_c_w"] = _conv_unpack(_gathered_to_full(g, True))
            else:
                self.full[l][k] = _gathered_to_full(g, k in COL_SHARDED)

    def weights(self, l):
        if l == 0 and not self.full[0]:
            names = GATHER_LAYER0["first"]
            self._store(0, names, _ride_alone(self._gather(0, names), "weights_all_gather_first"))
        return self.full[l]

    def note_grads(self, l, grads):
        self.grads[l].update(grads)

    def _owner_major(self, l, k):
        if (l, k) not in self.owner_major:
            g = self.grads[l]
            if k == "conv":
                self.owner_major[(l, k)] = _full_to_owner_major(_conv_pack(g["conv_a_w"], g["conv_c_w"]), True)
            else:
                self.owner_major[(l, k)] = _full_to_owner_major(g[k], k in COL_SHARDED)
        return self.owner_major[(l, k)]

    def _exchange(self, l, names):
        return _exchange_rider([self._owner_major(l, k) for k in names])

    def _scatter(self, groups):
        sums, sums_bf16, self.scattering = [], [], []
        for l, names in groups:
            for k in names:
                s, sb = _pair_add(self._owner_major(l, k), self.from_sibling.pop((l, k)), self.core,
                                  f"grad_pair_add_{l}_{k}")
                sums.append(s)
                sums_bf16.append(sb)
                self.scattering.append((l, k))
        return _scatter_rider(sums, sums_bf16)

    def _scattered(self, results):
        n = len(self.scattering)
        for i, key in enumerate(self.scattering):
            self.own[key], self.remote[key] = results[i], results[n + i]

    def _received(self, l, names, results):
        for k, r in zip(names, results):
            self.from_sibling[(l, k)] = r

    def rider(self, stage, l):
        if l == 0 and stage in ("in_proj", "mixer_fwd"):
            return self._gather(0, GATHER_LAYER0[stage])
        gather_next = GATHER_NEXT0 if l == 0 else GATHER_NEXT
        if stage in gather_next and l + 1 < DEPTH:
            return self._gather(l + 1, gather_next[stage])
        if stage == "ff2_bwd" and l + 1 < DEPTH:
            return self._exchange(l + 1, GRADS_LATE)
        if stage == "ff1_bwd_ln2" and l + 1 < DEPTH:
            return self._scatter([(l + 1, LATE_ON_FF1_BWD)])
        if stage == "attn_bwd" and l + 1 < DEPTH:
            return _join_riders([self._exchange(l, GRADS_EARLY), self._scatter([(l + 1, LATE_ON_ATTN_BWD)])])
        if stage == "attn_bwd":
            return self._exchange(l, GRADS_EARLY)
        if stage == "q_bwd_ln1" and l == 0:
            return self._exchange(0, GRADS_MID0)
        if stage == "mixer_bwd" and l == 0:
            return _join_riders([self._scatter([(0, GRADS_EARLY), (0, GRADS_MID0)]), self._exchange(0, GRADS_OUT0)])
        if stage == "mixer_bwd":
            return self._scatter([(l, GRADS_EARLY)])
        if stage == "in_proj_bwd":
            packed = _pack_rows([jnp.stack([self.grads[i][k] for i in range(DEPTH)]) for k in REPLICATED])
            return _gather_rider([packed])
        return None

    def rode(self, stage, l, results):
        gather_next = GATHER_NEXT0 if l == 0 else GATHER_NEXT
        if l == 0 and stage in ("in_proj", "mixer_fwd"):
            self._store(0, GATHER_LAYER0[stage], results)
        elif stage in gather_next:
            self._store(l + 1, gather_next[stage], results)
        elif stage == "ff2_bwd":
            self._received(l + 1, GRADS_LATE, results)
        elif stage == "ff1_bwd_ln2":
            self._scattered(results)
        elif stage == "attn_bwd":
            n = len(GRADS_EARLY)
            self._received(l, GRADS_EARLY, results[:n])
            if l + 1 < DEPTH:
                self._scattered(results[n:])
        elif stage == "q_bwd_ln1":
            self._received(0, GRADS_MID0, results)
        elif stage == "mixer_bwd" and l == 0:
            n = 2 * len(self.scattering)
            self._scattered(results[:n])
            self._received(0, GRADS_OUT0, results[n:])
        elif stage == "mixer_bwd":
            self._scattered(results)
        elif stage == "in_proj_bwd":
            self.replicated = results[0]

    def finish(self):
        self._received(0, GRADS_LAST0, _ride_alone(self._exchange(0, GRADS_LAST0), "grad_pair_exchange_last"))
        self._scattered(_ride_alone(self._scatter([(0, GRADS_OUT0), (0, GRADS_LAST0)]), "grad_chip_scatter_last"))


def _pack_rows(parts):
    flat = jnp.concatenate([p.reshape(-1, PACK_LANES) for p in parts], axis=0)
    return jnp.pad(flat, ((0, -flat.shape[0] % 8), (0, 0)))


def _unpack_rows(packed, like):
    out, r = [], 0
    for p in like:
        n = p.size // PACK_LANES
        out.append(packed[r:r + n].reshape(p.shape))
        r += n
    return out


def kernel(x, mem, w_in, conv_a_w, conv_a_b, ln_a_g, ln_a_b, ln_v_g, ln_v_b, w_s, b_s, conv_c_w, w_out, ln1_g, ln1_b, w_q, w_kv, w_o, ln2_g, ln2_b, w_ff1, w_ff2, ln3_g, ln3_b, loss_target, m_w_in, m_conv_a_w, m_conv_a_b, m_ln_a_g, m_ln_a_b, m_ln_v_g, m_ln_v_b, m_w_s, m_b_s, m_conv_c_w, m_w_out, m_ln1_g, m_ln1_b, m_w_q, m_w_kv, m_w_o, m_ln2_g, m_ln2_b, m_w_ff1, m_w_ff2, m_ln3_g, m_ln3_b, v_w_in, v_conv_a_w, v_conv_a_b, v_ln_a_g, v_ln_a_b, v_ln_v_g, v_ln_v_b, v_w_s, v_b_s, v_conv_c_w, v_w_out, v_ln1_g, v_ln1_b, v_w_q, v_w_kv, v_w_o, v_ln2_g, v_ln2_b, v_w_ff1, v_w_ff2, v_ln3_g, v_ln3_b):
    given = dict(locals())
    w = {k: given[k] for k in WEIGHTS}
    mom = {k: given["m_" + k] for k in WEIGHTS}
    var = {k: given["v_" + k] for k in WEIGHTS}
    bd, s_len, _ = x.shape
    core = lax.axis_index("c").astype(jnp.int32).reshape(1)

    conv_pack = lambda d: _conv_pack(d["conv_a_w"], d["conv_c_w"])
    sched = _Overlapped({k: w[k].astype(BF16) for k in BIG}, conv_pack(w), core)
    sq, grad_x, grads = _local_step(x.reshape(bd * s_len, D), mem.reshape(-1, D), loss_target.reshape(bd * s_len, D),
                                    sched, {k: w[k] for k in REPLICATED}, bd)
    loss = lax.psum(0.5 * jnp.sum(sq) / D, ("x", "y", "c"))

    out = {}
    for k in BIG + ("conv",):
        own = [sched.own[(l, k)] for l in range(DEPTH)]
        remote = [sched.remote[(l, k)] for l in range(DEPTH)]
        if k == "conv":
            conv_out = _adamw_layers(conv_pack(w), conv_pack(mom), conv_pack(var), own, remote, "adamw_conv")
            unpacked = [_conv_unpack(o) for o in conv_out]
            out["conv_a_w"], out["conv_c_w"] = [u[0] for u in unpacked], [u[1] for u in unpacked]
        else:
            out[k] = _adamw_layers(w[k], mom[k], var[k], own, remote, f"adamw_{k}")

    rep_out = _adamw(_pack_rows([w[k] for k in REPLICATED]), _pack_rows([mom[k] for k in REPLICATED]),
                     _pack_rows([var[k] for k in REPLICATED]), [sched.replicated], "adamw_replicated")
    for i, o in enumerate(rep_out):
        for k, piece in zip(REPLICATED, _unpack_rows(o, [w[k] for k in REPLICATED])):
            out.setdefault(k, [None] * 4)[i] = piece

    res = [loss, grad_x.reshape(bd, s_len, D)]
    for i in range(4):
        res += [out[k][i] for k in WEIGHTS]
    return tuple(res)
```
